```python
import math
import functools
import jax
import jax.numpy as jnp
from jax import lax
import numpy as np

D_MODEL = 1024
BATCH = 16
SEQ = 2048
DEPTH = 2

HEAD_DIM = 64
D_FF = 4 * D_MODEL
RMS_EPS = 1e-6

RWKV_HEADS = 8
RWKV_WIDTH = RWKV_HEADS * HEAD_DIM
RWKV_DECAY_LORA = 64
RWKV_A_LORA = 64
RWKV_GATE_LORA = 128
RWKV_GN_EPS = 6.4e-4
RWKV_SIZES = (RWKV_WIDTH, RWKV_WIDTH, RWKV_WIDTH, RWKV_DECAY_LORA, RWKV_A_LORA, RWKV_GATE_LORA)
RWKV_COLS = sum(RWKV_SIZES)

GDN_HEADS = 8
GDN_WIDTH = GDN_HEADS * HEAD_DIM
GDN_CONV = 4
GDN_CHUNK = 64
GDN_SIZES = (3 * GDN_WIDTH, GDN_HEADS, GDN_HEADS, GDN_WIDTH)
GDN_COLS = sum(GDN_SIZES)

EVEN_COLS = RWKV_COLS + GDN_COLS
EVEN_MIX_WIDTH = RWKV_WIDTH + GDN_WIDTH

DSA_HEADS = 8
DSA_KV_HEADS = 2
DSA_GROUP = DSA_HEADS // DSA_KV_HEADS
DSA_WIDTH = DSA_HEADS * HEAD_DIM
IDX_HEADS = 8
IDX_DIM = 64
DSA_TOPK_MAX = 256
DSA_Q_BLOCK = 128

MOBA_HEADS = 8
MOBA_WIDTH = MOBA_HEADS * HEAD_DIM
MOBA_BLOCK = 256
MOBA_TOPK = 3
MOBA_Q_BLOCK = 32

ODD_SIZES = (DSA_WIDTH, 2 * DSA_KV_HEADS * HEAD_DIM, IDX_HEADS * IDX_DIM, IDX_DIM, IDX_HEADS, MOBA_WIDTH, MOBA_WIDTH, MOBA_WIDTH)
ODD_COLS = sum(ODD_SIZES)
ODD_MIX_WIDTH = DSA_WIDTH + MOBA_WIDTH
ALIBI_HEADS = DSA_HEADS + MOBA_HEADS

kernel_name = 'hybrid_rwkv7_gdn_dsa_moba_trunk'


def split_cols(p, sizes):
    offs = np.cumsum((0,) + tuple(sizes))
    return [p[..., int(offs[i]):int(offs[i + 1])] for i in range(len(sizes))]


def rms_norm(x, g, eps=RMS_EPS):
    xf = x.astype(jnp.float32)
    y = xf * lax.rsqrt(jnp.mean(xf * xf, axis=-1, keepdims=True) + eps)
    return (y * g.astype(jnp.float32)).astype(x.dtype)


def layer_norm(x, w, b, eps=1e-6):
    xf = x.astype(jnp.float32)
    xc = xf - jnp.mean(xf, axis=-1, keepdims=True)
    y = xc * lax.rsqrt(jnp.mean(xc * xc, axis=-1, keepdims=True) + eps)
    return (y * w.astype(jnp.float32) + b.astype(jnp.float32)).astype(x.dtype)


def l2_normalize(x, eps=1e-6):
    xf = x.astype(jnp.float32)
    return (xf * lax.rsqrt(jnp.sum(xf * xf, axis=-1, keepdims=True) + eps)).astype(x.dtype)


def token_shift(x):
    return jnp.pad(x, ((0, 0), (1, 0), (0, 0)))[:, :-1]


def causal_depthwise_conv(x, w):
    K, C = w.shape
    return lax.conv_general_dilated(x, w[:, None, :].astype(x.dtype), window_strides=(1,), padding=[(K - 1, 0)], dimension_numbers=('NWC', 'WIO', 'NWC'), feature_group_count=C)


def alibi_slopes(n):
    return 2.0 ** (-8.0 * jnp.arange(1, n + 1, dtype=jnp.float32) / n)


def rwkv7_scan(r, log_w, k, v, a, b):
    B, T, H, N = r.shape

    def step(S, inp):
        r_t, w_t, k_t, v_t, a_t, b_t = inp
        sa = jnp.einsum('bhvk,bhk->bhv', S, a_t)
        S = S * w_t[:, :, None, :] + sa[..., None] * b_t[:, :, None, :] + v_t[..., None] * k_t[:, :, None, :]
        return S, jnp.einsum('bhvk,bhk->bhv', S, r_t)

    xs = tuple(jnp.moveaxis(t.astype(jnp.float32), 1, 0) for t in (r, jnp.exp(log_w), k, v, a, b))
    S0 = jnp.zeros((B, H, N, N), jnp.float32)
    _, o = lax.scan(step, S0, xs)
    return jnp.moveaxis(o, 0, 1)


def rwkv7_mix(p, mu, w0, w2, a0, a2, g2, k_k, k_a, r_k, lnx_w, lnx_b):
    B, T, _ = p.shape
    H, N = RWKV_HEADS, HEAD_DIM
    f32 = jnp.float32
    p = p + (token_shift(p) - p) * mu
    r, k, v, wd, ad, gd = split_cols(p, RWKV_SIZES)
    w_pre = (w0 + jnp.tanh(wd) @ w2).astype(f32)
    log_decay = -jnp.exp(-jax.nn.softplus(-w_pre) - 0.5)
    a = jax.nn.sigmoid(a0 + ad @ a2)
    g = jax.nn.sigmoid(gd) @ g2
    kk = l2_normalize((k * k_k).reshape(B, T, H, N))
    k = k * (1.0 + (a - 1.0) * k_a)

    def heads(t):
        return t.reshape(B, T, H, N)

    r_h, k_h, v_h, a_h = heads(r), heads(k), heads(v), heads(a)
    o = rwkv7_scan(r_h, heads(log_decay), k_h, v_h, -kk, kk * a_h)
    oc = o - jnp.mean(o, axis=-1, keepdims=True)
    o = oc * lax.rsqrt(jnp.mean(oc * oc, axis=-1, keepdims=True) + RWKV_GN_EPS)
    o = o.reshape(B, T, RWKV_WIDTH) * lnx_w.astype(f32) + lnx_b.astype(f32)
    bonus = jnp.sum((r_h * k_h * r_k).astype(f32), axis=-1, keepdims=True) * v_h.astype(f32)
    o = o + bonus.reshape(B, T, RWKV_WIDTH)
    return (o * g.astype(f32)).astype(p.dtype)


def chunk_gated_delta_rule(q, k, v, g, beta):
    B, T, H, N = q.shape
    C = GDN_CHUNK
    f32 = jnp.float32
    pad = (-T) % C

    def chunks(t):
        t = jnp.pad(t.astype(f32), [(0, 0), (0, pad)] + [(0, 0)] * (t.ndim - 2))
        t = jnp.moveaxis(t, 2, 1)
        return t.reshape((B, H, -1, C) + t.shape[3:])

    q = chunks(q) * (N ** -0.5)
    k, v = chunks(k), chunks(v)
    g = jnp.cumsum(chunks(g), axis=-1)
    beta = chunks(beta)
    causal = jnp.tril(jnp.ones((C, C), dtype=bool))
    strict = jnp.tril(jnp.ones((C, C), dtype=bool), -1)
    decay = jnp.exp(jnp.where(causal, g[..., :, None] - g[..., None, :], -jnp.inf))
    k_beta = k * beta[..., None]
    a_mat = jnp.where(strict, jnp.einsum('bhnik,bhnjk->bhnij', k_beta, k) * decay, 0.0) + jnp.eye(C, dtype=f32)
    tri = functools.partial(lax.linalg.triangular_solve, left_side=True, lower=True, unit_diagonal=True)
    u = tri(a_mat, v * beta[..., None])
    w = tri(a_mat, k_beta * jnp.exp(g)[..., None])
    qk = jnp.where(causal, jnp.einsum('bhnik,bhnjk->bhnij', q, k) * decay, 0.0)

    def step(S, inp):
        q_c, k_c, u_c, w_c, g_c, qk_c = inp
        v_new = u_c - jnp.einsum('bhck,bhkv->bhcv', w_c, S)
        o_c = jnp.einsum('bhck,bhkv->bhcv', q_c * jnp.exp(g_c)[..., None], S) + jnp.einsum('bhij,bhjv->bhiv', qk_c, v_new)
        g_last = g_c[..., -1:]
        S = S * jnp.exp(g_last)[..., None] + jnp.einsum('bhck,bhcv->bhkv', k_c * jnp.exp(g_last - g_c)[..., None], v_new)
        return S, o_c

    xs = tuple(jnp.moveaxis(t, 2, 0) for t in (q, k, u, w, g, qk))
    S0 = jnp.zeros((B, H, N, v.shape[-1]), f32)
    _, o = lax.scan(step, S0, xs)
    o = jnp.moveaxis(o, 0, 2).reshape(B, H, -1, N)[:, :, :T]
    return jnp.moveaxis(o, 1, 2)


def gated_deltanet_mix(p, conv_w, a_log, dt_bias, norm_w):
    B, T, _ = p.shape
    H, N = GDN_HEADS, HEAD_DIM
    f32 = jnp.float32
    qkv, b_in, a_in, z = split_cols(p, GDN_SIZES)
    qkv = jax.nn.silu(causal_depthwise_conv(qkv, conv_w))
    q, k, v = (t.reshape(B, T, H, N) for t in split_cols(qkv, (GDN_WIDTH, GDN_WIDTH, GDN_WIDTH)))
    q, k = l2_normalize(q), l2_normalize(k)
    beta = jax.nn.sigmoid(b_in.astype(f32))
    g = -jnp.exp(a_log.astype(f32)) * jax.nn.softplus(a_in.astype(f32) + dt_bias.astype(f32))
    o = chunk_gated_delta_rule(q, k, v, g, beta)
    o = rms_norm(o, norm_w) * jax.nn.silu(z.astype(f32)).reshape(B, T, H, N)
    return o.reshape(B, T, GDN_WIDTH).astype(p.dtype)


def dsa_attention(q, kv, iq, ik, iw, q_norm, k_norm, ln_w, ln_b, slopes):
    B, T, _ = q.shape
    H, KV, G, N = DSA_HEADS, DSA_KV_HEADS, DSA_GROUP, HEAD_DIM
    f32 = jnp.float32
    top_k = min(DSA_TOPK_MAX, T // 4)
    QB = DSA_Q_BLOCK if T % DSA_Q_BLOCK == 0 else T
    n_qb = T // QB
    q = rms_norm(q.reshape(B, T, H, N), q_norm)
    k_in, v_in = split_cols(kv, (KV * N, KV * N))
    k = rms_norm(k_in.reshape(B, T, KV, N), k_norm)
    v = v_in.reshape(B, T, KV, N)
    iq = iq.reshape(B, T, IDX_HEADS, IDX_DIM)
    ik = layer_norm(ik, ln_w, ln_b)
    iw = iw.astype(f32) * IDX_HEADS ** -0.5
    s_pos = jnp.arange(T)
    b_idx = jnp.arange(B)[:, None, None]
    m = slopes.reshape(KV, G)[None, None, :, :, None]

    def one_block(args):
        q_b, iq_b, iw_b, start = args
        t_pos = start + jnp.arange(QB)
        logits = jnp.einsum('bqhd,bsd->bqhs', iq_b, ik).astype(f32) * IDX_DIM ** -0.5
        score = jnp.einsum('bqh,bqhs->bqs', iw_b, jax.nn.relu(logits))
        score = jnp.where(s_pos[None, None, :] <= t_pos[None, :, None], score, -jnp.inf)
        _, idx = lax.top_k(score, top_k)
        k_sel = k[b_idx, idx]
        v_sel = v[b_idx, idx]
        dist = (t_pos[None, :, None] - idx).astype(f32)
        sc = jnp.einsum('bqgrn,bqsgn->bqgrs', q_b.reshape(B, QB, KV, G, N), k_sel).astype(f32) * N ** -0.5
        sc = sc - m * dist[:, :, None, None, :]
        sc = jnp.where((dist >= 0)[:, :, None, None, :], sc, -jnp.inf)
        pr = jax.nn.softmax(sc, axis=-1).astype(v.dtype)
        o = jnp.einsum('bqgrs,bqsgn->bqgrn', pr, v_sel)
        return o.reshape(B, QB, DSA_WIDTH)

    def blocks(t):
        return jnp.moveaxis(t.reshape((B, n_qb, QB) + t.shape[2:]), 1, 0)

    o = lax.map(one_block, (blocks(q), blocks(iq), blocks(iw), jnp.arange(n_qb) * QB))
    return jnp.moveaxis(o, 0, 1).reshape(B, T, DSA_WIDTH)


def moba_attention(q, k, v, q_norm, k_norm, slopes):
    B, T, _ = q.shape
    H, N, BS, QB = MOBA_HEADS, HEAD_DIM, MOBA_BLOCK, MOBA_Q_BLOCK
    f32 = jnp.float32
    n_kb = -(-T // BS)
    t_pad = n_kb * BS
    n_sel = min(MOBA_TOPK, n_kb - 1)
    pad = ((0, 0), (0, t_pad - T), (0, 0), (0, 0))
    q = jnp.pad(rms_norm(q.reshape(B, T, H, N), q_norm), pad)
    k = jnp.pad(rms_norm(k.reshape(B, T, H, N), k_norm), pad)
    v = jnp.pad(v.reshape(B, T, H, N), pad)
    k_bh = jnp.transpose(k.reshape(B, n_kb, BS, H, N), (0, 3, 1, 2, 4))
    v_bh = jnp.transpose(v.reshape(B, n_kb, BS, H, N), (0, 3, 1, 2, 4))
    k_mean = jnp.mean(k_bh.astype(f32), axis=3)
    scale = N ** -0.5
    b_idx = jnp.arange(B)[:, None, None, None]
    h_idx = jnp.arange(H)[None, None, :, None]

    def one_block(args):
        q_b, start = args
        t_pos = start + jnp.arange(QB)
        own = start // BS
        k_own = lax.dynamic_slice_in_dim(k, own * BS, BS, axis=1)
        v_own = lax.dynamic_slice_in_dim(v, own * BS, BS, axis=1)
        dist = (t_pos[:, None] - (own * BS + jnp.arange(BS))[None, :]).astype(f32)
        sc = jnp.einsum('bqhn,bshn->bqhs', q_b, k_own).astype(f32) * scale - slopes[None, None, :, None] * dist[None, :, None, :]
        sc = jnp.where((dist >= 0)[None, :, None, :], sc, -jnp.inf)
        if n_sel == 0:
            pr = jax.nn.softmax(sc, axis=-1).astype(v.dtype)
            return jnp.einsum('bqhs,bshn->bqhn', pr, v_own)
        gate = jnp.einsum('bqhn,bhjn->bqhj', q_b.astype(f32), k_mean)
        gate = jnp.where(jnp.arange(n_kb) < own, gate, -jnp.inf)
        _, sel = lax.top_k(gate, n_sel)
        k_sel = k_bh[b_idx, h_idx, sel]
        v_sel = v_bh[b_idx, h_idx, sel]
        sel_pos = sel[..., None] * BS + jnp.arange(BS)
        dist_sel = (t_pos[None, :, None, None, None] - sel_pos).astype(f32)
        sc_sel = jnp.einsum('bqhn,bqhjsn->bqhjs', q_b, k_sel).astype(f32) * scale - slopes[None, None, :, None, None] * dist_sel
        sc_sel = jnp.where((sel < own)[..., None], sc_sel, -jnp.inf)
        sc_all = jnp.concatenate([sc, sc_sel.reshape(B, QB, H, n_sel * BS)], axis=-1)
        pr = jax.nn.softmax(sc_all, axis=-1).astype(v.dtype)
        o = jnp.einsum('bqhs,bshn->bqhn', pr[..., :BS], v_own)
        o = o + jnp.einsum('bqhjs,bqhjsn->bqhn', pr[..., BS:].reshape(B, QB, H, n_sel, BS), v_sel)
        return o

    n_qb = t_pad // QB
    q_blocks = jnp.moveaxis(q.reshape(B, n_qb, QB, H, N), 1, 0)
    o = lax.map(one_block, (q_blocks, jnp.arange(n_qb) * QB))
    return jnp.moveaxis(o, 0, 1).reshape(B, t_pad, MOBA_WIDTH)[:, :T]


def even_mixer(h, w_in, w_out, mu, w0, w2, a0, a2, g2, k_k, k_a, r_k, lnx_w, lnx_b, conv_w, a_log, dt_bias, gdn_norm_w):
    p = h @ w_in
    o_a = rwkv7_mix(p[..., :RWKV_COLS], mu, w0, w2, a0, a2, g2, k_k, k_a, r_k, lnx_w, lnx_b)
    o_b = gated_deltanet_mix(p[..., RWKV_COLS:], conv_w, a_log, dt_bias, gdn_norm_w)
    return jnp.concatenate([o_a, o_b], axis=-1) @ w_out


def odd_mixer(h, w_in, w_out, dsa_q_norm, dsa_k_norm, idx_k_ln_w, idx_k_ln_b, moba_q_norm, moba_k_norm):
    p = h @ w_in
    dq, dkv, iq, ik, iw, mq, mk, mv = split_cols(p, ODD_SIZES)
    slopes = alibi_slopes(ALIBI_HEADS)
    o_c = dsa_attention(dq, dkv, iq, ik, iw, dsa_q_norm, dsa_k_norm, idx_k_ln_w, idx_k_ln_b, slopes[0::2])
    o_d = moba_attention(mq, mk, mv, moba_q_norm, moba_k_norm, slopes[1::2])
    return jnp.concatenate([o_c, o_d], axis=-1) @ w_out


def relu2_mlp(h, w1, w2):
    return jnp.square(jax.nn.relu(h @ w1)) @ w2


def setup_inputs(seed: int = 0) -> dict:
    key = jax.random.key(seed)
    keys = iter(jax.random.split(key, 64))
    f32 = jnp.float32

    def normal(shape, scale):
        return jax.random.normal(next(keys), shape, f32) * scale

    def uniform(shape, lo, hi):
        return jax.random.uniform(next(keys), shape, f32, lo, hi)

    ne = (DEPTH + 1) // 2
    no = DEPTH // 2
    dt = jnp.exp(uniform((ne, GDN_HEADS), math.log(1e-3), math.log(1e-1)))
    return {
        'x': normal((BATCH, SEQ, D_MODEL), 1.0),
        'norm1_g': 1.0 + normal((DEPTH, D_MODEL), 0.05),
        'norm2_g': 1.0 + normal((DEPTH, D_MODEL), 0.05),
        'mlp_w1': normal((DEPTH, D_MODEL, D_FF), D_MODEL ** -0.5),
        'mlp_w2': normal((DEPTH, D_FF, D_MODEL), D_FF ** -0.5),
        'ev_w_in': normal((ne, D_MODEL, EVEN_COLS), D_MODEL ** -0.5),
        'ev_w_out': normal((ne, EVEN_MIX_WIDTH, D_MODEL), EVEN_MIX_WIDTH ** -0.5),
        'rwkv_mu': uniform((ne, RWKV_COLS), 0.0, 1.0),
        'rwkv_w0': uniform((ne, RWKV_WIDTH), -6.5, -1.5),
        'rwkv_w2': normal((ne, RWKV_DECAY_LORA, RWKV_WIDTH), 0.5 * RWKV_DECAY_LORA ** -0.5),
        'rwkv_a0': normal((ne, RWKV_WIDTH), 0.1),
        'rwkv_a2': normal((ne, RWKV_A_LORA, RWKV_WIDTH), 0.5 * RWKV_A_LORA ** -0.5),
        'rwkv_g2': normal((ne, RWKV_GATE_LORA, RWKV_WIDTH), RWKV_GATE_LORA ** -0.5),
        'rwkv_k_k': 0.85 + normal((ne, RWKV_WIDTH), 0.05),
        'rwkv_k_a': 1.0 + normal((ne, RWKV_WIDTH), 0.05),
        'rwkv_r_k': normal((ne, RWKV_HEADS, HEAD_DIM), 0.1),
        'rwkv_lnx_w': 1.0 + normal((ne, RWKV_WIDTH), 0.05),
        'rwkv_lnx_b': normal((ne, RWKV_WIDTH), 0.02),
        'gdn_conv_w': normal((ne, GDN_CONV, 3 * GDN_WIDTH), GDN_CONV ** -0.5),
        'gdn_a_log': jnp.log(uniform((ne, GDN_HEADS), 1.0, 16.0)),
        'gdn_dt_bias': dt + jnp.log(-jnp.expm1(-dt)),
        'gdn_norm_w': 1.0 + normal((ne, HEAD_DIM), 0.05),
        'od_w_in': normal((no, D_MODEL, ODD_COLS), D_MODEL ** -0.5),
        'od_w_out': normal((no, ODD_MIX_WIDTH, D_MODEL), ODD_MIX_WIDTH ** -0.5),
        'dsa_q_norm': 1.0 + normal((no, HEAD_DIM), 0.05),
        'dsa_k_norm': 1.0 + normal((no, HEAD_DIM), 0.05),
        'idx_k_ln_w': 1.0 + normal((no, IDX_DIM), 0.05),
        'idx_k_ln_b': normal((no, IDX_DIM), 0.02),
        'moba_q_norm': 1.0 + normal((no, HEAD_DIM), 0.05),
        'moba_k_norm': 1.0 + normal((no, HEAD_DIM), 0.05),
    }


def reference(x, norm1_g, norm2_g, mlp_w1, mlp_w2, ev_w_in, ev_w_out, rwkv_mu, rwkv_w0, rwkv_w2, rwkv_a0, rwkv_a2, rwkv_g2, rwkv_k_k, rwkv_k_a, rwkv_r_k, rwkv_lnx_w, rwkv_lnx_b, gdn_conv_w, gdn_a_log, gdn_dt_bias, gdn_norm_w, od_w_in, od_w_out, dsa_q_norm, dsa_k_norm, idx_k_ln_w, idx_k_ln_b, moba_q_norm, moba_k_norm):
    for i in range(DEPTH):
        j = i // 2
        h = rms_norm(x, norm1_g[i])
        if i % 2 == 0:
            mix = even_mixer(h, ev_w_in[j], ev_w_out[j], rwkv_mu[j], rwkv_w0[j], rwkv_w2[j], rwkv_a0[j], rwkv_a2[j], rwkv_g2[j], rwkv_k_k[j], rwkv_k_a[j], rwkv_r_k[j], rwkv_lnx_w[j], rwkv_lnx_b[j], gdn_conv_w[j], gdn_a_log[j], gdn_dt_bias[j], gdn_norm_w[j])
        else:
            mix = odd_mixer(h, od_w_in[j], od_w_out[j], dsa_q_norm[j], dsa_k_norm[j], idx_k_ln_w[j], idx_k_ln_b[j], moba_q_norm[j], moba_k_norm[j])
        x = x + mix.astype(x.dtype)
        x = x + relu2_mlp(rms_norm(x, norm2_g[i]), mlp_w1[i], mlp_w2[i]).astype(x.dtype)
    return x
```

```python
import functools
import math

import jax
import jax.numpy as jnp
from jax import lax
from jax.experimental import pallas as pl
from jax.experimental.pallas import tpu as pltpu

F32 = jnp.float32
MXU_DTYPE = jnp.bfloat16
HIGHEST = lax.Precision.HIGHEST

LANES = 128
VMEM_LIMIT = 56 * 1024 * 1024

HEAD_DIM = 64
N_HEADS = 8
WIDTH = N_HEADS * HEAD_DIM
RMS_EPS = 1e-6

RWKV_DECAY_LORA = 64
RWKV_A_LORA = 64
RWKV_GATE_LORA = 128
RWKV_COLS = 3 * WIDTH + RWKV_DECAY_LORA + RWKV_A_LORA + RWKV_GATE_LORA
RWKV_GN_EPS = 6.4e-4

GDN_CONV = 4
GDN_COLS = 3 * WIDTH + WIDTH + LANES
SCAN_CHUNK = 64
SCAN_BLOCK = 256

DSA_KV_HEADS = 2
DSA_GROUP = N_HEADS // DSA_KV_HEADS
IDX_HEADS = 8
IDX_DIM = 64
DSA_TOPK_MAX = 256
DSA_Q_BLOCK = 128
DSA_KEY_CHUNK = 128
DSA_COLS = 2 * WIDTH + 2 * DSA_KV_HEADS * HEAD_DIM + LANES

MOBA_BLOCK = 256
MOBA_TOPK = 3
MOBA_COLS = 3 * WIDTH

ALIBI_HEADS = 2 * N_HEADS
NEG_INF = float("-inf")


def _alibi_slope(i):
    return 2.0 ** (-8.0 * (i + 1) / ALIBI_HEADS)


DSA_SLOPES = tuple(_alibi_slope(2 * h) for h in range(N_HEADS))
MOBA_SLOPES = tuple(_alibi_slope(2 * h + 1) for h in range(N_HEADS))


def _mm(a, b):
    return jnp.dot(a.astype(MXU_DTYPE), b.astype(MXU_DTYPE), preferred_element_type=F32)


def _mm32(a, b):
    return jnp.dot(a, b, preferred_element_type=F32, precision=HIGHEST)


def _mm32_nt(a, b):
    return lax.dot_general(a, b, (((1,), (1,)), ((), ())), preferred_element_type=F32, precision=HIGHEST)


def _iota(shape, dim):
    return lax.broadcasted_iota(jnp.int32, shape, dim)


def _sigmoid(x):
    return 1.0 / (1.0 + jnp.exp(-x))


def _silu(x):
    return x * _sigmoid(x)


def _softplus(x):
    return jnp.maximum(x, 0.0) + jnp.log1p(jnp.exp(-jnp.abs(x)))


def _head(x, h):
    return x[:, h * HEAD_DIM:(h + 1) * HEAD_DIM]


def _head_rms(x, w):
    return x * lax.rsqrt(jnp.mean(x * x, axis=-1, keepdims=True) + RMS_EPS) * w


def _norm_proj_kernel(x_ref, g_ref, *refs):
    n = len(refs) // 2
    x = x_ref[...]
    h = x * lax.rsqrt(jnp.mean(x * x, axis=-1, keepdims=True) + RMS_EPS) * g_ref[...]
    h = h.astype(MXU_DTYPE)
    for w_ref, o_ref in zip(refs[:n], refs[n:]):
        o_ref[...] = jnp.dot(h, w_ref[...], preferred_element_type=F32)


def _norm_proj(x, g, ws, tm=256):
    n, d = x.shape
    assert n % tm == 0
    const = lambda i: (0, 0)
    return pl.pallas_call(
        _norm_proj_kernel,
        grid=(n // tm,),
        in_specs=[pl.BlockSpec((tm, d), lambda i: (i, 0)), pl.BlockSpec((1, d), const)]
        + [pl.BlockSpec(w.shape, const) for w in ws],
        out_specs=[pl.BlockSpec((tm, w.shape[1]), lambda i: (i, 0)) for w in ws],
        out_shape=[jax.ShapeDtypeStruct((n, w.shape[1]), F32) for w in ws],
        compiler_params=pltpu.CompilerParams(dimension_semantics=("parallel",), vmem_limit_bytes=VMEM_LIMIT),
        name="norm_proj",
    )(x, g.reshape(1, d), *ws)


def _out_proj_kernel(x_ref, a_ref, b_ref, wa_ref, wb_ref, o_ref):
    o_ref[...] = x_ref[...] + _mm(a_ref[...], wa_ref[...]) + _mm(b_ref[...], wb_ref[...])


def _out_proj(x, a, b, w_out, tm=512):
    n, d = x.shape
    wa = w_out[:a.shape[1]].astype(MXU_DTYPE)
    wb = w_out[a.shape[1]:].astype(MXU_DTYPE)
    assert n % tm == 0
    row = lambda i: (i, 0)
    const = lambda i: (0, 0)
    return pl.pallas_call(
        _out_proj_kernel,
        grid=(n // tm,),
        in_specs=[pl.BlockSpec((tm, d), row), pl.BlockSpec((tm, a.shape[1]), row), pl.BlockSpec((tm, b.shape[1]), row),
                  pl.BlockSpec(wa.shape, const), pl.BlockSpec(wb.shape, const)],
        out_specs=pl.BlockSpec((tm, d), row),
        out_shape=jax.ShapeDtypeStruct((n, d), F32),
        compiler_params=pltpu.CompilerParams(dimension_semantics=("parallel",), vmem_limit_bytes=VMEM_LIMIT),
        name="out_proj",
    )(x, a, b, wa, wb)


def _mlp_kernel(x_ref, g_ref, w1_ref, w2_ref, o_ref, h_ref, acc_ref):
    j = pl.program_id(1)

    @pl.when(j == 0)
    def _():
        x = x_ref[...]
        h = x * lax.rsqrt(jnp.mean(x * x, axis=-1, keepdims=True) + RMS_EPS) * g_ref[...]
        h_ref[...] = h.astype(MXU_DTYPE)
        acc_ref[...] = x

    u = jnp.maximum(jnp.dot(h_ref[...], w1_ref[...], preferred_element_type=F32), 0.0)
    acc_ref[...] += jnp.dot((u * u).astype(MXU_DTYPE), w2_ref[...], preferred_element_type=F32)

    @pl.when(j == pl.num_programs(1) - 1)
    def _():
        o_ref[...] = acc_ref[...]


def _mlp(x, g, w1, w2, tm=1024, tf=1024):
    n, d = x.shape
    f = w1.shape[1]
    tm = min(tm, n)
    tf = min(tf, f)
    assert n % tm == 0 and f % tf == 0
    return pl.pallas_call(
        _mlp_kernel,
        grid=(n // tm, f // tf),
        in_specs=[pl.BlockSpec((tm, d), lambda i, j: (i, 0)), pl.BlockSpec((1, d), lambda i, j: (0, 0)),
                  pl.BlockSpec((d, tf), lambda i, j: (0, j)), pl.BlockSpec((tf, d), lambda i, j: (j, 0))],
        out_specs=pl.BlockSpec((tm, d), lambda i, j: (i, 0)),
        out_shape=jax.ShapeDtypeStruct((n, d), F32),
        scratch_shapes=[pltpu.VMEM((tm, d), MXU_DTYPE), pltpu.VMEM((tm, d), F32)],
        compiler_params=pltpu.CompilerParams(dimension_semantics=("parallel", "arbitrary"),
                                             vmem_limit_bytes=VMEM_LIMIT),
        name="mlp",
    )(x, g.reshape(1, d), w1.astype(MXU_DTYPE), w2.astype(MXU_DTYPE))


def _bmm(a, b):
    return lax.dot_general(a.astype(MXU_DTYPE), b.astype(MXU_DTYPE), (((2,), (1,)), ((0,), (0,))),
                           preferred_element_type=F32)


def _bmm_nt(a, b):
    return lax.dot_general(a.astype(MXU_DTYPE), b.astype(MXU_DTYPE), (((2,), (2,)), ((0,), (0,))),
                           preferred_element_type=F32)


def _bmm_tn(a, b):
    return lax.dot_general(a.astype(MXU_DTYPE), b.astype(MXU_DTYPE), (((1,), (1,)), ((0,), (0,))),
                           preferred_element_type=F32)


def _stack_heads(x, nc, c):
    return jnp.stack([x[i * c:(i + 1) * c, h * HEAD_DIM:(h + 1) * HEAD_DIM] for i in range(nc) for h in range(N_HEADS)])


def _unstack_heads(y, nc):
    return jnp.concatenate([jnp.concatenate([y[i * N_HEADS + h] for h in range(N_HEADS)], axis=-1)
                            for i in range(nc)], axis=0)


def _batched_unit_lower_solve(x, y):
    c = x.shape[1]
    y = y + _bmm(x, y)
    p = 2
    while p < c:
        x = _bmm(x, x)
        y = y + _bmm(x, y)
        p *= 2
    return y


def _chunk_cumsum_rows(x, c):
    tri = (_iota((c, c), 0) >= _iota((c, c), 1)).astype(F32)
    return jnp.concatenate([_mm32(tri, x[i * c:(i + 1) * c]) for i in range(x.shape[0] // c)], axis=0)


def _chunk_cumsum_cols(x, c):
    tri = (_iota((c, c), 0) <= _iota((c, c), 1)).astype(F32)
    return jnp.concatenate([_mm32(x[:, i * c:(i + 1) * c], tri) for i in range(x.shape[1] // c)], axis=1)


def _run_chunks(st_ref, m, q, oq, o0, gamma, nc):
    s = st_ref[...]
    outs = []
    for i in range(nc):
        g = slice(i * N_HEADS, (i + 1) * N_HEADS)
        mo = _bmm(jnp.concatenate([m[g], oq[g]], axis=1), s)
        outs.append(mo[:, HEAD_DIM:] + o0[g])
        s = gamma[g] * s + mo[:, :HEAD_DIM] + q[g]
    st_ref[...] = s
    return jnp.concatenate(outs, axis=0)


def _rwkv_kernel(p_ref, mu_ref, w0_ref, w2_ref, a0_ref, a2_ref, g2_ref, kk_ref, ka_ref, rk_ref, lnw_ref, lnb_ref,
                 o_ref, prev_ref, st_ref):
    tb = p_ref.shape[1]
    c = SCAN_CHUNK
    nc = tb // c

    @pl.when(pl.program_id(1) == 0)
    def _():
        prev_ref[...] = jnp.zeros_like(prev_ref)
        st_ref[...] = jnp.zeros_like(st_ref)

    p = p_ref[0]
    shifted = jnp.where(_iota((tb, 1), 0) == 0, prev_ref[...], pltpu.roll(p, 1, 0))
    prev_ref[...] = p[tb - 1:tb, :]
    x = p + (shifted - p) * mu_ref[...]

    r = x[:, 0:WIDTH]
    k = x[:, WIDTH:2 * WIDTH]
    v = x[:, 2 * WIDTH:3 * WIDTH]
    off = 3 * WIDTH
    wd = x[:, off:off + RWKV_DECAY_LORA]
    ad = x[:, off + RWKV_DECAY_LORA:off + RWKV_DECAY_LORA + RWKV_A_LORA]
    gd = x[:, off + RWKV_DECAY_LORA + RWKV_A_LORA:]

    w_pre = w0_ref[...] + _mm(jnp.tanh(wd), w2_ref[...])
    log_w = -(_sigmoid(w_pre) * math.exp(-0.5))
    a = _sigmoid(a0_ref[...] + _mm(ad, a2_ref[...]))
    g = _mm(_sigmoid(gd), g2_ref[...])
    kk_all = k * kk_ref[...]
    k = k * (1.0 + (a - 1.0) * ka_ref[...])

    gam = _chunk_cumsum_rows(log_w, c)
    gam_last = jnp.concatenate([jnp.broadcast_to(gam[(i + 1) * c - 1:(i + 1) * c, :], (c, WIDTH)) for i in range(nc)],
                               axis=0)

    st = lambda y: _stack_heads(y, nc, c)
    kk = st(kk_all)
    kk = kk * lax.rsqrt(jnp.sum(kk * kk, axis=-1, keepdims=True) + 1e-6)
    r_h, k_h, v_h = st(r), st(k), st(v)
    b_h = kk * st(a)
    e_neg = st(jnp.exp(-gam))
    a_t = -kk * st(jnp.exp(gam - log_w))
    r_t = r_h * st(jnp.exp(gam))
    b_t = b_h * e_neg
    k_t = k_h * e_neg
    tail = st(jnp.exp(gam_last - gam))
    e_last = st(jnp.exp(gam_last))[:, 0:1, :]

    ri = _iota((c, c), 0)
    ci = _iota((c, c), 1)
    incl = (ri >= ci)[None]
    strict = (ri > ci)[None]
    eye = (ri == ci)[None]

    lhs = jnp.concatenate([a_t, r_t], axis=1)
    m_b = _bmm_nt(lhs, b_t)
    m_k = _bmm_nt(lhs, k_t)
    a_rb = jnp.where(incl, m_b[:, c:], 0.0)
    akv = _bmm(jnp.where(strict, m_k[:, :c], 0.0), v_h)
    wu = _batched_unit_lower_solve(jnp.where(strict, m_b[:, :c], 0.0), jnp.concatenate([a_t, akv], axis=-1))
    ro = _bmm(a_rb, wu)
    oq = r_t + ro[:, :, :HEAD_DIM]
    o0 = ro[:, :, HEAD_DIM:] + _bmm(jnp.where(incl, m_k[:, c:], 0.0), v_h)
    mq = _bmm_tn(b_h * tail, wu)
    m = mq[:, :, :HEAD_DIM]
    q = mq[:, :, HEAD_DIM:] + _bmm_tn(k_h * tail, v_h)
    gamma = jnp.sum(jnp.where(eye, e_last, 0.0), axis=2, keepdims=True)

    o = _run_chunks(st_ref, m, q, oq, o0, gamma, nc)

    oc = o - jnp.mean(o, axis=-1, keepdims=True)
    on = oc * lax.rsqrt(jnp.mean(oc * oc, axis=-1, keepdims=True) + RWKV_GN_EPS)
    rk = jnp.stack([rk_ref[:, h * HEAD_DIM:(h + 1) * HEAD_DIM] for h in range(N_HEADS)] * nc)
    bonus = jnp.sum(r_h * k_h * rk, axis=-1, keepdims=True) * v_h
    o_ref[0] = (_unstack_heads(on, nc) * lnw_ref[...] + lnb_ref[...] + _unstack_heads(bonus, nc)) * g


def _rwkv(p, mu, w0, w2, a0, a2, g2, k_k, k_a, r_k, lnx_w, lnx_b, tb=SCAN_BLOCK):
    b, t, cols = p.shape
    tb = min(tb, t)
    assert t % tb == 0 and tb % SCAN_CHUNK == 0 and cols == RWKV_COLS
    row = lambda v: v.reshape(1, -1).astype(F32)
    params = [row(mu), row(w0), w2.astype(MXU_DTYPE), row(a0), a2.astype(MXU_DTYPE), g2.astype(MXU_DTYPE),
              row(k_k), row(k_a), row(r_k), row(lnx_w), row(lnx_b)]
    const = lambda i, j: (0, 0)
    return pl.pallas_call(
        _rwkv_kernel,
        grid=(b, t // tb),
        in_specs=[pl.BlockSpec((1, tb, cols), lambda i, j: (i, j, 0))] + [pl.BlockSpec(q.shape, const) for q in params],
        out_specs=pl.BlockSpec((1, tb, WIDTH), lambda i, j: (i, j, 0)),
        out_shape=jax.ShapeDtypeStruct((b, t, WIDTH), F32),
        scratch_shapes=[pltpu.VMEM((1, cols), F32), pltpu.VMEM((N_HEADS, HEAD_DIM, HEAD_DIM), F32)],
        compiler_params=pltpu.CompilerParams(dimension_semantics=("parallel", "arbitrary"),
                                             vmem_limit_bytes=VMEM_LIMIT),
        name="rwkv7",
    )(p, *params)


def _gdn_kernel(p_ref, cw_ref, alog_r_ref, dt_r_ref, alog_c_ref, dt_c_ref, nw_ref, o_ref, tail_ref, st_ref):
    tb = p_ref.shape[1]
    c = SCAN_CHUNK
    nc = tb // c
    qkv_w = 3 * WIDTH

    @pl.when(pl.program_id(1) == 0)
    def _():
        tail_ref[...] = jnp.zeros_like(tail_ref)
        st_ref[...] = jnp.zeros_like(st_ref)

    p = p_ref[0]
    xin = p[:, :qkv_w]
    z = p[:, qkv_w:qkv_w + WIDTH]
    b_in = p[:, qkv_w + WIDTH:qkv_w + WIDTH + N_HEADS]
    a_in = p[:, qkv_w + WIDTH + N_HEADS:qkv_w + WIDTH + 2 * N_HEADS]

    tail = tail_ref[...]
    row8 = _iota((8, 1), 0)
    conv = xin * cw_ref[GDN_CONV - 1:GDN_CONV, :]
    for s in range(1, GDN_CONV):
        rolled = pltpu.roll(xin, s, 0)
        first = jnp.where(row8 < s, pltpu.roll(tail, s, 0), rolled[:8])
        shifted = jnp.concatenate([first, rolled[8:]], axis=0)
        conv = conv + shifted * cw_ref[GDN_CONV - 1 - s:GDN_CONV - s, :]
    tail_ref[...] = xin[tb - 8:, :]
    qkv = _silu(conv)

    beta = _sigmoid(b_in)
    g_col = -jnp.exp(alog_r_ref[...]) * _softplus(a_in + dt_r_ref[...])
    gc_col = _chunk_cumsum_rows(g_col, c)
    eye_h = (_iota((N_HEADS, N_HEADS), 0) == _iota((N_HEADS, N_HEADS), 1)).astype(F32)
    a_row = _mm32_nt(eye_h, a_in)
    g_row = -jnp.exp(alog_c_ref[...]) * _softplus(a_row + dt_c_ref[...])
    gc_row = _chunk_cumsum_cols(g_row, c)

    def per_head_cols(y):
        return jnp.stack([y[i * c:(i + 1) * c, h:h + 1] for i in range(nc) for h in range(N_HEADS)])

    gc = per_head_cols(gc_col)
    bt = per_head_cols(beta)
    g_rows = jnp.stack([gc_row[h:h + 1, i * c:(i + 1) * c] for i in range(nc) for h in range(N_HEADS)])

    st = lambda y: _stack_heads(y, nc, c)
    q = st(qkv[:, :WIDTH])
    k = st(qkv[:, WIDTH:2 * WIDTH])
    v = st(qkv[:, 2 * WIDTH:])
    q = q * lax.rsqrt(jnp.sum(q * q, axis=-1, keepdims=True) + 1e-6) * (HEAD_DIM ** -0.5)
    k = k * lax.rsqrt(jnp.sum(k * k, axis=-1, keepdims=True) + 1e-6)

    ri = _iota((c, c), 0)
    ci = _iota((c, c), 1)
    incl = (ri >= ci)[None]
    strict = (ri > ci)[None]
    decay = jnp.exp(jnp.where(incl, gc - g_rows, NEG_INF))
    kb = k * bt
    e_g = jnp.exp(gc)
    mm = _bmm_nt(jnp.concatenate([kb, q], axis=1), k)
    a_mat = jnp.where(strict, mm[:, :c] * decay, 0.0)
    qk = jnp.where(incl, mm[:, c:] * decay, 0.0)
    uw = _batched_unit_lower_solve(-a_mat, jnp.concatenate([v * bt, kb * e_g], axis=-1))
    qkuw = _bmm(qk, uw)
    o0 = qkuw[:, :, :HEAD_DIM]
    oq = q * e_g - qkuw[:, :, HEAD_DIM:]
    g_last = gc[:, c - 1:c, :]
    kuw = _bmm_tn(k * jnp.exp(g_last - gc), uw)
    o = _run_chunks(st_ref, -kuw[:, :, HEAD_DIM:], kuw[:, :, :HEAD_DIM], oq, o0, jnp.exp(g_last), nc)

    o = o * lax.rsqrt(jnp.mean(o * o, axis=-1, keepdims=True) + RMS_EPS) * nw_ref[...]
    o_ref[0] = _unstack_heads(o, nc) * _silu(z)


def _gdn(p, conv_w, a_log, dt_bias, norm_w, tb=SCAN_BLOCK):
    b, t, cols = p.shape
    tb = min(tb, t)
    assert t % tb == 0 and tb % SCAN_CHUNK == 0 and cols == GDN_COLS
    params = [conv_w.astype(F32), a_log.reshape(1, -1), dt_bias.reshape(1, -1), a_log.reshape(-1, 1),
              dt_bias.reshape(-1, 1), norm_w.reshape(1, -1)]
    const = lambda i, j: (0, 0)
    return pl.pallas_call(
        _gdn_kernel,
        grid=(b, t // tb),
        in_specs=[pl.BlockSpec((1, tb, cols), lambda i, j: (i, j, 0))] + [pl.BlockSpec(q.shape, const) for q in params],
        out_specs=pl.BlockSpec((1, tb, WIDTH), lambda i, j: (i, j, 0)),
        out_shape=jax.ShapeDtypeStruct((b, t, WIDTH), F32),
        scratch_shapes=[pltpu.VMEM((8, 3 * WIDTH), F32), pltpu.VMEM((N_HEADS, HEAD_DIM, HEAD_DIM), F32)],
        compiler_params=pltpu.CompilerParams(dimension_semantics=("parallel", "arbitrary"),
                                             vmem_limit_bytes=VMEM_LIMIT),
        name="gdn",
    )(p, *params)


def _key_to_float(key):
    return pltpu.bitcast(jnp.where(key >= 0, key, key ^ jnp.int32(0x7FFFFFFF)), F32)


NEG_INF_KEY = -2139095041
SOFTMAX_FLOOR = -1e30


def _split_hi_lo(x):
    hi = x.astype(MXU_DTYPE)
    lo = (x - hi.astype(F32)).astype(MXU_DTYPE)
    return hi, lo


def _dsa_kernel(dq_ref, iq_ref, kv_ref, ikw_ref, iwq_ref, qn_ref, kn_ref, lnw_ref, lnb_ref, o_ref,
                k_s, vt_s, ik_s, score_s, distm_s, m_s, l_s, acc_s, *, top_k):
    qb = dq_ref.shape[1]
    t_len = kv_ref.shape[1]
    kc = DSA_KEY_CHUNK
    n_chunks = t_len // kc
    kvw = DSA_KV_HEADS * HEAD_DIM
    i = pl.program_id(1)
    n_live = (i * qb) // kc + qb // kc

    @pl.when(i == 0)
    def _():
        kv = kv_ref[0]
        k_s[...] = jnp.concatenate([_head_rms(_head(kv, g), kn_ref[...]) for g in range(DSA_KV_HEADS)],
                                   axis=-1).astype(MXU_DTYPE)
        for c in range(n_chunks):
            vt_s[c] = kv[c * kc:(c + 1) * kc, kvw:].T.astype(MXU_DTYPE)
        ik = ikw_ref[0][:, :IDX_DIM]
        ikc = ik - jnp.mean(ik, axis=-1, keepdims=True)
        ik = ikc * lax.rsqrt(jnp.mean(ikc * ikc, axis=-1, keepdims=True) + 1e-6) * lnw_ref[...] + lnb_ref[...]
        hi, lo = _split_hi_lo(ik)
        ik_s[...] = jnp.concatenate([hi, hi, lo], axis=-1)
        score_s[...] = jnp.full(score_s.shape, NEG_INF, F32)

    t_row = i * qb + _iota((1, qb), 1)
    s_loc = _iota((kc, 1), 0)

    eye_h = (_iota((IDX_HEADS, IDX_HEADS), 0) == _iota((IDX_HEADS, IDX_HEADS), 1)).astype(F32)
    iw_t = _mm32_nt(eye_h, iwq_ref[0][:, IDX_DIM:IDX_DIM + IDX_HEADS]) * (IDX_HEADS ** -0.5 * IDX_DIM ** -0.5)
    iq = iq_ref[0]
    iq_cat = []
    for h in range(IDX_HEADS):
        hi, lo = _split_hi_lo(_head(iq, h))
        iq_cat.append(jnp.concatenate([hi, lo, hi], axis=-1))

    def score_chunk(c, carry):
        rows = pl.ds(pl.multiple_of(c * kc, kc), kc)
        ikc = ik_s[rows, :]
        sc = jnp.zeros((kc, qb), F32)
        for h in range(IDX_HEADS):
            dots = lax.dot_general(ikc, iq_cat[h], (((1,), (1,)), ((), ())), preferred_element_type=F32)
            sc = sc + iw_t[h:h + 1, :] * jnp.maximum(dots, 0.0)
        score_s[rows, :] = jnp.where(c * kc + s_loc <= t_row, sc, NEG_INF)
        return carry

    lax.fori_loop(0, n_live, score_chunk, 0)

    kf = float(top_k)
    grp = 4 * kc
    n_grp = (n_live * kc + grp - 1) // grp

    def count(pred):
        def body(g, acc):
            blk = score_s[pl.ds(pl.multiple_of(g * grp, grp), grp), :]
            return acc + jnp.sum(jnp.where(pred(blk), 1.0, 0.0), axis=0, keepdims=True)
        return lax.fori_loop(0, n_grp, body, jnp.zeros((1, qb), F32))

    int_min = jnp.int32(-2 ** 31)
    tau = jnp.where(count(lambda blk: blk >= 0.0) >= kf, jnp.int32(0), int_min)

    def bit_step(b, tau):
        cand = tau | (jnp.int32(1) << (30 - b))
        cand_f = _key_to_float(cand)
        return jnp.where(count(lambda blk: blk >= cand_f) >= kf, cand, tau)

    tau = lax.fori_loop(0, 31, bit_step, tau)
    tau_f = _key_to_float(jnp.maximum(tau, jnp.int32(NEG_INF_KEY)))
    need = kf - count(lambda blk: blk > tau_f)

    lower = (_iota((kc, kc), 0) >= _iota((kc, kc), 1)).astype(MXU_DTYPE)

    def select_chunk(c, run):
        rows = pl.ds(pl.multiple_of(c * kc, kc), kc)
        blk = score_s[rows, :]
        eq = blk == tau_f
        pref = run + jnp.dot(lower, jnp.where(eq, 1.0, 0.0).astype(MXU_DTYPE), preferred_element_type=F32)
        s_pos = c * kc + s_loc
        sel = ((blk > tau_f) | (eq & (pref <= need))) & (s_pos <= t_row)
        distm_s[rows, :] = jnp.where(sel, (t_row - s_pos).astype(F32), jnp.inf)
        return pref[kc - 1:kc, :]

    lax.fori_loop(0, n_live, select_chunk, jnp.zeros((1, qb), F32))

    dq = dq_ref[0]
    qs = [(_head_rms(_head(dq, h), qn_ref[...]) * (HEAD_DIM ** -0.5)).astype(MXU_DTYPE) for h in range(N_HEADS)]
    m_s[...] = jnp.full(m_s.shape, SOFTMAX_FLOOR, F32)
    l_s[...] = jnp.zeros(l_s.shape, F32)
    acc_s[...] = jnp.zeros(acc_s.shape, F32)

    def attend_chunk(c, carry):
        rows = pl.ds(pl.multiple_of(c * kc, kc), kc)
        kch = k_s[rows, :]
        dm = distm_s[rows, :]
        vt = vt_s[c]
        for h in range(N_HEADS):
            g = h // DSA_GROUP
            lanes = slice(g * HEAD_DIM, (g + 1) * HEAD_DIM)
            sc = lax.dot_general(kch[:, lanes], qs[h], (((1,), (1,)), ((), ())), preferred_element_type=F32)
            sc = sc - DSA_SLOPES[h] * dm
            m_old = m_s[h:h + 1, :]
            m_new = jnp.maximum(m_old, jnp.max(sc, axis=0, keepdims=True))
            alpha = jnp.exp(m_old - m_new)
            e = jnp.exp(sc - m_new)
            l_s[h:h + 1, :] = alpha * l_s[h:h + 1, :] + jnp.sum(e, axis=0, keepdims=True)
            acc_s[h] = alpha * acc_s[h] + jnp.dot(vt[lanes, :], e.astype(MXU_DTYPE), preferred_element_type=F32)
            m_s[h:h + 1, :] = m_new
        return carry

    lax.fori_loop(0, n_live, attend_chunk, 0)
    o_t = jnp.concatenate([acc_s[h] / l_s[h:h + 1, :] for h in range(N_HEADS)], axis=0)
    o_ref[0] = o_t.T


def _dsa(p, q_norm, k_norm, ln_w, ln_b):
    b, t, cols = p.shape
    qb = DSA_Q_BLOCK
    kc = DSA_KEY_CHUNK
    assert t % qb == 0 and cols == DSA_COLS and t % (4 * kc) == 0 and qb % kc == 0
    top_k = min(DSA_TOPK_MAX, t // 4)
    kvw = 2 * DSA_KV_HEADS * HEAD_DIM
    row = lambda v: v.reshape(1, -1).astype(F32)
    params = [row(q_norm), row(k_norm), row(ln_w), row(ln_b)]
    const = lambda i, j: (0, 0)
    return pl.pallas_call(
        functools.partial(_dsa_kernel, top_k=top_k),
        grid=(b, t // qb),
        in_specs=[pl.BlockSpec((1, qb, WIDTH), lambda i, j: (i, j, 0)),
                  pl.BlockSpec((1, qb, WIDTH), lambda i, j: (i, j, 1)),
                  pl.BlockSpec((1, t, kvw), lambda i, j: (i, 0, 2 * WIDTH // kvw)),
                  pl.BlockSpec((1, t, LANES), lambda i, j: (i, 0, (2 * WIDTH + kvw) // LANES)),
                  pl.BlockSpec((1, qb, LANES), lambda i, j: (i, j, (2 * WIDTH + kvw) // LANES))]
        + [pl.BlockSpec(q.shape, const) for q in params],
        out_specs=pl.BlockSpec((1, qb, WIDTH), lambda i, j: (i, j, 0)),
        out_shape=jax.ShapeDtypeStruct((b, t, WIDTH), F32),
        scratch_shapes=[pltpu.VMEM((t, DSA_KV_HEADS * HEAD_DIM), MXU_DTYPE),
                        pltpu.VMEM((t // kc, DSA_KV_HEADS * HEAD_DIM, kc), MXU_DTYPE),
                        pltpu.VMEM((t, 3 * IDX_DIM), MXU_DTYPE),
                        pltpu.VMEM((t, qb), F32),
                        pltpu.VMEM((t, qb), F32),
                        pltpu.VMEM((N_HEADS, qb), F32), pltpu.VMEM((N_HEADS, qb), F32),
                        pltpu.VMEM((N_HEADS, HEAD_DIM, qb), F32)],
        compiler_params=pltpu.CompilerParams(dimension_semantics=("parallel", "arbitrary"),
                                             vmem_limit_bytes=VMEM_LIMIT),
        name="dsa",
    )(p, p, p, p, p, *params)


def _moba_kernel(q_ref, k_ref, v_ref, qn_ref, kn_ref, o_ref, k_s, vt_s, km_s, sel_s):
    bs = q_ref.shape[1]
    t_len = k_ref.shape[1]
    n_kb = t_len // bs
    hp = pl.program_id(1)
    i = pl.program_id(2)

    @pl.when(i == 0)
    def _():
        kraw = k_ref[0]
        for hh in range(2):
            kn = _head_rms(_head(kraw, hh), kn_ref[...])
            k_s[:, hh * HEAD_DIM:(hh + 1) * HEAD_DIM] = kn.astype(MXU_DTYPE)
            means = [jnp.mean(kn[j * bs:(j + 1) * bs], axis=0, keepdims=True) for j in range(n_kb)]
            km_s[hh] = jnp.concatenate(means, axis=0)
        v = v_ref[0]
        for j in range(n_kb):
            vt_s[j] = v[j * bs:(j + 1) * bs, :].T.astype(MXU_DTYPE)

    qraw = q_ref[0]
    dist_own = (_iota((1, bs), 1) - _iota((bs, 1), 0)).astype(F32)
    jcol = _iota((n_kb, 1), 0)
    own = pl.multiple_of(i * bs, bs)
    outs = []
    for hh in range(2):
        slope = jnp.where(hp == 0, MOBA_SLOPES[hh], 0.0)
        for pair in range(1, N_HEADS // 2):
            slope = jnp.where(hp == pair, MOBA_SLOPES[2 * pair + hh], slope)
        lanes = slice(hh * HEAD_DIM, (hh + 1) * HEAD_DIM)
        q = _head_rms(_head(qraw, hh), qn_ref[...])
        gate = _mm32_nt(km_s[hh], q)
        gate = jnp.where(jcol < i, gate, NEG_INF)
        rank = jnp.zeros((n_kb, bs), F32)
        for j2 in range(n_kb):
            other = gate[j2:j2 + 1, :]
            beats = (other > gate) | ((other == gate) & (j2 < jcol))
            rank = rank + jnp.where(beats, 1.0, 0.0)
        sel_s[hh] = jnp.where((rank < float(MOBA_TOPK)) & (jcol < i), 0.0, jnp.inf)

        qs = (q * (HEAD_DIM ** -0.5)).astype(MXU_DTYPE)
        bias = slope * dist_own
        sc = lax.dot_general(k_s[pl.ds(own, bs), lanes], qs, (((1,), (1,)), ((), ())), preferred_element_type=F32)
        sc = jnp.where(dist_own >= 0, sc - bias, NEG_INF)
        m0 = jnp.max(sc, axis=0, keepdims=True)
        e = jnp.exp(sc - m0)
        l0 = jnp.sum(e, axis=0, keepdims=True)
        acc0 = jnp.dot(vt_s[i][lanes, :], e.astype(MXU_DTYPE), preferred_element_type=F32)

        def block_step(j, carry):
            m, l, acc = carry
            start = pl.multiple_of(j * bs, bs)
            shift = sel_s[hh, pl.ds(j, 1), :] + slope * ((i - j) * bs).astype(F32)
            s2 = lax.dot_general(k_s[pl.ds(start, bs), lanes], qs, (((1,), (1,)), ((), ())),
                                 preferred_element_type=F32)
            s2 = s2 - bias - shift
            m_new = jnp.maximum(m, jnp.max(s2, axis=0, keepdims=True))
            alpha = jnp.exp(m - m_new)
            e2 = jnp.exp(s2 - m_new)
            l = alpha * l + jnp.sum(e2, axis=0, keepdims=True)
            acc = alpha * acc + jnp.dot(vt_s[j][lanes, :], e2.astype(MXU_DTYPE), preferred_element_type=F32)
            return m_new, l, acc

        _, l, acc = lax.fori_loop(0, i, block_step, (m0, l0, acc0))
        outs.append(acc / l)
    o_ref[0] = jnp.concatenate(outs, axis=0).T


def _moba(p, q_norm, k_norm):
    b, t, cols = p.shape
    bs = MOBA_BLOCK
    assert t % bs == 0 and cols == MOBA_COLS
    n_pairs = WIDTH // LANES
    row = lambda v: v.reshape(1, -1).astype(F32)
    const = lambda i, h, j: (0, 0)
    return pl.pallas_call(
        _moba_kernel,
        grid=(b, n_pairs, t // bs),
        in_specs=[pl.BlockSpec((1, bs, LANES), lambda i, h, j: (i, j, h)),
                  pl.BlockSpec((1, t, LANES), lambda i, h, j: (i, 0, n_pairs + h)),
                  pl.BlockSpec((1, t, LANES), lambda i, h, j: (i, 0, 2 * n_pairs + h)),
                  pl.BlockSpec((1, HEAD_DIM), const), pl.BlockSpec((1, HEAD_DIM), const)],
        out_specs=pl.BlockSpec((1, bs, LANES), lambda i, h, j: (i, j, h)),
        out_shape=jax.ShapeDtypeStruct((b, t, WIDTH), F32),
        scratch_shapes=[pltpu.VMEM((t, LANES), MXU_DTYPE),
                        pltpu.VMEM((t // bs, LANES, bs), MXU_DTYPE),
                        pltpu.VMEM((2, t // bs, HEAD_DIM), F32),
                        pltpu.VMEM((2, t // bs, bs), F32)],
        compiler_params=pltpu.CompilerParams(dimension_semantics=("parallel", "parallel", "arbitrary"),
                                             vmem_limit_bytes=VMEM_LIMIT),
        name="moba",
    )(p, p, p, row(q_norm), row(k_norm))


def _pad_cols(w, n):
    return jnp.pad(w, ((0, 0), (0, n - w.shape[1])))


def _even_layer(x, b, t, norm_g, w_in, w_out, mu, w0, w2, a0, a2, g2, k_k, k_a, r_k, lnx_w, lnx_b,
                conv_w, a_log, dt_bias, gdn_norm_w):
    qkv_w = 3 * WIDTH
    w_rwkv = w_in[:, :RWKV_COLS]
    w_g = w_in[:, RWKV_COLS:]
    w_gdn = jnp.concatenate([w_g[:, :qkv_w], w_g[:, qkv_w + 2 * N_HEADS:],
                             _pad_cols(w_g[:, qkv_w:qkv_w + 2 * N_HEADS], LANES)], axis=1)
    p_rwkv, p_gdn = _norm_proj(x, norm_g, [w_rwkv.astype(MXU_DTYPE), w_gdn.astype(MXU_DTYPE)])
    o_a = _rwkv(p_rwkv.reshape(b, t, -1), mu, w0, w2, a0, a2, g2, k_k, k_a, r_k, lnx_w, lnx_b)
    o_b = _gdn(p_gdn.reshape(b, t, -1), conv_w, a_log, dt_bias, gdn_norm_w)
    return _out_proj(x, o_a.reshape(b * t, -1), o_b.reshape(b * t, -1), w_out)


def _odd_layer(x, b, t, norm_g, w_in, w_out, dsa_q_norm, dsa_k_norm, idx_ln_w, idx_ln_b, moba_q_norm, moba_k_norm):
    kvw = 2 * DSA_KV_HEADS * HEAD_DIM
    sizes = (WIDTH, kvw, IDX_HEADS * IDX_DIM, IDX_DIM, IDX_HEADS, WIDTH, WIDTH, WIDTH)
    offs = [0]
    for s in sizes:
        offs.append(offs[-1] + s)
    dq, dkv, iq, ik, iw, mq, mk, mv = (w_in[:, offs[n]:offs[n + 1]] for n in range(len(sizes)))
    w_dsa = jnp.concatenate([dq, iq, dkv, _pad_cols(jnp.concatenate([ik, iw], axis=1), LANES)], axis=1)
    w_moba = jnp.concatenate([mq, mk, mv], axis=1)
    p_dsa, p_moba = _norm_proj(x, norm_g, [w_dsa.astype(MXU_DTYPE), w_moba.astype(MXU_DTYPE)])
    o_c = _dsa(p_dsa.reshape(b, t, -1), dsa_q_norm, dsa_k_norm, idx_ln_w, idx_ln_b)
    o_d = _moba(p_moba.reshape(b, t, -1), moba_q_norm, moba_k_norm)
    return _out_proj(x, o_c.reshape(b * t, -1), o_d.reshape(b * t, -1), w_out)


def kernel(x, norm1_g, norm2_g, mlp_w1, mlp_w2, ev_w_in, ev_w_out, rwkv_mu, rwkv_w0, rwkv_w2, rwkv_a0, rwkv_a2, rwkv_g2, rwkv_k_k, rwkv_k_a, rwkv_r_k, rwkv_lnx_w, rwkv_lnx_b, gdn_conv_w, gdn_a_log, gdn_dt_bias, gdn_norm_w, od_w_in, od_w_out, dsa_q_norm, dsa_k_norm, idx_k_ln_w, idx_k_ln_b, moba_q_norm, moba_k_norm):
    b, t, d = x.shape
    depth = norm1_g.shape[0]
    h = x.reshape(b * t, d)
    for i in range(depth):
        j = i // 2
        if i % 2 == 0:
            h = _even_layer(h, b, t, norm1_g[i], ev_w_in[j], ev_w_out[j], rwkv_mu[j], rwkv_w0[j], rwkv_w2[j],
                            rwkv_a0[j], rwkv_a2[j], rwkv_g2[j], rwkv_k_k[j], rwkv_k_a[j], rwkv_r_k[j],
                            rwkv_lnx_w[j], rwkv_lnx_b[j], gdn_conv_w[j], gdn_a_log[j], gdn_dt_bias[j],
                            gdn_norm_w[j])
        else:
            h = _odd_layer(h, b, t, norm1_g[i], od_w_in[j], od_w_out[j], dsa_q_norm[j], dsa_k_norm[j],
                           idx_k_ln_w[j], idx_k_ln_b[j], moba_q_norm[j], moba_k_norm[j])
        h = _mlp(h, norm2_g[i], mlp_w1[i], mlp_w2[i])
    return h.reshape(b, t, d)
```

```python
import functools
import math

import jax
import jax.numpy as jnp
from jax import lax
from jax.experimental import pallas as pl
from jax.experimental.pallas import tpu as pltpu

F32 = jnp.float32
MXU_DTYPE = jnp.bfloat16
HIGHEST = lax.Precision.HIGHEST

LANES = 128
VMEM_LIMIT = 56 * 1024 * 1024

HEAD_DIM = 64
N_HEADS = 8
WIDTH = N_HEADS * HEAD_DIM
RMS_EPS = 1e-6

RWKV_DECAY_LORA = 64
RWKV_A_LORA = 64
RWKV_GATE_LORA = 128
RWKV_COLS = 3 * WIDTH + RWKV_DECAY_LORA + RWKV_A_LORA + RWKV_GATE_LORA
RWKV_GN_EPS = 6.4e-4

GDN_CONV = 4
GDN_COLS = 3 * WIDTH + WIDTH + LANES
SCAN_CHUNK = 64
SCAN_BLOCK = 256

DSA_KV_HEADS = 2
DSA_GROUP = N_HEADS // DSA_KV_HEADS
IDX_HEADS = 8
IDX_DIM = 64
DSA_TOPK_MAX = 256
DSA_Q_BLOCK = 128
DSA_KEY_CHUNK = 256
DSA_COUNT_ROWS = 256
DSA_COLS = 2 * WIDTH + 2 * DSA_KV_HEADS * HEAD_DIM + LANES

MOBA_BLOCK = 256
MOBA_TOPK = 3
MOBA_COLS = 3 * WIDTH

ALIBI_HEADS = 2 * N_HEADS
NEG_INF = float("-inf")


def _alibi_slope(i):
    return 2.0 ** (-8.0 * (i + 1) / ALIBI_HEADS)


DSA_SLOPES = tuple(_alibi_slope(2 * h) for h in range(N_HEADS))
MOBA_SLOPES = tuple(_alibi_slope(2 * h + 1) for h in range(N_HEADS))


def _mm(a, b):
    return jnp.dot(a.astype(MXU_DTYPE), b.astype(MXU_DTYPE), preferred_element_type=F32)


def _mm32(a, b):
    return jnp.dot(a, b, preferred_element_type=F32, precision=HIGHEST)


def _mm32_nt(a, b):
    return lax.dot_general(a, b, (((1,), (1,)), ((), ())), preferred_element_type=F32, precision=HIGHEST)


def _iota(shape, dim):
    return lax.broadcasted_iota(jnp.int32, shape, dim)


def _sigmoid(x):
    return 1.0 / (1.0 + jnp.exp(-x))


def _silu(x):
    return x * _sigmoid(x)


def _softplus(x):
    return jnp.maximum(x, 0.0) + jnp.log1p(jnp.exp(-jnp.abs(x)))


def _head(x, h):
    return x[:, h * HEAD_DIM:(h + 1) * HEAD_DIM]


def _head_rms(x, w):
    return x * lax.rsqrt(jnp.mean(x * x, axis=-1, keepdims=True) + RMS_EPS) * w


def _norm_proj_kernel(x_ref, g_ref, *refs):
    n = len(refs) // 2
    x = x_ref[...]
    h = x * lax.rsqrt(jnp.mean(x * x, axis=-1, keepdims=True) + RMS_EPS) * g_ref[...]
    h = h.astype(MXU_DTYPE)
    for w_ref, o_ref in zip(refs[:n], refs[n:]):
        o_ref[...] = jnp.dot(h, w_ref[...], preferred_element_type=F32)


def _norm_proj(x, g, ws, tm=256):
    n, d = x.shape
    assert n % tm == 0
    const = lambda i: (0, 0)
    return pl.pallas_call(
        _norm_proj_kernel,
        grid=(n // tm,),
        in_specs=[pl.BlockSpec((tm, d), lambda i: (i, 0)), pl.BlockSpec((1, d), const)]
        + [pl.BlockSpec(w.shape, const) for w in ws],
        out_specs=[pl.BlockSpec((tm, w.shape[1]), lambda i: (i, 0)) for w in ws],
        out_shape=[jax.ShapeDtypeStruct((n, w.shape[1]), F32) for w in ws],
        compiler_params=pltpu.CompilerParams(dimension_semantics=("parallel",), vmem_limit_bytes=VMEM_LIMIT),
        name="norm_proj",
    )(x, g.reshape(1, d), *ws)


def _out_proj_kernel(x_ref, a_ref, b_ref, wa_ref, wb_ref, o_ref):
    o_ref[...] = x_ref[...] + _mm(a_ref[...], wa_ref[...]) + _mm(b_ref[...], wb_ref[...])


def _out_proj(x, a, b, w_out, tm=512):
    n, d = x.shape
    wa = w_out[:a.shape[1]].astype(MXU_DTYPE)
    wb = w_out[a.shape[1]:].astype(MXU_DTYPE)
    assert n % tm == 0
    row = lambda i: (i, 0)
    const = lambda i: (0, 0)
    return pl.pallas_call(
        _out_proj_kernel,
        grid=(n // tm,),
        in_specs=[pl.BlockSpec((tm, d), row), pl.BlockSpec((tm, a.shape[1]), row), pl.BlockSpec((tm, b.shape[1]), row),
                  pl.BlockSpec(wa.shape, const), pl.BlockSpec(wb.shape, const)],
        out_specs=pl.BlockSpec((tm, d), row),
        out_shape=jax.ShapeDtypeStruct((n, d), F32),
        compiler_params=pltpu.CompilerParams(dimension_semantics=("parallel",), vmem_limit_bytes=VMEM_LIMIT),
        name="out_proj",
    )(x, a, b, wa, wb)


def _mlp_kernel(x_ref, g_ref, w1_ref, w2_ref, o_ref, h_ref, acc_ref):
    j = pl.program_id(1)

    @pl.when(j == 0)
    def _():
        x = x_ref[...]
        h = x * lax.rsqrt(jnp.mean(x * x, axis=-1, keepdims=True) + RMS_EPS) * g_ref[...]
        h_ref[...] = h.astype(MXU_DTYPE)
        acc_ref[...] = x

    u = jnp.maximum(jnp.dot(h_ref[...], w1_ref[...], preferred_element_type=F32), 0.0)
    acc_ref[...] += jnp.dot((u * u).astype(MXU_DTYPE), w2_ref[...], preferred_element_type=F32)

    @pl.when(j == pl.num_programs(1) - 1)
    def _():
        o_ref[...] = acc_ref[...]


def _mlp(x, g, w1, w2, tm=1024, tf=1024):
    n, d = x.shape
    f = w1.shape[1]
    tm = min(tm, n)
    tf = min(tf, f)
    assert n % tm == 0 and f % tf == 0
    return pl.pallas_call(
        _mlp_kernel,
        grid=(n // tm, f // tf),
        in_specs=[pl.BlockSpec((tm, d), lambda i, j: (i, 0)), pl.BlockSpec((1, d), lambda i, j: (0, 0)),
                  pl.BlockSpec((d, tf), lambda i, j: (0, j)), pl.BlockSpec((tf, d), lambda i, j: (j, 0))],
        out_specs=pl.BlockSpec((tm, d), lambda i, j: (i, 0)),
        out_shape=jax.ShapeDtypeStruct((n, d), F32),
        scratch_shapes=[pltpu.VMEM((tm, d), MXU_DTYPE), pltpu.VMEM((tm, d), F32)],
        compiler_params=pltpu.CompilerParams(dimension_semantics=("parallel", "arbitrary"),
                                             vmem_limit_bytes=VMEM_LIMIT),
        name="mlp",
    )(x, g.reshape(1, d), w1.astype(MXU_DTYPE), w2.astype(MXU_DTYPE))


def _bmm(a, b):
    return lax.dot_general(a.astype(MXU_DTYPE), b.astype(MXU_DTYPE), (((2,), (1,)), ((0,), (0,))),
                           preferred_element_type=F32)


def _bmm_nt(a, b):
    return lax.dot_general(a.astype(MXU_DTYPE), b.astype(MXU_DTYPE), (((2,), (2,)), ((0,), (0,))),
                           preferred_element_type=F32)


def _bmm_tn(a, b):
    return lax.dot_general(a.astype(MXU_DTYPE), b.astype(MXU_DTYPE), (((1,), (1,)), ((0,), (0,))),
                           preferred_element_type=F32)


def _stack_heads(x, nc, c):
    return jnp.stack([x[i * c:(i + 1) * c, h * HEAD_DIM:(h + 1) * HEAD_DIM] for i in range(nc) for h in range(N_HEADS)])


def _unstack_heads(y, nc):
    return jnp.concatenate([jnp.concatenate([y[i * N_HEADS + h] for h in range(N_HEADS)], axis=-1)
                            for i in range(nc)], axis=0)


def _batched_unit_lower_solve(x, y):
    c = x.shape[1]
    y = y + _bmm(x, y)
    p = 2
    while p < c:
        x = _bmm(x, x)
        y = y + _bmm(x, y)
        p *= 2
    return y


def _chunk_cumsum_rows(x, c):
    tri = (_iota((c, c), 0) >= _iota((c, c), 1)).astype(F32)
    return jnp.concatenate([_mm32(tri, x[i * c:(i + 1) * c]) for i in range(x.shape[0] // c)], axis=0)


def _chunk_cumsum_cols(x, c):
    tri = (_iota((c, c), 0) <= _iota((c, c), 1)).astype(F32)
    return jnp.concatenate([_mm32(x[:, i * c:(i + 1) * c], tri) for i in range(x.shape[1] // c)], axis=1)


def _run_chunks(st_ref, m, q, oq, o0, gamma, nc):
    s = st_ref[...]
    outs = []
    for i in range(nc):
        g = slice(i * N_HEADS, (i + 1) * N_HEADS)
        mo = _bmm(jnp.concatenate([m[g], oq[g]], axis=1), s)
        outs.append(mo[:, HEAD_DIM:] + o0[g])
        s = gamma[g] * s + mo[:, :HEAD_DIM] + q[g]
    st_ref[...] = s
    return jnp.concatenate(outs, axis=0)


def _rwkv_kernel(p_ref, mu_ref, w0_ref, w2_ref, a0_ref, a2_ref, g2_ref, kk_ref, ka_ref, rk_ref, lnw_ref, lnb_ref,
                 o_ref, prev_ref, st_ref):
    tb = p_ref.shape[1]
    c = SCAN_CHUNK
    nc = tb // c

    @pl.when(pl.program_id(1) == 0)
    def _():
        prev_ref[...] = jnp.zeros_like(prev_ref)
        st_ref[...] = jnp.zeros_like(st_ref)

    p = p_ref[0]
    shifted = jnp.where(_iota((tb, 1), 0) == 0, prev_ref[...], pltpu.roll(p, 1, 0))
    prev_ref[...] = p[tb - 1:tb, :]
    x = p + (shifted - p) * mu_ref[...]

    r = x[:, 0:WIDTH]
    k = x[:, WIDTH:2 * WIDTH]
    v = x[:, 2 * WIDTH:3 * WIDTH]
    off = 3 * WIDTH
    wd = x[:, off:off + RWKV_DECAY_LORA]
    ad = x[:, off + RWKV_DECAY_LORA:off + RWKV_DECAY_LORA + RWKV_A_LORA]
    gd = x[:, off + RWKV_DECAY_LORA + RWKV_A_LORA:]

    w_pre = w0_ref[...] + _mm(jnp.tanh(wd), w2_ref[...])
    log_w = -(_sigmoid(w_pre) * math.exp(-0.5))
    a = _sigmoid(a0_ref[...] + _mm(ad, a2_ref[...]))
    g = _mm(_sigmoid(gd), g2_ref[...])
    kk_all = k * kk_ref[...]
    k = k * (1.0 + (a - 1.0) * ka_ref[...])

    gam = _chunk_cumsum_rows(log_w, c)
    gam_last = jnp.concatenate([jnp.broadcast_to(gam[(i + 1) * c - 1:(i + 1) * c, :], (c, WIDTH)) for i in range(nc)],
                               axis=0)

    st = lambda y: _stack_heads(y, nc, c)
    kk = st(kk_all)
    kk = kk * lax.rsqrt(jnp.sum(kk * kk, axis=-1, keepdims=True) + 1e-6)
    r_h, k_h, v_h = st(r), st(k), st(v)
    b_h = kk * st(a)
    e_neg = st(jnp.exp(-gam))
    a_t = -kk * st(jnp.exp(gam - log_w))
    r_t = r_h * st(jnp.exp(gam))
    b_t = b_h * e_neg
    k_t = k_h * e_neg
    tail = st(jnp.exp(gam_last - gam))
    e_last = st(jnp.exp(gam_last))[:, 0:1, :]

    ri = _iota((c, c), 0)
    ci = _iota((c, c), 1)
    incl = (ri >= ci)[None]
    strict = (ri > ci)[None]
    eye = (ri == ci)[None]

    lhs = jnp.concatenate([a_t, r_t], axis=1)
    m_b = _bmm_nt(lhs, b_t)
    m_k = _bmm_nt(lhs, k_t)
    a_rb = jnp.where(incl, m_b[:, c:], 0.0)
    akv = _bmm(jnp.where(strict, m_k[:, :c], 0.0), v_h)
    wu = _batched_unit_lower_solve(jnp.where(strict, m_b[:, :c], 0.0), jnp.concatenate([a_t, akv], axis=-1))
    ro = _bmm(a_rb, wu)
    oq = r_t + ro[:, :, :HEAD_DIM]
    o0 = ro[:, :, HEAD_DIM:] + _bmm(jnp.where(incl, m_k[:, c:], 0.0), v_h)
    mq = _bmm_tn(b_h * tail, wu)
    m = mq[:, :, :HEAD_DIM]
    q = mq[:, :, HEAD_DIM:] + _bmm_tn(k_h * tail, v_h)
    gamma = jnp.sum(jnp.where(eye, e_last, 0.0), axis=2, keepdims=True)

    o = _run_chunks(st_ref, m, q, oq, o0, gamma, nc)

    oc = o - jnp.mean(o, axis=-1, keepdims=True)
    on = oc * lax.rsqrt(jnp.mean(oc * oc, axis=-1, keepdims=True) + RWKV_GN_EPS)
    rk = jnp.stack([rk_ref[:, h * HEAD_DIM:(h + 1) * HEAD_DIM] for h in range(N_HEADS)] * nc)
    bonus = jnp.sum(r_h * k_h * rk, axis=-1, keepdims=True) * v_h
    o_ref[0] = (_unstack_heads(on, nc) * lnw_ref[...] + lnb_ref[...] + _unstack_heads(bonus, nc)) * g


def _rwkv(p, mu, w0, w2, a0, a2, g2, k_k, k_a, r_k, lnx_w, lnx_b, tb=SCAN_BLOCK):
    b, t, cols = p.shape
    tb = min(tb, t)
    assert t % tb == 0 and tb % SCAN_CHUNK == 0 and cols == RWKV_COLS
    row = lambda v: v.reshape(1, -1).astype(F32)
    params = [row(mu), row(w0), w2.astype(MXU_DTYPE), row(a0), a2.astype(MXU_DTYPE), g2.astype(MXU_DTYPE),
              row(k_k), row(k_a), row(r_k), row(lnx_w), row(lnx_b)]
    const = lambda i, j: (0, 0)
    return pl.pallas_call(
        _rwkv_kernel,
        grid=(b, t // tb),
        in_specs=[pl.BlockSpec((1, tb, cols), lambda i, j: (i, j, 0))] + [pl.BlockSpec(q.shape, const) for q in params],
        out_specs=pl.BlockSpec((1, tb, WIDTH), lambda i, j: (i, j, 0)),
        out_shape=jax.ShapeDtypeStruct((b, t, WIDTH), F32),
        scratch_shapes=[pltpu.VMEM((1, cols), F32), pltpu.VMEM((N_HEADS, HEAD_DIM, HEAD_DIM), F32)],
        compiler_params=pltpu.CompilerParams(dimension_semantics=("parallel", "arbitrary"),
                                             vmem_limit_bytes=VMEM_LIMIT),
        name="rwkv7",
    )(p, *params)


def _gdn_kernel(p_ref, cw_ref, alog_r_ref, dt_r_ref, alog_c_ref, dt_c_ref, nw_ref, o_ref, tail_ref, st_ref):
    tb = p_ref.shape[1]
    c = SCAN_CHUNK
    nc = tb // c
    qkv_w = 3 * WIDTH

    @pl.when(pl.program_id(1) == 0)
    def _():
        tail_ref[...] = jnp.zeros_like(tail_ref)
        st_ref[...] = jnp.zeros_like(st_ref)

    p = p_ref[0]
    xin = p[:, :qkv_w]
    z = p[:, qkv_w:qkv_w + WIDTH]
    b_in = p[:, qkv_w + WIDTH:qkv_w + WIDTH + N_HEADS]
    a_in = p[:, qkv_w + WIDTH + N_HEADS:qkv_w + WIDTH + 2 * N_HEADS]

    tail = tail_ref[...]
    row8 = _iota((8, 1), 0)
    conv = xin * cw_ref[GDN_CONV - 1:GDN_CONV, :]
    for s in range(1, GDN_CONV):
        rolled = pltpu.roll(xin, s, 0)
        first = jnp.where(row8 < s, pltpu.roll(tail, s, 0), rolled[:8])
        shifted = jnp.concatenate([first, rolled[8:]], axis=0)
        conv = conv + shifted * cw_ref[GDN_CONV - 1 - s:GDN_CONV - s, :]
    tail_ref[...] = xin[tb - 8:, :]
    qkv = _silu(conv)

    beta = _sigmoid(b_in)
    g_col = -jnp.exp(alog_r_ref[...]) * _softplus(a_in + dt_r_ref[...])
    gc_col = _chunk_cumsum_rows(g_col, c)
    eye_h = (_iota((N_HEADS, N_HEADS), 0) == _iota((N_HEADS, N_HEADS), 1)).astype(F32)
    a_row = _mm32_nt(eye_h, a_in)
    g_row = -jnp.exp(alog_c_ref[...]) * _softplus(a_row + dt_c_ref[...])
    gc_row = _chunk_cumsum_cols(g_row, c)

    def per_head_cols(y):
        return jnp.stack([y[i * c:(i + 1) * c, h:h + 1] for i in range(nc) for h in range(N_HEADS)])

    gc = per_head_cols(gc_col)
    bt = per_head_cols(beta)
    g_rows = jnp.stack([gc_row[h:h + 1, i * c:(i + 1) * c] for i in range(nc) for h in range(N_HEADS)])

    st = lambda y: _stack_heads(y, nc, c)
    q = st(qkv[:, :WIDTH])
    k = st(qkv[:, WIDTH:2 * WIDTH])
    v = st(qkv[:, 2 * WIDTH:])
    q = q * lax.rsqrt(jnp.sum(q * q, axis=-1, keepdims=True) + 1e-6) * (HEAD_DIM ** -0.5)
    k = k * lax.rsqrt(jnp.sum(k * k, axis=-1, keepdims=True) + 1e-6)

    ri = _iota((c, c), 0)
    ci = _iota((c, c), 1)
    incl = (ri >= ci)[None]
    strict = (ri > ci)[None]
    decay = jnp.exp(jnp.where(incl, gc - g_rows, NEG_INF))
    kb = k * bt
    e_g = jnp.exp(gc)
    mm = _bmm_nt(jnp.concatenate([kb, q], axis=1), k)
    a_mat = jnp.where(strict, mm[:, :c] * decay, 0.0)
    qk = jnp.where(incl, mm[:, c:] * decay, 0.0)
    uw = _batched_unit_lower_solve(-a_mat, jnp.concatenate([v * bt, kb * e_g], axis=-1))
    qkuw = _bmm(qk, uw)
    o0 = qkuw[:, :, :HEAD_DIM]
    oq = q * e_g - qkuw[:, :, HEAD_DIM:]
    g_last = gc[:, c - 1:c, :]
    kuw = _bmm_tn(k * jnp.exp(g_last - gc), uw)
    o = _run_chunks(st_ref, -kuw[:, :, HEAD_DIM:], kuw[:, :, :HEAD_DIM], oq, o0, jnp.exp(g_last), nc)

    o = o * lax.rsqrt(jnp.mean(o * o, axis=-1, keepdims=True) + RMS_EPS) * nw_ref[...]
    o_ref[0] = _unstack_heads(o, nc) * _silu(z)


def _gdn(p, conv_w, a_log, dt_bias, norm_w, tb=SCAN_BLOCK):
    b, t, cols = p.shape
    tb = min(tb, t)
    assert t % tb == 0 and tb % SCAN_CHUNK == 0 and cols == GDN_COLS
    params = [conv_w.astype(F32), a_log.reshape(1, -1), dt_bias.reshape(1, -1), a_log.reshape(-1, 1),
              dt_bias.reshape(-1, 1), norm_w.reshape(1, -1)]
    const = lambda i, j: (0, 0)
    return pl.pallas_call(
        _gdn_kernel,
        grid=(b, t // tb),
        in_specs=[pl.BlockSpec((1, tb, cols), lambda i, j: (i, j, 0))] + [pl.BlockSpec(q.shape, const) for q in params],
        out_specs=pl.BlockSpec((1, tb, WIDTH), lambda i, j: (i, j, 0)),
        out_shape=jax.ShapeDtypeStruct((b, t, WIDTH), F32),
        scratch_shapes=[pltpu.VMEM((8, 3 * WIDTH), F32), pltpu.VMEM((N_HEADS, HEAD_DIM, HEAD_DIM), F32)],
        compiler_params=pltpu.CompilerParams(dimension_semantics=("parallel", "arbitrary"),
                                             vmem_limit_bytes=VMEM_LIMIT),
        name="gdn",
    )(p, *params)


def _key_to_float(key):
    return pltpu.bitcast(jnp.where(key >= 0, key, key ^ jnp.int32(0x7FFFFFFF)), F32)


NEG_INF_KEY = -2139095041
SOFTMAX_FLOOR = -1e30


def _split_hi_lo(x):
    hi = x.astype(MXU_DTYPE)
    lo = (x - hi.astype(F32)).astype(MXU_DTYPE)
    return hi, lo


def _tree_sum(parts):
    parts = list(parts)
    while len(parts) > 1:
        nxt = [parts[a] + parts[a + 1] for a in range(0, len(parts) - 1, 2)]
        if len(parts) % 2:
            nxt.append(parts[-1])
        parts = nxt
    return parts[0]


def _lane_tile(x, n):
    return jnp.concatenate([x] * n, axis=1)


def _dsa_kernel(dq_ref, iq_ref, kv_ref, ikw_ref, iwq_ref, qn_ref, kn_ref, lnw_ref, lnb_ref, o_ref,
                k_s, vt_s, ik_s, score_s, distm_s, *, top_k):
    qb = dq_ref.shape[1]
    t_len = kv_ref.shape[1]
    kc = DSA_KEY_CHUNK
    n_chunks = t_len // kc
    kvw = DSA_KV_HEADS * HEAD_DIM
    i = pl.program_id(1)
    n_live = (i * qb + qb + kc - 1) // kc

    @pl.when(i == 0)
    def _():
        kv = kv_ref[0]
        k_s[...] = jnp.concatenate([_head_rms(_head(kv, g), kn_ref[...]) for g in range(DSA_KV_HEADS)],
                                   axis=-1).astype(MXU_DTYPE)
        for c in range(n_chunks):
            vt_s[c] = kv[c * kc:(c + 1) * kc, kvw:].T.astype(MXU_DTYPE)
        ik = ikw_ref[0][:, :IDX_DIM]
        ikc = ik - jnp.mean(ik, axis=-1, keepdims=True)
        ik = ikc * lax.rsqrt(jnp.mean(ikc * ikc, axis=-1, keepdims=True) + 1e-6) * lnw_ref[...] + lnb_ref[...]
        hi, lo = _split_hi_lo(ik)
        ik_s[...] = jnp.concatenate([hi, hi, lo], axis=-1)
        score_s[...] = jnp.full(score_s.shape, NEG_INF, F32)

    t_row = i * qb + _iota((1, qb), 1)
    s_loc = _iota((kc, 1), 0)

    eye_h = (_iota((IDX_HEADS, IDX_HEADS), 0) == _iota((IDX_HEADS, IDX_HEADS), 1)).astype(F32)
    iw_t = _mm32_nt(eye_h, iwq_ref[0][:, IDX_DIM:IDX_DIM + IDX_HEADS]) * (IDX_HEADS ** -0.5 * IDX_DIM ** -0.5)
    iw_wide = jnp.concatenate([iw_t[h:h + 1, :] for h in range(IDX_HEADS)], axis=1)
    iq = iq_ref[0]
    iq_cat = []
    for h in range(IDX_HEADS):
        hi, lo = _split_hi_lo(_head(iq, h))
        iq_cat.append(jnp.concatenate([hi, lo, hi], axis=-1))
    iq_all = jnp.concatenate(iq_cat, axis=0)

    def score_chunk(c, carry):
        rows = pl.ds(pl.multiple_of(c * kc, kc), kc)
        dots = lax.dot_general(ik_s[rows, :], iq_all, (((1,), (1,)), ((), ())), preferred_element_type=F32)
        w = iw_wide * jnp.maximum(dots, 0.0)
        sc = _tree_sum([w[:, h * qb:(h + 1) * qb] for h in range(IDX_HEADS)])
        score_s[rows, :] = jnp.where(c * kc + s_loc <= t_row, sc, NEG_INF)
        return carry

    lax.fori_loop(0, n_live, score_chunk, 0)

    kf = float(top_k)
    grp = DSA_COUNT_ROWS
    n_grp = (n_live * kc + grp - 1) // grp

    def count(pred):
        def body(g, acc):
            blk = score_s[pl.ds(pl.multiple_of(g * grp, grp), grp), :]
            ones = jnp.where(pred(blk), 1.0, 0.0)
            return acc + _tree_sum([ones[r * 8:(r + 1) * 8] for r in range(grp // 8)])
        acc = lax.fori_loop(0, n_grp, body, jnp.zeros((8, qb), F32))
        return jnp.sum(acc, axis=0, keepdims=True)

    int_min = jnp.int32(-2 ** 31)
    tau = jnp.where(count(lambda blk: blk >= 0.0) >= kf, jnp.int32(0), int_min)

    def bit_step(b, tau):
        cand = tau | (jnp.int32(1) << (30 - b))
        cand_f = _key_to_float(cand)
        return jnp.where(count(lambda blk: blk >= cand_f) >= kf, cand, tau)

    tau = lax.fori_loop(0, 31, bit_step, tau)
    tau_f = _key_to_float(jnp.maximum(tau, jnp.int32(NEG_INF_KEY)))
    need = kf - count(lambda blk: blk > tau_f)

    n_ge = count(lambda blk: blk >= tau_f)
    no_partial_tie = jnp.min(jnp.where((n_ge <= kf) | (tau_f == NEG_INF), 1.0, 0.0)) > 0.5

    def masked_distance(c, sel_fn):
        rows = pl.ds(pl.multiple_of(c * kc, kc), kc)
        s_pos = c * kc + s_loc
        sel = sel_fn(score_s[rows, :]) & (s_pos <= t_row)
        distm_s[rows, :] = jnp.where(sel, (t_row - s_pos).astype(F32), jnp.inf)

    @pl.when(no_partial_tie)
    def _():
        def select_chunk(c, carry):
            masked_distance(c, lambda blk: blk >= tau_f)
            return carry
        lax.fori_loop(0, n_live, select_chunk, 0)

    @pl.when(jnp.logical_not(no_partial_tie))
    def _():
        lower = (_iota((kc, kc), 0) >= _iota((kc, kc), 1)).astype(MXU_DTYPE)

        def select_chunk(c, run):
            def sel_fn(blk):
                eq = blk == tau_f
                pref = run + jnp.dot(lower, jnp.where(eq, 1.0, 0.0).astype(MXU_DTYPE), preferred_element_type=F32)
                return (blk > tau_f) | (eq & (pref <= need))
            masked_distance(c, sel_fn)
            rows = pl.ds(pl.multiple_of(c * kc, kc), kc)
            return run + jnp.sum(jnp.where(score_s[rows, :] == tau_f, 1.0, 0.0), axis=0, keepdims=True)
        lax.fori_loop(0, n_live, select_chunk, jnp.zeros((1, qb), F32))

    dq = dq_ref[0]
    qs = [(_head_rms(_head(dq, h), qn_ref[...]) * (HEAD_DIM ** -0.5)).astype(MXU_DTYPE) for h in range(N_HEADS)]
    q_grp = [jnp.concatenate(qs[g * DSA_GROUP:(g + 1) * DSA_GROUP], axis=0) for g in range(DSA_KV_HEADS)]
    slopes = jnp.concatenate([jnp.full((1, qb), DSA_SLOPES[h], F32) for h in range(N_HEADS)], axis=1)
    gw = DSA_GROUP * qb

    def attend_chunk(c, carry):
        m, l, acc = carry
        rows = pl.ds(pl.multiple_of(c * kc, kc), kc)
        kch = k_s[rows, :]
        vt = vt_s[c]
        sc = jnp.concatenate(
            [lax.dot_general(kch[:, g * HEAD_DIM:(g + 1) * HEAD_DIM], q_grp[g], (((1,), (1,)), ((), ())),
                             preferred_element_type=F32) for g in range(DSA_KV_HEADS)], axis=1)
        sc = sc - slopes * _lane_tile(distm_s[rows, :], N_HEADS)
        m_new = jnp.maximum(m, jnp.max(sc, axis=0, keepdims=True))
        alpha = jnp.exp(m - m_new)
        e = jnp.exp(sc - m_new)
        l = alpha * l + jnp.sum(e, axis=0, keepdims=True)
        eb = e.astype(MXU_DTYPE)
        pv = jnp.concatenate(
            [jnp.dot(vt[g * HEAD_DIM:(g + 1) * HEAD_DIM, :], eb[:, g * gw:(g + 1) * gw], preferred_element_type=F32)
             for g in range(DSA_KV_HEADS)], axis=1)
        return m_new, l, alpha * acc + pv

    init = (jnp.full((1, N_HEADS * qb), SOFTMAX_FLOOR, F32), jnp.zeros((1, N_HEADS * qb), F32),
            jnp.zeros((HEAD_DIM, N_HEADS * qb), F32))
    _, l, acc = lax.fori_loop(0, n_live, attend_chunk, init)
    o_t = jnp.concatenate([acc[:, h * qb:(h + 1) * qb] / l[:, h * qb:(h + 1) * qb] for h in range(N_HEADS)], axis=0)
    o_ref[0] = o_t.T


def _dsa(p, q_norm, k_norm, ln_w, ln_b):
    b, t, cols = p.shape
    qb = DSA_Q_BLOCK
    kc = DSA_KEY_CHUNK
    assert t % qb == 0 and cols == DSA_COLS and t % DSA_COUNT_ROWS == 0 and t % kc == 0 and kc % DSA_COUNT_ROWS == 0
    top_k = min(DSA_TOPK_MAX, t // 4)
    kvw = 2 * DSA_KV_HEADS * HEAD_DIM
    row = lambda v: v.reshape(1, -1).astype(F32)
    params = [row(q_norm), row(k_norm), row(ln_w), row(ln_b)]
    const = lambda i, j: (0, 0)
    return pl.pallas_call(
        functools.partial(_dsa_kernel, top_k=top_k),
        grid=(b, t // qb),
        in_specs=[pl.BlockSpec((1, qb, WIDTH), lambda i, j: (i, j, 0)),
                  pl.BlockSpec((1, qb, WIDTH), lambda i, j: (i, j, 1)),
                  pl.BlockSpec((1, t, kvw), lambda i, j: (i, 0, 2 * WIDTH // kvw)),
                  pl.BlockSpec((1, t, LANES), lambda i, j: (i, 0, (2 * WIDTH + kvw) // LANES)),
                  pl.BlockSpec((1, qb, LANES), lambda i, j: (i, j, (2 * WIDTH + kvw) // LANES))]
        + [pl.BlockSpec(q.shape, const) for q in params],
        out_specs=pl.BlockSpec((1, qb, WIDTH), lambda i, j: (i, j, 0)),
        out_shape=jax.ShapeDtypeStruct((b, t, WIDTH), F32),
        scratch_shapes=[pltpu.VMEM((t, DSA_KV_HEADS * HEAD_DIM), MXU_DTYPE),
                        pltpu.VMEM((t // kc, DSA_KV_HEADS * HEAD_DIM, kc), MXU_DTYPE),
                        pltpu.VMEM((t, 3 * IDX_DIM), MXU_DTYPE),
                        pltpu.VMEM((t, qb), F32),
                        pltpu.VMEM((t, qb), F32)],
        compiler_params=pltpu.CompilerParams(dimension_semantics=("parallel", "arbitrary"),
                                             vmem_limit_bytes=VMEM_LIMIT),
        name="dsa",
    )(p, p, p, p, p, *params)


def _moba_kernel(q_ref, k_ref, v_ref, qn_ref, kn_ref, o_ref, k_s, vt_s, km_s, sel_s, bias_s):
    bs = q_ref.shape[1]
    t_len = k_ref.shape[1]
    n_kb = t_len // bs
    i = pl.program_id(1)

    @pl.when(i == 0)
    def _():
        kraw = k_ref[0]
        for h in range(N_HEADS):
            kn = _head_rms(_head(kraw, h), kn_ref[...])
            k_s[:, h * HEAD_DIM:(h + 1) * HEAD_DIM] = kn.astype(MXU_DTYPE)
            means = [jnp.mean(kn[j * bs:(j + 1) * bs], axis=0, keepdims=True) for j in range(n_kb)]
            km_s[h] = jnp.concatenate(means, axis=0)
        v = v_ref[0]
        for j in range(n_kb):
            vt_s[j] = v[j * bs:(j + 1) * bs, :].T.astype(MXU_DTYPE)
        dist_own = (_iota((1, bs), 1) - _iota((bs, 1), 0)).astype(F32)
        for h in range(N_HEADS):
            bias_s[:, h * bs:(h + 1) * bs] = MOBA_SLOPES[h] * dist_own

    qraw = q_ref[0]
    jcol = _iota((n_kb, 1), 0)
    own = pl.multiple_of(i * bs, bs)
    qs = []
    gates = []
    for h in range(N_HEADS):
        q = _head_rms(_head(qraw, h), qn_ref[...])
        gates.append(_mm32_nt(km_s[h], q))
        qs.append((q * (HEAD_DIM ** -0.5)).astype(MXU_DTYPE))
    gate = jnp.where(jcol < i, jnp.concatenate(gates, axis=1), NEG_INF)
    rank = jnp.zeros(gate.shape, F32)
    for j2 in range(n_kb):
        other = gate[j2:j2 + 1, :]
        beats = (other > gate) | ((other == gate) & (j2 < jcol))
        rank = rank + jnp.where(beats, 1.0, 0.0)
    sel_s[...] = jnp.where((rank < float(MOBA_TOPK)) & (jcol < i), 0.0, jnp.inf)
    slopes = jnp.concatenate([jnp.full((1, bs), MOBA_SLOPES[h], F32) for h in range(N_HEADS)], axis=1)

    def scores(start):
        return jnp.concatenate(
            [lax.dot_general(k_s[pl.ds(start, bs), h * HEAD_DIM:(h + 1) * HEAD_DIM], qs[h], (((1,), (1,)), ((), ())),
                             preferred_element_type=F32) for h in range(N_HEADS)], axis=1)

    def weighted_values(j, e):
        vt = vt_s[j]
        eb = e.astype(MXU_DTYPE)
        return jnp.concatenate(
            [jnp.dot(vt[h * HEAD_DIM:(h + 1) * HEAD_DIM, :], eb[:, h * bs:(h + 1) * bs], preferred_element_type=F32)
             for h in range(N_HEADS)], axis=1)

    bias = bias_s[...]
    sc = jnp.where(bias >= 0, scores(own) - bias, NEG_INF)
    m0 = jnp.max(sc, axis=0, keepdims=True)
    e = jnp.exp(sc - m0)
    l0 = jnp.sum(e, axis=0, keepdims=True)
    acc0 = weighted_values(i, e)

    def block_step(j, carry):
        m, l, acc = carry
        shift = sel_s[pl.ds(j, 1), :] + slopes * ((i - j) * bs).astype(F32)
        s2 = scores(pl.multiple_of(j * bs, bs)) - bias_s[...] - shift
        m_new = jnp.maximum(m, jnp.max(s2, axis=0, keepdims=True))
        alpha = jnp.exp(m - m_new)
        e2 = jnp.exp(s2 - m_new)
        l = alpha * l + jnp.sum(e2, axis=0, keepdims=True)
        return m_new, l, alpha * acc + weighted_values(j, e2)

    _, l, acc = lax.fori_loop(0, i, block_step, (m0, l0, acc0))
    o_t = jnp.concatenate([acc[:, h * bs:(h + 1) * bs] / l[:, h * bs:(h + 1) * bs] for h in range(N_HEADS)], axis=0)
    o_ref[0] = o_t.T


def _moba(p, q_norm, k_norm):
    b, t, cols = p.shape
    bs = MOBA_BLOCK
    assert t % bs == 0 and cols == MOBA_COLS
    row = lambda v: v.reshape(1, -1).astype(F32)
    const = lambda i, j: (0, 0)
    return pl.pallas_call(
        _moba_kernel,
        grid=(b, t // bs),
        in_specs=[pl.BlockSpec((1, bs, WIDTH), lambda i, j: (i, j, 0)),
                  pl.BlockSpec((1, t, WIDTH), lambda i, j: (i, 0, 1)),
                  pl.BlockSpec((1, t, WIDTH), lambda i, j: (i, 0, 2)),
                  pl.BlockSpec((1, HEAD_DIM), const), pl.BlockSpec((1, HEAD_DIM), const)],
        out_specs=pl.BlockSpec((1, bs, WIDTH), lambda i, j: (i, j, 0)),
        out_shape=jax.ShapeDtypeStruct((b, t, WIDTH), F32),
        scratch_shapes=[pltpu.VMEM((t, WIDTH), MXU_DTYPE),
                        pltpu.VMEM((t // bs, WIDTH, bs), MXU_DTYPE),
                        pltpu.VMEM((N_HEADS, t // bs, HEAD_DIM), F32),
                        pltpu.VMEM((t // bs, N_HEADS * bs), F32),
                        pltpu.VMEM((bs, N_HEADS * bs), F32)],
        compiler_params=pltpu.CompilerParams(dimension_semantics=("parallel", "arbitrary"),
                                             vmem_limit_bytes=VMEM_LIMIT),
        name="moba",
    )(p, p, p, row(q_norm), row(k_norm))


def _pad_cols(w, n):
    return jnp.pad(w, ((0, 0), (0, n - w.shape[1])))


def _even_layer(x, b, t, norm_g, w_in, w_out, mu, w0, w2, a0, a2, g2, k_k, k_a, r_k, lnx_w, lnx_b,
                conv_w, a_log, dt_bias, gdn_norm_w):
    qkv_w = 3 * WIDTH
    w_rwkv = w_in[:, :RWKV_COLS]
    w_g = w_in[:, RWKV_COLS:]
    w_gdn = jnp.concatenate([w_g[:, :qkv_w], w_g[:, qkv_w + 2 * N_HEADS:],
                             _pad_cols(w_g[:, qkv_w:qkv_w + 2 * N_HEADS], LANES)], axis=1)
    p_rwkv, p_gdn = _norm_proj(x, norm_g, [w_rwkv.astype(MXU_DTYPE), w_gdn.astype(MXU_DTYPE)])
    o_a = _rwkv(p_rwkv.reshape(b, t, -1), mu, w0, w2, a0, a2, g2, k_k, k_a, r_k, lnx_w, lnx_b)
    o_b = _gdn(p_gdn.reshape(b, t, -1), conv_w, a_log, dt_bias, gdn_norm_w)
    return _out_proj(x, o_a.reshape(b * t, -1), o_b.reshape(b * t, -1), w_out)


def _odd_layer(x, b, t, norm_g, w_in, w_out, dsa_q_norm, dsa_k_norm, idx_ln_w, idx_ln_b, moba_q_norm, moba_k_norm):
    kvw = 2 * DSA_KV_HEADS * HEAD_DIM
    sizes = (WIDTH, kvw, IDX_HEADS * IDX_DIM, IDX_DIM, IDX_HEADS, WIDTH, WIDTH, WIDTH)
    offs = [0]
    for s in sizes:
        offs.append(offs[-1] + s)
    dq, dkv, iq, ik, iw, mq, mk, mv = (w_in[:, offs[n]:offs[n + 1]] for n in range(len(sizes)))
    w_dsa = jnp.concatenate([dq, iq, dkv, _pad_cols(jnp.concatenate([ik, iw], axis=1), LANES)], axis=1)
    w_moba = jnp.concatenate([mq, mk, mv], axis=1)
    p_dsa, p_moba = _norm_proj(x, norm_g, [w_dsa.astype(MXU_DTYPE), w_moba.astype(MXU_DTYPE)])
    o_c = _dsa(p_dsa.reshape(b, t, -1), dsa_q_norm, dsa_k_norm, idx_ln_w, idx_ln_b)
    o_d = _moba(p_moba.reshape(b, t, -1), moba_q_norm, moba_k_norm)
    return _out_proj(x, o_c.reshape(b * t, -1), o_d.reshape(b * t, -1), w_out)


def kernel(x, norm1_g, norm2_g, mlp_w1, mlp_w2, ev_w_in, ev_w_out, rwkv_mu, rwkv_w0, rwkv_w2, rwkv_a0, rwkv_a2, rwkv_g2, rwkv_k_k, rwkv_k_a, rwkv_r_k, rwkv_lnx_w, rwkv_lnx_b, gdn_conv_w, gdn_a_log, gdn_dt_bias, gdn_norm_w, od_w_in, od_w_out, dsa_q_norm, dsa_k_norm, idx_k_ln_w, idx_k_ln_b, moba_q_norm, moba_k_norm):
    b, t, d = x.shape
    depth = norm1_g.shape[0]
    h = x.reshape(b * t, d)
    for i in range(depth):
        j = i // 2
        if i % 2 == 0:
            h = _even_layer(h, b, t, norm1_g[i], ev_w_in[j], ev_w_out[j], rwkv_mu[j], rwkv_w0[j], rwkv_w2[j],
                            rwkv_a0[j], rwkv_a2[j], rwkv_g2[j], rwkv_k_k[j], rwkv_k_a[j], rwkv_r_k[j],
                            rwkv_lnx_w[j], rwkv_lnx_b[j], gdn_conv_w[j], gdn_a_log[j], gdn_dt_bias[j],
                            gdn_norm_w[j])
        else:
            h = _odd_layer(h, b, t, norm1_g[i], od_w_in[j], od_w_out[j], dsa_q_norm[j], dsa_k_norm[j],
                           idx_k_ln_w[j], idx_k_ln_b[j], moba_q_norm[j], moba_k_norm[j])
        h = _mlp(h, norm2_g[i], mlp_w1[i], mlp_w2[i])
    return h.reshape(b, t, d)
```

```python
import functools
import math

import jax
import jax.numpy as jnp
from jax import lax
from jax.experimental import pallas as pl
from jax.experimental.pallas import tpu as pltpu

F32 = jnp.float32
MXU_DTYPE = jnp.bfloat16
HIGHEST = lax.Precision.HIGHEST

LANES = 128
VMEM_LIMIT = 56 * 1024 * 1024

HEAD_DIM = 64
N_HEADS = 8
WIDTH = N_HEADS * HEAD_DIM
RMS_EPS = 1e-6

RWKV_DECAY_LORA = 64
RWKV_A_LORA = 64
RWKV_GATE_LORA = 128
RWKV_COLS = 3 * WIDTH + RWKV_DECAY_LORA + RWKV_A_LORA + RWKV_GATE_LORA
RWKV_GN_EPS = 6.4e-4

GDN_CONV = 4
GDN_COLS = 3 * WIDTH + WIDTH + LANES
SCAN_CHUNK = 64
SCAN_BLOCK = 256

DSA_KV_HEADS = 2
DSA_GROUP = N_HEADS // DSA_KV_HEADS
IDX_HEADS = 8
IDX_DIM = 64
DSA_TOPK_MAX = 256
DSA_Q_BLOCK = 256
DSA_KEY_CHUNK = 256
DSA_COUNT_ROWS = 256
DSA_COLS = 2 * WIDTH + 2 * DSA_KV_HEADS * HEAD_DIM + LANES

MOBA_BLOCK = 256
MOBA_TOPK = 3
MOBA_COLS = 3 * WIDTH

PROJ_ROWS = 512
MLP_ROWS = 1024
MLP_FF_COLS = 1024

ALIBI_HEADS = 2 * N_HEADS
NEG_INF = float("-inf")


def _alibi_slope(i):
    return 2.0 ** (-8.0 * (i + 1) / ALIBI_HEADS)


DSA_SLOPES = tuple(_alibi_slope(2 * h) for h in range(N_HEADS))
MOBA_SLOPES = tuple(_alibi_slope(2 * h + 1) for h in range(N_HEADS))


def _mm(a, b):
    return jnp.dot(a.astype(MXU_DTYPE), b.astype(MXU_DTYPE), preferred_element_type=F32)


def _mm32_nt(a, b):
    return lax.dot_general(a, b, (((1,), (1,)), ((), ())), preferred_element_type=F32, precision=HIGHEST)


def _iota(shape, dim):
    return lax.broadcasted_iota(jnp.int32, shape, dim)


def _sigmoid(x):
    return 1.0 / (1.0 + jnp.exp(-x))


def _silu(x):
    return x * _sigmoid(x)


def _softplus(x):
    return jnp.maximum(x, 0.0) + jnp.log1p(jnp.exp(-jnp.abs(x)))


def _head(x, h):
    return x[:, h * HEAD_DIM:(h + 1) * HEAD_DIM]


def _head_rms(x, w):
    return x * lax.rsqrt(jnp.mean(x * x, axis=-1, keepdims=True) + RMS_EPS) * w


def _split_hi_lo(x):
    hi = x.astype(MXU_DTYPE)
    lo = (x - hi.astype(F32)).astype(MXU_DTYPE)
    return hi, lo


def _head_segments(width):
    return (_iota((width, width), 0) // HEAD_DIM == _iota((width, width), 1) // HEAD_DIM).astype(MXU_DTYPE)


def _head_sums(x, seg):
    hi, lo = _split_hi_lo(x)
    return jnp.dot(hi, seg, preferred_element_type=F32) + jnp.dot(lo, seg, preferred_element_type=F32)


def _tree_sum(parts):
    parts = list(parts)
    while len(parts) > 1:
        nxt = [parts[a] + parts[a + 1] for a in range(0, len(parts) - 1, 2)]
        if len(parts) % 2:
            nxt.append(parts[-1])
        parts = nxt
    return parts[0]


def _lane_tile(x, n):
    return jnp.concatenate([x] * n, axis=1)


def _norm_proj_kernel(x_ref, g_ref, *refs):
    n = len(refs) // 2
    x = x_ref[...]
    h = x * lax.rsqrt(jnp.mean(x * x, axis=-1, keepdims=True) + RMS_EPS) * g_ref[...]
    h = h.astype(MXU_DTYPE)
    for w_ref, o_ref in zip(refs[:n], refs[n:]):
        o_ref[...] = jnp.dot(h, w_ref[...], preferred_element_type=F32)


def _norm_proj(x, g, ws, tm=PROJ_ROWS):
    n, d = x.shape
    assert n % tm == 0
    const = lambda i: (0, 0)
    return pl.pallas_call(
        _norm_proj_kernel,
        grid=(n // tm,),
        in_specs=[pl.BlockSpec((tm, d), lambda i: (i, 0)), pl.BlockSpec((1, d), const)]
        + [pl.BlockSpec(w.shape, const) for w in ws],
        out_specs=[pl.BlockSpec((tm, w.shape[1]), lambda i: (i, 0)) for w in ws],
        out_shape=[jax.ShapeDtypeStruct((n, w.shape[1]), F32) for w in ws],
        compiler_params=pltpu.CompilerParams(dimension_semantics=("parallel",), vmem_limit_bytes=VMEM_LIMIT),
        name="norm_proj",
    )(x, g.reshape(1, d), *ws)


def _out_mlp_kernel(x_ref, a_ref, b_ref, wa_ref, wb_ref, g_ref, w1_ref, w2_ref, o_ref, h_ref, acc_ref):
    j = pl.program_id(1)

    @pl.when(j == 0)
    def _():
        x = x_ref[...] + _mm(a_ref[...], wa_ref[...]) + _mm(b_ref[...], wb_ref[...])
        h = x * lax.rsqrt(jnp.mean(x * x, axis=-1, keepdims=True) + RMS_EPS) * g_ref[...]
        h_ref[...] = h.astype(MXU_DTYPE)
        acc_ref[...] = x

    u = jnp.maximum(jnp.dot(h_ref[...], w1_ref[...], preferred_element_type=F32), 0.0)
    acc_ref[...] += jnp.dot((u * u).astype(MXU_DTYPE), w2_ref[...], preferred_element_type=F32)

    @pl.when(j == pl.num_programs(1) - 1)
    def _():
        o_ref[...] = acc_ref[...]


def _out_mlp(x, a, b, w_out, g, w1, w2, tm=MLP_ROWS, tf=MLP_FF_COLS):
    n, d = x.shape
    f = w1.shape[1]
    tm = min(tm, n)
    tf = min(tf, f)
    assert n % tm == 0 and f % tf == 0
    wa = w_out[:a.shape[1]].astype(MXU_DTYPE)
    wb = w_out[a.shape[1]:].astype(MXU_DTYPE)
    row = lambda i, j: (i, 0)
    const = lambda i, j: (0, 0)
    return pl.pallas_call(
        _out_mlp_kernel,
        grid=(n // tm, f // tf),
        in_specs=[pl.BlockSpec((tm, d), row), pl.BlockSpec((tm, a.shape[1]), row), pl.BlockSpec((tm, b.shape[1]), row),
                  pl.BlockSpec(wa.shape, const), pl.BlockSpec(wb.shape, const), pl.BlockSpec((1, d), const),
                  pl.BlockSpec((d, tf), lambda i, j: (0, j)), pl.BlockSpec((tf, d), lambda i, j: (j, 0))],
        out_specs=pl.BlockSpec((tm, d), row),
        out_shape=jax.ShapeDtypeStruct((n, d), F32),
        scratch_shapes=[pltpu.VMEM((tm, d), MXU_DTYPE), pltpu.VMEM((tm, d), F32)],
        compiler_params=pltpu.CompilerParams(dimension_semantics=("parallel", "arbitrary"),
                                             vmem_limit_bytes=VMEM_LIMIT),
        name="out_mlp",
    )(x, a, b, wa, wb, g.reshape(1, d), w1.astype(MXU_DTYPE), w2.astype(MXU_DTYPE))


def _bmm(a, b):
    return lax.dot_general(a.astype(MXU_DTYPE), b.astype(MXU_DTYPE), (((2,), (1,)), ((0,), (0,))),
                           preferred_element_type=F32)


def _bmm_nt(a, b):
    return lax.dot_general(a.astype(MXU_DTYPE), b.astype(MXU_DTYPE), (((2,), (2,)), ((0,), (0,))),
                           preferred_element_type=F32)


def _bmm_tn(a, b):
    return lax.dot_general(a.astype(MXU_DTYPE), b.astype(MXU_DTYPE), (((1,), (1,)), ((0,), (0,))),
                           preferred_element_type=F32)


def _stack_heads(x, nc, c):
    return jnp.stack([x[i * c:(i + 1) * c, h * HEAD_DIM:(h + 1) * HEAD_DIM] for i in range(nc) for h in range(N_HEADS)])


def _unstack_heads(y, nc):
    return jnp.concatenate([jnp.concatenate([y[i * N_HEADS + h] for h in range(N_HEADS)], axis=-1)
                            for i in range(nc)], axis=0)


def _batched_unit_lower_solve(x, y):
    c = x.shape[1]
    y = y + _bmm(x, y)
    p = 2
    while p < c:
        x = _bmm(x, x)
        y = y + _bmm(x, y)
        p *= 2
    return y


def _split3(x):
    hi = x.astype(MXU_DTYPE)
    r = x - hi.astype(F32)
    mid = r.astype(MXU_DTYPE)
    lo = (r - mid.astype(F32)).astype(MXU_DTYPE)
    return hi, mid, lo


def _chunk_cumsum_rows(x, c):
    tri = (_iota((c, c), 0) >= _iota((c, c), 1)).astype(MXU_DTYPE)
    parts = _split3(x)
    return jnp.concatenate(
        [_tree_sum([jnp.dot(tri, p[i * c:(i + 1) * c], preferred_element_type=F32) for p in parts])
         for i in range(x.shape[0] // c)], axis=0)


def _chunk_cumsum_cols(x, c):
    tri = (_iota((c, c), 0) <= _iota((c, c), 1)).astype(MXU_DTYPE)
    parts = _split3(x)
    return jnp.concatenate(
        [_tree_sum([jnp.dot(p[:, i * c:(i + 1) * c], tri, preferred_element_type=F32) for p in parts])
         for i in range(x.shape[1] // c)], axis=1)


def _run_chunks(st_ref, m, q, oq, o0, gamma, nc):
    s = st_ref[...]
    outs = []
    for i in range(nc):
        g = slice(i * N_HEADS, (i + 1) * N_HEADS)
        mo = _bmm(jnp.concatenate([m[g], oq[g]], axis=1), s)
        outs.append(mo[:, HEAD_DIM:] + o0[g])
        s = gamma[g] * s + mo[:, :HEAD_DIM] + q[g]
    st_ref[...] = s
    return jnp.concatenate(outs, axis=0)


def _rwkv_kernel(p_ref, mu_ref, w0_ref, w2_ref, a0_ref, a2_ref, g2_ref, kk_ref, ka_ref, rk_ref, lnw_ref, lnb_ref,
                 o_ref, prev_ref, st_ref):
    tb = p_ref.shape[1]
    c = SCAN_CHUNK
    nc = tb // c

    @pl.when(pl.program_id(1) == 0)
    def _():
        prev_ref[...] = jnp.zeros_like(prev_ref)
        st_ref[...] = jnp.zeros_like(st_ref)

    p = p_ref[0]
    shifted = jnp.where(_iota((tb, 1), 0) == 0, prev_ref[...], pltpu.roll(p, 1, 0))
    prev_ref[...] = p[tb - 1:tb, :]
    x = p + (shifted - p) * mu_ref[...]

    r = x[:, 0:WIDTH]
    k = x[:, WIDTH:2 * WIDTH]
    v = x[:, 2 * WIDTH:3 * WIDTH]
    off = 3 * WIDTH
    wd = x[:, off:off + RWKV_DECAY_LORA]
    ad = x[:, off + RWKV_DECAY_LORA:off + RWKV_DECAY_LORA + RWKV_A_LORA]
    gd = x[:, off + RWKV_DECAY_LORA + RWKV_A_LORA:]

    w_pre = w0_ref[...] + _mm(jnp.tanh(wd), w2_ref[...])
    log_w = -(_sigmoid(w_pre) * math.exp(-0.5))
    a = _sigmoid(a0_ref[...] + _mm(ad, a2_ref[...]))
    g = _mm(_sigmoid(gd), g2_ref[...])
    kk_all = k * kk_ref[...]
    k = k * (1.0 + (a - 1.0) * ka_ref[...])

    gam = _chunk_cumsum_rows(log_w, c)
    gam_last = jnp.concatenate([jnp.broadcast_to(gam[(i + 1) * c - 1:(i + 1) * c, :], (c, WIDTH)) for i in range(nc)],
                               axis=0)

    seg = _head_segments(WIDTH)
    kk = kk_all * lax.rsqrt(_head_sums(kk_all * kk_all, seg) + 1e-6)
    b = kk * a
    e_neg = jnp.exp(-gam)
    tail = jnp.exp(gam_last - gam)

    st = lambda y: _stack_heads(y, nc, c)
    v_h = st(v)
    a_t = st(-kk * jnp.exp(gam - log_w))
    r_t = st(r * jnp.exp(gam))
    b_t = st(b * e_neg)
    k_t = st(k * e_neg)
    e_last = st(jnp.exp(gam_last))[:, 0:1, :]

    ri = _iota((c, c), 0)
    ci = _iota((c, c), 1)
    incl = (ri >= ci)[None]
    strict = (ri > ci)[None]
    eye = (ri == ci)[None]

    lhs = jnp.concatenate([a_t, r_t], axis=1)
    m_b = _bmm_nt(lhs, b_t)
    m_k = _bmm_nt(lhs, k_t)
    a_rb = jnp.where(incl, m_b[:, c:], 0.0)
    akv = _bmm(jnp.where(strict, m_k[:, :c], 0.0), v_h)
    wu = _batched_unit_lower_solve(jnp.where(strict, m_b[:, :c], 0.0), jnp.concatenate([a_t, akv], axis=-1))
    ro = _bmm(a_rb, wu)
    oq = r_t + ro[:, :, :HEAD_DIM]
    o0 = ro[:, :, HEAD_DIM:] + _bmm(jnp.where(incl, m_k[:, c:], 0.0), v_h)
    mq = _bmm_tn(st(b * tail), wu)
    m = mq[:, :, :HEAD_DIM]
    q = mq[:, :, HEAD_DIM:] + _bmm_tn(st(k * tail), v_h)
    gamma = jnp.sum(jnp.where(eye, e_last, 0.0), axis=2, keepdims=True)

    o = _unstack_heads(_run_chunks(st_ref, m, q, oq, o0, gamma, nc), nc)

    oc = o - _head_sums(o, seg) * (1.0 / HEAD_DIM)
    on = oc * lax.rsqrt(_head_sums(oc * oc, seg) * (1.0 / HEAD_DIM) + RWKV_GN_EPS)
    bonus = _head_sums(r * k * rk_ref[...], seg) * v
    o_ref[0] = (on * lnw_ref[...] + lnb_ref[...] + bonus) * g


def _rwkv(p, mu, w0, w2, a0, a2, g2, k_k, k_a, r_k, lnx_w, lnx_b, tb=SCAN_BLOCK):
    b, t, cols = p.shape
    tb = min(tb, t)
    assert t % tb == 0 and tb % SCAN_CHUNK == 0 and cols == RWKV_COLS
    row = lambda v: v.reshape(1, -1).astype(F32)
    params = [row(mu), row(w0), w2.astype(MXU_DTYPE), row(a0), a2.astype(MXU_DTYPE), g2.astype(MXU_DTYPE),
              row(k_k), row(k_a), row(r_k), row(lnx_w), row(lnx_b)]
    const = lambda i, j: (0, 0)
    return pl.pallas_call(
        _rwkv_kernel,
        grid=(b, t // tb),
        in_specs=[pl.BlockSpec((1, tb, cols), lambda i, j: (i, j, 0))] + [pl.BlockSpec(q.shape, const) for q in params],
        out_specs=pl.BlockSpec((1, tb, WIDTH), lambda i, j: (i, j, 0)),
        out_shape=jax.ShapeDtypeStruct((b, t, WIDTH), F32),
        scratch_shapes=[pltpu.VMEM((1, cols), F32), pltpu.VMEM((N_HEADS, HEAD_DIM, HEAD_DIM), F32)],
        compiler_params=pltpu.CompilerParams(dimension_semantics=("parallel", "arbitrary"),
                                             vmem_limit_bytes=VMEM_LIMIT),
        name="rwkv7",
    )(p, *params)


def _gdn_kernel(p_ref, cw_ref, alog_r_ref, dt_r_ref, alog_c_ref, dt_c_ref, nw_ref, o_ref, tail_ref, st_ref):
    tb = p_ref.shape[1]
    c = SCAN_CHUNK
    nc = tb // c
    qkv_w = 3 * WIDTH

    @pl.when(pl.program_id(1) == 0)
    def _():
        tail_ref[...] = jnp.zeros_like(tail_ref)
        st_ref[...] = jnp.zeros_like(st_ref)

    p = p_ref[0]
    xin = p[:, :qkv_w]
    z = p[:, qkv_w:qkv_w + WIDTH]
    b_in = p[:, qkv_w + WIDTH:qkv_w + WIDTH + N_HEADS]
    a_in = p[:, qkv_w + WIDTH + N_HEADS:qkv_w + WIDTH + 2 * N_HEADS]

    tail = tail_ref[...]
    row8 = _iota((8, 1), 0)
    conv = xin * cw_ref[GDN_CONV - 1:GDN_CONV, :]
    for s in range(1, GDN_CONV):
        rolled = pltpu.roll(xin, s, 0)
        first = jnp.where(row8 < s, pltpu.roll(tail, s, 0), rolled[:8])
        shifted = jnp.concatenate([first, rolled[8:]], axis=0)
        conv = conv + shifted * cw_ref[GDN_CONV - 1 - s:GDN_CONV - s, :]
    tail_ref[...] = xin[tb - 8:, :]
    qkv = _silu(conv)

    beta = _sigmoid(b_in)
    g_col = -jnp.exp(alog_r_ref[...]) * _softplus(a_in + dt_r_ref[...])
    gc_col = _chunk_cumsum_rows(g_col, c)
    eye_h = (_iota((N_HEADS, N_HEADS), 0) == _iota((N_HEADS, N_HEADS), 1)).astype(F32)
    a_row = _mm32_nt(eye_h, a_in)
    g_row = -jnp.exp(alog_c_ref[...]) * _softplus(a_row + dt_c_ref[...])
    gc_row = _chunk_cumsum_cols(g_row, c)

    def per_head_cols(y):
        return jnp.stack([y[i * c:(i + 1) * c, h:h + 1] for i in range(nc) for h in range(N_HEADS)])

    gc = per_head_cols(gc_col)
    bt = per_head_cols(beta)
    g_rows = jnp.stack([gc_row[h:h + 1, i * c:(i + 1) * c] for i in range(nc) for h in range(N_HEADS)])

    seg = _head_segments(WIDTH)
    q = qkv[:, :WIDTH]
    k = qkv[:, WIDTH:2 * WIDTH]
    st = lambda y: _stack_heads(y, nc, c)
    q = st(q * lax.rsqrt(_head_sums(q * q, seg) + 1e-6) * (HEAD_DIM ** -0.5))
    k = st(k * lax.rsqrt(_head_sums(k * k, seg) + 1e-6))
    v = st(qkv[:, 2 * WIDTH:])

    ri = _iota((c, c), 0)
    ci = _iota((c, c), 1)
    incl = (ri >= ci)[None]
    strict = (ri > ci)[None]
    decay = jnp.exp(jnp.where(incl, gc - g_rows, NEG_INF))
    kb = k * bt
    e_g = jnp.exp(gc)
    mm = _bmm_nt(jnp.concatenate([kb, q], axis=1), k)
    a_mat = jnp.where(strict, mm[:, :c] * decay, 0.0)
    qk = jnp.where(incl, mm[:, c:] * decay, 0.0)
    uw = _batched_unit_lower_solve(-a_mat, jnp.concatenate([v * bt, kb * e_g], axis=-1))
    qkuw = _bmm(qk, uw)
    o0 = qkuw[:, :, :HEAD_DIM]
    oq = q * e_g - qkuw[:, :, HEAD_DIM:]
    g_last = gc[:, c - 1:c, :]
    kuw = _bmm_tn(k * jnp.exp(g_last - gc), uw)
    o = _run_chunks(st_ref, -kuw[:, :, HEAD_DIM:], kuw[:, :, :HEAD_DIM], oq, o0, jnp.exp(g_last), nc)

    o = _unstack_heads(o, nc)
    o = o * lax.rsqrt(_head_sums(o * o, seg) * (1.0 / HEAD_DIM) + RMS_EPS) * _lane_tile(nw_ref[...], N_HEADS)
    o_ref[0] = o * _silu(z)


def _gdn(p, conv_w, a_log, dt_bias, norm_w, tb=SCAN_BLOCK):
    b, t, cols = p.shape
    tb = min(tb, t)
    assert t % tb == 0 and tb % SCAN_CHUNK == 0 and cols == GDN_COLS
    params = [conv_w.astype(F32), a_log.reshape(1, -1), dt_bias.reshape(1, -1), a_log.reshape(-1, 1),
              dt_bias.reshape(-1, 1), norm_w.reshape(1, -1)]
    const = lambda i, j: (0, 0)
    return pl.pallas_call(
        _gdn_kernel,
        grid=(b, t // tb),
        in_specs=[pl.BlockSpec((1, tb, cols), lambda i, j: (i, j, 0))] + [pl.BlockSpec(q.shape, const) for q in params],
        out_specs=pl.BlockSpec((1, tb, WIDTH), lambda i, j: (i, j, 0)),
        out_shape=jax.ShapeDtypeStruct((b, t, WIDTH), F32),
        scratch_shapes=[pltpu.VMEM((8, 3 * WIDTH), F32), pltpu.VMEM((N_HEADS, HEAD_DIM, HEAD_DIM), F32)],
        compiler_params=pltpu.CompilerParams(dimension_semantics=("parallel", "arbitrary"),
                                             vmem_limit_bytes=VMEM_LIMIT),
        name="gdn",
    )(p, *params)


def _key_to_float(key):
    return pltpu.bitcast(jnp.where(key >= 0, key, key ^ jnp.int32(0x7FFFFFFF)), F32)


NEG_INF_KEY = -2139095041
SOFTMAX_FLOOR = -1e30


def _dsa_kernel(dq_ref, iq_ref, kv_ref, ikw_ref, iwq_ref, qn_ref, kn_ref, lnw_ref, lnb_ref, o_ref,
                k_s, vt_s, ik_s, score_s, distm_s, *, top_k):
    qb = dq_ref.shape[1]
    t_len = kv_ref.shape[1]
    kc = DSA_KEY_CHUNK
    n_chunks = t_len // kc
    kvw = DSA_KV_HEADS * HEAD_DIM
    i = pl.program_id(1)
    n_live = (i * qb + qb + kc - 1) // kc

    @pl.when(i == 0)
    def _():
        kv = kv_ref[0]
        k_s[...] = jnp.concatenate([_head_rms(_head(kv, g), kn_ref[...]) for g in range(DSA_KV_HEADS)],
                                   axis=-1).astype(MXU_DTYPE)
        for c in range(n_chunks):
            vt_s[c] = kv[c * kc:(c + 1) * kc, kvw:].T.astype(MXU_DTYPE)
        ik = ikw_ref[0][:, :IDX_DIM]
        ikc = ik - jnp.mean(ik, axis=-1, keepdims=True)
        ik = ikc * lax.rsqrt(jnp.mean(ikc * ikc, axis=-1, keepdims=True) + 1e-6) * lnw_ref[...] + lnb_ref[...]
        hi, lo = _split_hi_lo(ik)
        ik_s[...] = jnp.concatenate([hi, hi, lo], axis=-1)
        score_s[...] = jnp.full(score_s.shape, NEG_INF, F32)

    t_row = i * qb + _iota((1, qb), 1)
    s_loc = _iota((kc, 1), 0)

    eye_h = (_iota((IDX_HEADS, IDX_HEADS), 0) == _iota((IDX_HEADS, IDX_HEADS), 1)).astype(F32)
    iw_t = _mm32_nt(eye_h, iwq_ref[0][:, IDX_DIM:IDX_DIM + IDX_HEADS]) * (IDX_HEADS ** -0.5 * IDX_DIM ** -0.5)
    iw_wide = jnp.concatenate([iw_t[h:h + 1, :] for h in range(IDX_HEADS)], axis=1)
    iq = iq_ref[0]
    iq_cat = []
    for h in range(IDX_HEADS):
        hi, lo = _split_hi_lo(_head(iq, h))
        iq_cat.append(jnp.concatenate([hi, lo, hi], axis=-1))
    iq_all = jnp.concatenate(iq_cat, axis=0)

    def score_chunk(c, carry):
        rows = pl.ds(pl.multiple_of(c * kc, kc), kc)
        dots = lax.dot_general(ik_s[rows, :], iq_all, (((1,), (1,)), ((), ())), preferred_element_type=F32)
        w = iw_wide * jnp.maximum(dots, 0.0)
        sc = _tree_sum([w[:, h * qb:(h + 1) * qb] for h in range(IDX_HEADS)])
        score_s[rows, :] = jnp.where(c * kc + s_loc <= t_row, sc, NEG_INF)
        return carry

    lax.fori_loop(0, n_live, score_chunk, 0)

    kf = float(top_k)
    grp = DSA_COUNT_ROWS
    n_grp = (n_live * kc + grp - 1) // grp

    def count(pred):
        def body(g, acc):
            blk = score_s[pl.ds(pl.multiple_of(g * grp, grp), grp), :]
            ones = jnp.where(pred(blk), 1.0, 0.0)
            return acc + _tree_sum([ones[r * 8:(r + 1) * 8] for r in range(grp // 8)])
        acc = lax.fori_loop(0, n_grp, body, jnp.zeros((8, qb), F32))
        return jnp.sum(acc, axis=0, keepdims=True)

    int_min = jnp.int32(-2 ** 31)
    tau = jnp.where(count(lambda blk: blk >= 0.0) >= kf, jnp.int32(0), int_min)

    def bit_step(b, tau):
        cand = tau | (jnp.int32(1) << (30 - b))
        cand_f = _key_to_float(cand)
        return jnp.where(count(lambda blk: blk >= cand_f) >= kf, cand, tau)

    tau = lax.fori_loop(0, 31, bit_step, tau)
    tau_f = _key_to_float(jnp.maximum(tau, jnp.int32(NEG_INF_KEY)))
    need = kf - count(lambda blk: blk > tau_f)

    n_ge = count(lambda blk: blk >= tau_f)
    no_partial_tie = jnp.min(jnp.where((n_ge <= kf) | (tau_f == NEG_INF), 1.0, 0.0)) > 0.5

    def masked_distance(c, sel_fn):
        rows = pl.ds(pl.multiple_of(c * kc, kc), kc)
        s_pos = c * kc + s_loc
        sel = sel_fn(score_s[rows, :]) & (s_pos <= t_row)
        distm_s[rows, :] = jnp.where(sel, (t_row - s_pos).astype(F32), jnp.inf)

    @pl.when(no_partial_tie)
    def _():
        def select_chunk(c, carry):
            masked_distance(c, lambda blk: blk >= tau_f)
            return carry
        lax.fori_loop(0, n_live, select_chunk, 0)

    @pl.when(jnp.logical_not(no_partial_tie))
    def _():
        lower = (_iota((kc, kc), 0) >= _iota((kc, kc), 1)).astype(MXU_DTYPE)

        def select_chunk(c, run):
            def sel_fn(blk):
                eq = blk == tau_f
                pref = run + jnp.dot(lower, jnp.where(eq, 1.0, 0.0).astype(MXU_DTYPE), preferred_element_type=F32)
                return (blk > tau_f) | (eq & (pref <= need))
            masked_distance(c, sel_fn)
            rows = pl.ds(pl.multiple_of(c * kc, kc), kc)
            return run + jnp.sum(jnp.where(score_s[rows, :] == tau_f, 1.0, 0.0), axis=0, keepdims=True)
        lax.fori_loop(0, n_live, select_chunk, jnp.zeros((1, qb), F32))

    dq = dq_ref[0]
    dq = dq * lax.rsqrt(_head_sums(dq * dq, _head_segments(WIDTH)) * (1.0 / HEAD_DIM) + RMS_EPS)
    dq = (dq * _lane_tile(qn_ref[...] * (HEAD_DIM ** -0.5), N_HEADS)).astype(MXU_DTYPE)
    qs = [_head(dq, h) for h in range(N_HEADS)]
    q_grp = [jnp.concatenate(qs[g * DSA_GROUP:(g + 1) * DSA_GROUP], axis=0) for g in range(DSA_KV_HEADS)]
    slopes = jnp.concatenate([jnp.full((1, qb), DSA_SLOPES[h], F32) for h in range(N_HEADS)], axis=1)
    gw = DSA_GROUP * qb

    def attend_chunk(c, carry):
        m, l, acc = carry
        rows = pl.ds(pl.multiple_of(c * kc, kc), kc)
        kch = k_s[rows, :]
        vt = vt_s[c]
        sc = jnp.concatenate(
            [lax.dot_general(kch[:, g * HEAD_DIM:(g + 1) * HEAD_DIM], q_grp[g], (((1,), (1,)), ((), ())),
                             preferred_element_type=F32) for g in range(DSA_KV_HEADS)], axis=1)
        sc = sc - slopes * _lane_tile(distm_s[rows, :], N_HEADS)
        m_new = jnp.maximum(m, jnp.max(sc, axis=0, keepdims=True))
        alpha = jnp.exp(m - m_new)
        e = jnp.exp(sc - m_new)
        l = alpha * l + jnp.sum(e, axis=0, keepdims=True)
        eb = e.astype(MXU_DTYPE)
        pv = jnp.concatenate(
            [jnp.dot(vt[g * HEAD_DIM:(g + 1) * HEAD_DIM, :], eb[:, g * gw:(g + 1) * gw], preferred_element_type=F32)
             for g in range(DSA_KV_HEADS)], axis=1)
        return m_new, l, alpha * acc + pv

    init = (jnp.full((1, N_HEADS * qb), SOFTMAX_FLOOR, F32), jnp.zeros((1, N_HEADS * qb), F32),
            jnp.zeros((HEAD_DIM, N_HEADS * qb), F32))
    _, l, acc = lax.fori_loop(0, n_live, attend_chunk, init)
    o_t = jnp.concatenate([acc[:, h * qb:(h + 1) * qb] / l[:, h * qb:(h + 1) * qb] for h in range(N_HEADS)], axis=0)
    o_ref[0] = o_t.T


def _dsa(p, q_norm, k_norm, ln_w, ln_b):
    b, t, cols = p.shape
    qb = DSA_Q_BLOCK
    kc = DSA_KEY_CHUNK
    assert t % qb == 0 and cols == DSA_COLS and t % DSA_COUNT_ROWS == 0 and t % kc == 0 and kc % DSA_COUNT_ROWS == 0
    top_k = min(DSA_TOPK_MAX, t // 4)
    kvw = 2 * DSA_KV_HEADS * HEAD_DIM
    row = lambda v: v.reshape(1, -1).astype(F32)
    params = [row(q_norm), row(k_norm), row(ln_w), row(ln_b)]
    const = lambda i, j: (0, 0)
    return pl.pallas_call(
        functools.partial(_dsa_kernel, top_k=top_k),
        grid=(b, t // qb),
        in_specs=[pl.BlockSpec((1, qb, WIDTH), lambda i, j: (i, j, 0)),
                  pl.BlockSpec((1, qb, WIDTH), lambda i, j: (i, j, 1)),
                  pl.BlockSpec((1, t, kvw), lambda i, j: (i, 0, 2 * WIDTH // kvw)),
                  pl.BlockSpec((1, t, LANES), lambda i, j: (i, 0, (2 * WIDTH + kvw) // LANES)),
                  pl.BlockSpec((1, qb, LANES), lambda i, j: (i, j, (2 * WIDTH + kvw) // LANES))]
        + [pl.BlockSpec(q.shape, const) for q in params],
        out_specs=pl.BlockSpec((1, qb, WIDTH), lambda i, j: (i, j, 0)),
        out_shape=jax.ShapeDtypeStruct((b, t, WIDTH), F32),
        scratch_shapes=[pltpu.VMEM((t, DSA_KV_HEADS * HEAD_DIM), MXU_DTYPE),
                        pltpu.VMEM((t // kc, DSA_KV_HEADS * HEAD_DIM, kc), MXU_DTYPE),
                        pltpu.VMEM((t, 3 * IDX_DIM), MXU_DTYPE),
                        pltpu.VMEM((t, qb), F32),
                        pltpu.VMEM((t, qb), F32)],
        compiler_params=pltpu.CompilerParams(dimension_semantics=("parallel", "arbitrary"),
                                             vmem_limit_bytes=VMEM_LIMIT),
        name="dsa",
    )(p, p, p, p, p, *params)


def _moba_kernel(q_ref, k_ref, v_ref, qn_ref, kn_ref, o_ref, k_s, vt_s, km_s, sel_s, bias_s):
    bs = q_ref.shape[1]
    t_len = k_ref.shape[1]
    n_kb = t_len // bs
    i = pl.program_id(1)

    @pl.when(i == 0)
    def _():
        kraw = k_ref[0]
        for h in range(N_HEADS):
            kn = _head_rms(_head(kraw, h), kn_ref[...])
            k_s[:, h * HEAD_DIM:(h + 1) * HEAD_DIM] = kn.astype(MXU_DTYPE)
            means = [jnp.mean(kn[j * bs:(j + 1) * bs], axis=0, keepdims=True) for j in range(n_kb)]
            km_s[h] = jnp.concatenate(means, axis=0)
        v = v_ref[0]
        for j in range(n_kb):
            vt_s[j] = v[j * bs:(j + 1) * bs, :].T.astype(MXU_DTYPE)
        dist_own = (_iota((1, bs), 1) - _iota((bs, 1), 0)).astype(F32)
        for h in range(N_HEADS):
            bias_s[:, h * bs:(h + 1) * bs] = MOBA_SLOPES[h] * dist_own

    qraw = q_ref[0]
    jcol = _iota((n_kb, 1), 0)
    own = pl.multiple_of(i * bs, bs)
    qs = []
    gates = []
    qn = qraw * lax.rsqrt(_head_sums(qraw * qraw, _head_segments(WIDTH)) * (1.0 / HEAD_DIM) + RMS_EPS)
    qn = qn * _lane_tile(qn_ref[...], N_HEADS)
    for h in range(N_HEADS):
        q = _head(qn, h)
        gates.append(_mm32_nt(km_s[h], q))
        qs.append((q * (HEAD_DIM ** -0.5)).astype(MXU_DTYPE))
    gate = jnp.where(jcol < i, jnp.concatenate(gates, axis=1), NEG_INF)
    rank = jnp.zeros(gate.shape, F32)
    for j2 in range(n_kb):
        other = gate[j2:j2 + 1, :]
        beats = (other > gate) | ((other == gate) & (j2 < jcol))
        rank = rank + jnp.where(beats, 1.0, 0.0)
    sel_s[...] = jnp.where((rank < float(MOBA_TOPK)) & (jcol < i), 0.0, jnp.inf)
    slopes = jnp.concatenate([jnp.full((1, bs), MOBA_SLOPES[h], F32) for h in range(N_HEADS)], axis=1)

    def scores(start):
        return jnp.concatenate(
            [lax.dot_general(k_s[pl.ds(start, bs), h * HEAD_DIM:(h + 1) * HEAD_DIM], qs[h], (((1,), (1,)), ((), ())),
                             preferred_element_type=F32) for h in range(N_HEADS)], axis=1)

    def weighted_values(j, e):
        vt = vt_s[j]
        eb = e.astype(MXU_DTYPE)
        return jnp.concatenate(
            [jnp.dot(vt[h * HEAD_DIM:(h + 1) * HEAD_DIM, :], eb[:, h * bs:(h + 1) * bs], preferred_element_type=F32)
             for h in range(N_HEADS)], axis=1)

    bias = bias_s[...]
    sc = jnp.where(bias >= 0, scores(own) - bias, NEG_INF)
    m0 = jnp.max(sc, axis=0, keepdims=True)
    e = jnp.exp(sc - m0)
    l0 = jnp.sum(e, axis=0, keepdims=True)
    acc0 = weighted_values(i, e)

    def block_step(j, carry):
        m, l, acc = carry
        shift = sel_s[pl.ds(j, 1), :] + slopes * ((i - j) * bs).astype(F32)
        s2 = scores(pl.multiple_of(j * bs, bs)) - bias_s[...] - shift
        m_new = jnp.maximum(m, jnp.max(s2, axis=0, keepdims=True))
        alpha = jnp.exp(m - m_new)
        e2 = jnp.exp(s2 - m_new)
        l = alpha * l + jnp.sum(e2, axis=0, keepdims=True)
        return m_new, l, alpha * acc + weighted_values(j, e2)

    _, l, acc = lax.fori_loop(0, i, block_step, (m0, l0, acc0))
    o_t = jnp.concatenate([acc[:, h * bs:(h + 1) * bs] / l[:, h * bs:(h + 1) * bs] for h in range(N_HEADS)], axis=0)
    o_ref[0] = o_t.T


def _moba(p, q_norm, k_norm):
    b, t, cols = p.shape
    bs = MOBA_BLOCK
    assert t % bs == 0 and cols == MOBA_COLS
    row = lambda v: v.reshape(1, -1).astype(F32)
    const = lambda i, j: (0, 0)
    return pl.pallas_call(
        _moba_kernel,
        grid=(b, t // bs),
        in_specs=[pl.BlockSpec((1, bs, WIDTH), lambda i, j: (i, j, 0)),
                  pl.BlockSpec((1, t, WIDTH), lambda i, j: (i, 0, 1)),
                  pl.BlockSpec((1, t, WIDTH), lambda i, j: (i, 0, 2)),
                  pl.BlockSpec((1, HEAD_DIM), const), pl.BlockSpec((1, HEAD_DIM), const)],
        out_specs=pl.BlockSpec((1, bs, WIDTH), lambda i, j: (i, j, 0)),
        out_shape=jax.ShapeDtypeStruct((b, t, WIDTH), F32),
        scratch_shapes=[pltpu.VMEM((t, WIDTH), MXU_DTYPE),
                        pltpu.VMEM((t // bs, WIDTH, bs), MXU_DTYPE),
                        pltpu.VMEM((N_HEADS, t // bs, HEAD_DIM), F32),
                        pltpu.VMEM((t // bs, N_HEADS * bs), F32),
                        pltpu.VMEM((bs, N_HEADS * bs), F32)],
        compiler_params=pltpu.CompilerParams(dimension_semantics=("parallel", "arbitrary"),
                                             vmem_limit_bytes=VMEM_LIMIT),
        name="moba",
    )(p, p, p, row(q_norm), row(k_norm))


def _pad_cols(w, n):
    return jnp.pad(w, ((0, 0), (0, n - w.shape[1])))


def _even_layer(x, b, t, norm_g, norm2_g, mlp_w1, mlp_w2, w_in, w_out, mu, w0, w2, a0, a2, g2, k_k, k_a, r_k,
                lnx_w, lnx_b, conv_w, a_log, dt_bias, gdn_norm_w):
    qkv_w = 3 * WIDTH
    w_rwkv = w_in[:, :RWKV_COLS]
    w_g = w_in[:, RWKV_COLS:]
    w_gdn = jnp.concatenate([w_g[:, :qkv_w], w_g[:, qkv_w + 2 * N_HEADS:],
                             _pad_cols(w_g[:, qkv_w:qkv_w + 2 * N_HEADS], LANES)], axis=1)
    p_rwkv, p_gdn = _norm_proj(x, norm_g, [w_rwkv.astype(MXU_DTYPE), w_gdn.astype(MXU_DTYPE)])
    o_a = _rwkv(p_rwkv.reshape(b, t, -1), mu, w0, w2, a0, a2, g2, k_k, k_a, r_k, lnx_w, lnx_b)
    o_b = _gdn(p_gdn.reshape(b, t, -1), conv_w, a_log, dt_bias, gdn_norm_w)
    return _out_mlp(x, o_a.reshape(b * t, -1), o_b.reshape(b * t, -1), w_out, norm2_g, mlp_w1, mlp_w2)


def _odd_layer(x, b, t, norm_g, norm2_g, mlp_w1, mlp_w2, w_in, w_out, dsa_q_norm, dsa_k_norm, idx_ln_w, idx_ln_b,
               moba_q_norm, moba_k_norm):
    kvw = 2 * DSA_KV_HEADS * HEAD_DIM
    sizes = (WIDTH, kvw, IDX_HEADS * IDX_DIM, IDX_DIM, IDX_HEADS, WIDTH, WIDTH, WIDTH)
    offs = [0]
    for s in sizes:
        offs.append(offs[-1] + s)
    dq, dkv, iq, ik, iw, mq, mk, mv = (w_in[:, offs[n]:offs[n + 1]] for n in range(len(sizes)))
    w_dsa = jnp.concatenate([dq, iq, dkv, _pad_cols(jnp.concatenate([ik, iw], axis=1), LANES)], axis=1)
    w_moba = jnp.concatenate([mq, mk, mv], axis=1)
    p_dsa, p_moba = _norm_proj(x, norm_g, [w_dsa.astype(MXU_DTYPE), w_moba.astype(MXU_DTYPE)])
    o_c = _dsa(p_dsa.reshape(b, t, -1), dsa_q_norm, dsa_k_norm, idx_ln_w, idx_ln_b)
    o_d = _moba(p_moba.reshape(b, t, -1), moba_q_norm, moba_k_norm)
    return _out_mlp(x, o_c.reshape(b * t, -1), o_d.reshape(b * t, -1), w_out, norm2_g, mlp_w1, mlp_w2)


def kernel(x, norm1_g, norm2_g, mlp_w1, mlp_w2, ev_w_in, ev_w_out, rwkv_mu, rwkv_w0, rwkv_w2, rwkv_a0, rwkv_a2, rwkv_g2, rwkv_k_k, rwkv_k_a, rwkv_r_k, rwkv_lnx_w, rwkv_lnx_b, gdn_conv_w, gdn_a_log, gdn_dt_bias, gdn_norm_w, od_w_in, od_w_out, dsa_q_norm, dsa_k_norm, idx_k_ln_w, idx_k_ln_b, moba_q_norm, moba_k_norm):
    b, t, d = x.shape
    depth = norm1_g.shape[0]
    h = x.reshape(b * t, d)
    for i in range(depth):
        j = i // 2
        if i % 2 == 0:
            h = _even_layer(h, b, t, norm1_g[i], norm2_g[i], mlp_w1[i], mlp_w2[i], ev_w_in[j], ev_w_out[j],
                            rwkv_mu[j], rwkv_w0[j], rwkv_w2[j], rwkv_a0[j], rwkv_a2[j], rwkv_g2[j], rwkv_k_k[j], rwkv_k_a[j], rwkv_r_k[j],
                            rwkv_lnx_w[j], rwkv_lnx_b[j], gdn_conv_w[j], gdn_a_log[j], gdn_dt_bias[j],
                            gdn_norm_w[j])
        else:
            h = _odd_layer(h, b, t, norm1_g[i], norm2_g[i], mlp_w1[i], mlp_w2[i], od_w_in[j], od_w_out[j],
                           dsa_q_norm[j], dsa_k_norm[j], idx_k_ln_w[j], idx_k_ln_b[j], moba_q_norm[j], moba_k_norm[j])
    return h.reshape(b, t, d)
```

```python
import functools
import math

import jax
import jax.numpy as jnp
from jax import lax
from jax.experimental import pallas as pl
from jax.experimental.pallas import tpu as pltpu

F32 = jnp.float32
MXU_DTYPE = jnp.bfloat16
HIGHEST = lax.Precision.HIGHEST

LANES = 128
VMEM_LIMIT = 56 * 1024 * 1024

HEAD_DIM = 64
N_HEADS = 8
WIDTH = N_HEADS * HEAD_DIM
RMS_EPS = 1e-6

RWKV_DECAY_LORA = 64
RWKV_A_LORA = 64
RWKV_GATE_LORA = 128
RWKV_COLS = 3 * WIDTH + RWKV_DECAY_LORA + RWKV_A_LORA + RWKV_GATE_LORA
RWKV_GN_EPS = 6.4e-4

GDN_CONV = 4
GDN_COLS = 3 * WIDTH + WIDTH + LANES
SCAN_CHUNK = 64
SCAN_BLOCK = 256

DSA_KV_HEADS = 2
DSA_GROUP = N_HEADS // DSA_KV_HEADS
IDX_HEADS = 8
IDX_DIM = 64
DSA_TOPK_MAX = 256
DSA_Q_BLOCK = 256
DSA_KEY_CHUNK = 256
DSA_COUNT_ROWS = 256
DSA_COLS = 2 * WIDTH + 2 * DSA_KV_HEADS * HEAD_DIM + LANES

MOBA_BLOCK = 256
MOBA_TOPK = 3
MOBA_COLS = 3 * WIDTH

PROJ_ROWS = 512
MLP_ROWS = 1024
MLP_FF_COLS = 1024

ALIBI_HEADS = 2 * N_HEADS
NEG_INF = float("-inf")


def _alibi_slope(i):
    return 2.0 ** (-8.0 * (i + 1) / ALIBI_HEADS)


DSA_SLOPES = tuple(_alibi_slope(2 * h) for h in range(N_HEADS))
MOBA_SLOPES = tuple(_alibi_slope(2 * h + 1) for h in range(N_HEADS))


def _mm(a, b):
    return jnp.dot(a.astype(MXU_DTYPE), b.astype(MXU_DTYPE), preferred_element_type=F32)


def _mm32_nt(a, b):
    return lax.dot_general(a, b, (((1,), (1,)), ((), ())), preferred_element_type=F32, precision=HIGHEST)


def _iota(shape, dim):
    return lax.broadcasted_iota(jnp.int32, shape, dim)


def _sigmoid(x):
    return 1.0 / (1.0 + jnp.exp(-x))


def _silu(x):
    return x * _sigmoid(x)


def _softplus(x):
    return jnp.maximum(x, 0.0) + jnp.log1p(jnp.exp(-jnp.abs(x)))


def _head(x, h):
    return x[:, h * HEAD_DIM:(h + 1) * HEAD_DIM]


def _split_hi_lo(x):
    hi = x.astype(MXU_DTYPE)
    lo = (x - hi.astype(F32)).astype(MXU_DTYPE)
    return hi, lo


def _head_segments(width):
    return (_iota((width, width), 0) // HEAD_DIM == _iota((width, width), 1) // HEAD_DIM).astype(MXU_DTYPE)


def _head_sums(x, seg):
    hi, lo = _split_hi_lo(x)
    return jnp.dot(hi, seg, preferred_element_type=F32) + jnp.dot(lo, seg, preferred_element_type=F32)


def _tree_sum(parts):
    parts = list(parts)
    while len(parts) > 1:
        nxt = [parts[a] + parts[a + 1] for a in range(0, len(parts) - 1, 2)]
        if len(parts) % 2:
            nxt.append(parts[-1])
        parts = nxt
    return parts[0]


def _lane_tile(x, n):
    return jnp.concatenate([x] * n, axis=1)


def _norm_proj_kernel(x_ref, g_ref, *refs):
    n = len(refs) // 2
    x = x_ref[...]
    h = x * lax.rsqrt(jnp.mean(x * x, axis=-1, keepdims=True) + RMS_EPS) * g_ref[...]
    h = h.astype(MXU_DTYPE)
    for w_ref, o_ref in zip(refs[:n], refs[n:]):
        o_ref[...] = jnp.dot(h, w_ref[...], preferred_element_type=F32)


def _norm_proj(x, g, ws, tm=PROJ_ROWS):
    n, d = x.shape
    assert n % tm == 0
    const = lambda i: (0, 0)
    return pl.pallas_call(
        _norm_proj_kernel,
        grid=(n // tm,),
        in_specs=[pl.BlockSpec((tm, d), lambda i: (i, 0)), pl.BlockSpec((1, d), const)]
        + [pl.BlockSpec(w.shape, const) for w in ws],
        out_specs=[pl.BlockSpec((tm, w.shape[1]), lambda i: (i, 0)) for w in ws],
        out_shape=[jax.ShapeDtypeStruct((n, w.shape[1]), F32) for w in ws],
        compiler_params=pltpu.CompilerParams(dimension_semantics=("parallel",), vmem_limit_bytes=VMEM_LIMIT),
        name="norm_proj",
    )(x, g.reshape(1, d), *ws)


def _out_mlp_kernel(x_ref, a_ref, b_ref, wa_ref, wb_ref, g_ref, w1_ref, w2_ref, o_ref, h_ref, acc_ref):
    j = pl.program_id(1)

    @pl.when(j == 0)
    def _():
        x = x_ref[...] + _mm(a_ref[...], wa_ref[...]) + _mm(b_ref[...], wb_ref[...])
        h = x * lax.rsqrt(jnp.mean(x * x, axis=-1, keepdims=True) + RMS_EPS) * g_ref[...]
        h_ref[...] = h.astype(MXU_DTYPE)
        acc_ref[...] = x

    u = jnp.maximum(jnp.dot(h_ref[...], w1_ref[...], preferred_element_type=F32), 0.0)
    acc_ref[...] += jnp.dot((u * u).astype(MXU_DTYPE), w2_ref[...], preferred_element_type=F32)

    @pl.when(j == pl.num_programs(1) - 1)
    def _():
        o_ref[...] = acc_ref[...]


def _out_mlp(x, a, b, w_out, g, w1, w2, tm=MLP_ROWS, tf=MLP_FF_COLS):
    n, d = x.shape
    f = w1.shape[1]
    tm = min(tm, n)
    tf = min(tf, f)
    assert n % tm == 0 and f % tf == 0
    wa = w_out[:a.shape[1]].astype(MXU_DTYPE)
    wb = w_out[a.shape[1]:].astype(MXU_DTYPE)
    row = lambda i, j: (i, 0)
    const = lambda i, j: (0, 0)
    return pl.pallas_call(
        _out_mlp_kernel,
        grid=(n // tm, f // tf),
        in_specs=[pl.BlockSpec((tm, d), row), pl.BlockSpec((tm, a.shape[1]), row), pl.BlockSpec((tm, b.shape[1]), row),
                  pl.BlockSpec(wa.shape, const), pl.BlockSpec(wb.shape, const), pl.BlockSpec((1, d), const),
                  pl.BlockSpec((d, tf), lambda i, j: (0, j)), pl.BlockSpec((tf, d), lambda i, j: (j, 0))],
        out_specs=pl.BlockSpec((tm, d), row),
        out_shape=jax.ShapeDtypeStruct((n, d), F32),
        scratch_shapes=[pltpu.VMEM((tm, d), MXU_DTYPE), pltpu.VMEM((tm, d), F32)],
        compiler_params=pltpu.CompilerParams(dimension_semantics=("parallel", "arbitrary"),
                                             vmem_limit_bytes=VMEM_LIMIT),
        name="out_mlp",
    )(x, a, b, wa, wb, g.reshape(1, d), w1.astype(MXU_DTYPE), w2.astype(MXU_DTYPE))


def _bmm(a, b):
    return lax.dot_general(a.astype(MXU_DTYPE), b.astype(MXU_DTYPE), (((2,), (1,)), ((0,), (0,))),
                           preferred_element_type=F32)


def _bmm_nt(a, b):
    return lax.dot_general(a.astype(MXU_DTYPE), b.astype(MXU_DTYPE), (((2,), (2,)), ((0,), (0,))),
                           preferred_element_type=F32)


def _bmm_tn(a, b):
    return lax.dot_general(a.astype(MXU_DTYPE), b.astype(MXU_DTYPE), (((1,), (1,)), ((0,), (0,))),
                           preferred_element_type=F32)


def _stack_heads(x, nc, c):
    return jnp.stack([x[i * c:(i + 1) * c, h * HEAD_DIM:(h + 1) * HEAD_DIM] for i in range(nc) for h in range(N_HEADS)])


def _unstack_heads(y, nc):
    return jnp.concatenate([jnp.concatenate([y[i * N_HEADS + h] for h in range(N_HEADS)], axis=-1)
                            for i in range(nc)], axis=0)


def _batched_unit_lower_solve(x, y):
    c = x.shape[1]
    y = y + _bmm(x, y)
    p = 2
    while p < c:
        x = _bmm(x, x)
        y = y + _bmm(x, y)
        p *= 2
    return y


def _split3(x):
    hi = x.astype(MXU_DTYPE)
    r = x - hi.astype(F32)
    mid = r.astype(MXU_DTYPE)
    lo = (r - mid.astype(F32)).astype(MXU_DTYPE)
    return hi, mid, lo


def _chunk_cumsum_rows(x, c):
    tri = (_iota((c, c), 0) >= _iota((c, c), 1)).astype(MXU_DTYPE)
    parts = _split3(x)
    return jnp.concatenate(
        [_tree_sum([jnp.dot(tri, p[i * c:(i + 1) * c], preferred_element_type=F32) for p in parts])
         for i in range(x.shape[0] // c)], axis=0)


def _chunk_cumsum_cols(x, c):
    tri = (_iota((c, c), 0) <= _iota((c, c), 1)).astype(MXU_DTYPE)
    parts = _split3(x)
    return jnp.concatenate(
        [_tree_sum([jnp.dot(p[:, i * c:(i + 1) * c], tri, preferred_element_type=F32) for p in parts])
         for i in range(x.shape[1] // c)], axis=1)


def _run_chunks(st_ref, m, q, oq, o0, gamma, nc):
    s = st_ref[...]
    outs = []
    for i in range(nc):
        g = slice(i * N_HEADS, (i + 1) * N_HEADS)
        mo = _bmm(jnp.concatenate([m[g], oq[g]], axis=1), s)
        outs.append(mo[:, HEAD_DIM:] + o0[g])
        s = gamma[g] * s + mo[:, :HEAD_DIM] + q[g]
    st_ref[...] = s
    return jnp.concatenate(outs, axis=0)


def _rwkv_kernel(p_ref, mu_ref, w0_ref, w2_ref, a0_ref, a2_ref, g2_ref, kk_ref, ka_ref, rk_ref, lnw_ref, lnb_ref,
                 o_ref, prev_ref, st_ref):
    tb = p_ref.shape[1]
    c = SCAN_CHUNK
    nc = tb // c

    @pl.when(pl.program_id(1) == 0)
    def _():
        prev_ref[...] = jnp.zeros_like(prev_ref)
        st_ref[...] = jnp.zeros_like(st_ref)

    p = p_ref[0]
    shifted = jnp.where(_iota((tb, 1), 0) == 0, prev_ref[...], pltpu.roll(p, 1, 0))
    prev_ref[...] = p[tb - 1:tb, :]
    x = p + (shifted - p) * mu_ref[...]

    r = x[:, 0:WIDTH]
    k = x[:, WIDTH:2 * WIDTH]
    v = x[:, 2 * WIDTH:3 * WIDTH]
    off = 3 * WIDTH
    wd = x[:, off:off + RWKV_DECAY_LORA]
    ad = x[:, off + RWKV_DECAY_LORA:off + RWKV_DECAY_LORA + RWKV_A_LORA]
    gd = x[:, off + RWKV_DECAY_LORA + RWKV_A_LORA:]

    w_pre = w0_ref[...] + _mm(jnp.tanh(wd), w2_ref[...])
    log_w = -(_sigmoid(w_pre) * math.exp(-0.5))
    a = _sigmoid(a0_ref[...] + _mm(ad, a2_ref[...]))
    g = _mm(_sigmoid(gd), g2_ref[...])
    kk_all = k * kk_ref[...]
    k = k * (1.0 + (a - 1.0) * ka_ref[...])

    gam = _chunk_cumsum_rows(log_w, c)
    gam_last = jnp.concatenate([jnp.broadcast_to(gam[(i + 1) * c - 1:(i + 1) * c, :], (c, WIDTH)) for i in range(nc)],
                               axis=0)

    seg = _head_segments(WIDTH)
    kk = kk_all * lax.rsqrt(_head_sums(kk_all * kk_all, seg) + 1e-6)
    b = kk * a
    e_neg = jnp.exp(-gam)
    tail = jnp.exp(gam_last - gam)

    st = lambda y: _stack_heads(y, nc, c)
    v_h = st(v)
    a_t = st(-kk * jnp.exp(gam - log_w))
    r_t = st(r * jnp.exp(gam))
    b_t = st(b * e_neg)
    k_t = st(k * e_neg)
    e_last = st(jnp.exp(gam_last))[:, 0:1, :]

    ri = _iota((c, c), 0)
    ci = _iota((c, c), 1)
    incl = (ri >= ci)[None]
    strict = (ri > ci)[None]
    eye = (ri == ci)[None]

    lhs = jnp.concatenate([a_t, r_t], axis=1)
    m_b = _bmm_nt(lhs, b_t)
    m_k = _bmm_nt(lhs, k_t)
    a_rb = jnp.where(incl, m_b[:, c:], 0.0)
    akv = _bmm(jnp.where(strict, m_k[:, :c], 0.0), v_h)
    wu = _batched_unit_lower_solve(jnp.where(strict, m_b[:, :c], 0.0), jnp.concatenate([a_t, akv], axis=-1))
    ro = _bmm(a_rb, wu)
    oq = r_t + ro[:, :, :HEAD_DIM]
    o0 = ro[:, :, HEAD_DIM:] + _bmm(jnp.where(incl, m_k[:, c:], 0.0), v_h)
    mq = _bmm_tn(st(b * tail), wu)
    m = mq[:, :, :HEAD_DIM]
    q = mq[:, :, HEAD_DIM:] + _bmm_tn(st(k * tail), v_h)
    gamma = jnp.sum(jnp.where(eye, e_last, 0.0), axis=2, keepdims=True)

    o = _unstack_heads(_run_chunks(st_ref, m, q, oq, o0, gamma, nc), nc)

    oc = o - _head_sums(o, seg) * (1.0 / HEAD_DIM)
    on = oc * lax.rsqrt(_head_sums(oc * oc, seg) * (1.0 / HEAD_DIM) + RWKV_GN_EPS)
    bonus = _head_sums(r * k * rk_ref[...], seg) * v
    o_ref[0] = (on * lnw_ref[...] + lnb_ref[...] + bonus) * g


def _rwkv(p, mu, w0, w2, a0, a2, g2, k_k, k_a, r_k, lnx_w, lnx_b, tb=SCAN_BLOCK):
    b, t, cols = p.shape
    tb = min(tb, t)
    assert t % tb == 0 and tb % SCAN_CHUNK == 0 and cols == RWKV_COLS
    row = lambda v: v.reshape(1, -1).astype(F32)
    params = [row(mu), row(w0), w2.astype(MXU_DTYPE), row(a0), a2.astype(MXU_DTYPE), g2.astype(MXU_DTYPE),
              row(k_k), row(k_a), row(r_k), row(lnx_w), row(lnx_b)]
    const = lambda i, j: (0, 0)
    return pl.pallas_call(
        _rwkv_kernel,
        grid=(b, t // tb),
        in_specs=[pl.BlockSpec((1, tb, cols), lambda i, j: (i, j, 0))] + [pl.BlockSpec(q.shape, const) for q in params],
        out_specs=pl.BlockSpec((1, tb, WIDTH), lambda i, j: (i, j, 0)),
        out_shape=jax.ShapeDtypeStruct((b, t, WIDTH), F32),
        scratch_shapes=[pltpu.VMEM((1, cols), F32), pltpu.VMEM((N_HEADS, HEAD_DIM, HEAD_DIM), F32)],
        compiler_params=pltpu.CompilerParams(dimension_semantics=("parallel", "arbitrary"),
                                             vmem_limit_bytes=VMEM_LIMIT),
        name="rwkv7",
    )(p, *params)


def _gdn_kernel(p_ref, cw_ref, alog_r_ref, dt_r_ref, alog_c_ref, dt_c_ref, nw_ref, o_ref, xpad_ref, st_ref):
    tb = p_ref.shape[1]
    c = SCAN_CHUNK
    nc = tb // c
    qkv_w = 3 * WIDTH

    @pl.when(pl.program_id(1) == 0)
    def _():
        xpad_ref[:8, :] = jnp.zeros((8, xpad_ref.shape[1]), F32)
        st_ref[...] = jnp.zeros_like(st_ref)

    p = p_ref[0]
    xin = p[:, :qkv_w]
    z = p[:, qkv_w:qkv_w + WIDTH]
    b_in = p[:, qkv_w + WIDTH:qkv_w + WIDTH + N_HEADS]
    a_in = p[:, qkv_w + WIDTH + N_HEADS:qkv_w + WIDTH + 2 * N_HEADS]

    xpad_ref[8:, :] = xin
    conv = xin * cw_ref[GDN_CONV - 1:GDN_CONV, :]
    for s in range(1, GDN_CONV):
        conv = conv + xpad_ref[8 - s:8 - s + tb, :] * cw_ref[GDN_CONV - 1 - s:GDN_CONV - s, :]
    xpad_ref[:8, :] = xin[tb - 8:, :]
    qkv = _silu(conv)

    beta = _sigmoid(b_in)
    g_col = -jnp.exp(alog_r_ref[...]) * _softplus(a_in + dt_r_ref[...])
    gc_col = _chunk_cumsum_rows(g_col, c)
    eye_h = (_iota((N_HEADS, N_HEADS), 0) == _iota((N_HEADS, N_HEADS), 1)).astype(F32)
    a_row = _mm32_nt(eye_h, a_in)
    g_row = -jnp.exp(alog_c_ref[...]) * _softplus(a_row + dt_c_ref[...])
    gc_row = _chunk_cumsum_cols(g_row, c)

    def per_head_cols(y):
        return jnp.stack([y[i * c:(i + 1) * c, h:h + 1] for i in range(nc) for h in range(N_HEADS)])

    def over_head_lanes(y):
        expand = (_iota((N_HEADS, WIDTH), 1) // HEAD_DIM == _iota((N_HEADS, WIDTH), 0)).astype(MXU_DTYPE)
        return _tree_sum([jnp.dot(part, expand, preferred_element_type=F32) for part in _split3(y)])

    gc = per_head_cols(gc_col)
    g_rows = jnp.stack([gc_row[h:h + 1, i * c:(i + 1) * c] for i in range(nc) for h in range(N_HEADS)])
    gc_d = over_head_lanes(gc_col)
    beta_d = over_head_lanes(beta)
    g_last_d = jnp.concatenate([jnp.broadcast_to(gc_d[(i + 1) * c - 1:(i + 1) * c, :], (c, WIDTH)) for i in range(nc)],
                               axis=0)

    seg = _head_segments(WIDTH)
    q = qkv[:, :WIDTH]
    k = qkv[:, WIDTH:2 * WIDTH]
    q = q * lax.rsqrt(_head_sums(q * q, seg) + 1e-6) * (HEAD_DIM ** -0.5)
    k = k * lax.rsqrt(_head_sums(k * k, seg) + 1e-6)
    kb_d = k * beta_d
    e_g = jnp.exp(gc_d)

    st = lambda y: _stack_heads(y, nc, c)
    q_h, k_h, kb = st(q), st(k), st(kb_d)

    ri = _iota((c, c), 0)
    ci = _iota((c, c), 1)
    incl = (ri >= ci)[None]
    strict = (ri > ci)[None]
    decay = jnp.exp(jnp.where(incl, gc - g_rows, NEG_INF))
    mm = _bmm_nt(jnp.concatenate([kb, q_h], axis=1), k_h)
    a_mat = jnp.where(strict, mm[:, :c] * decay, 0.0)
    qk = jnp.where(incl, mm[:, c:] * decay, 0.0)
    rhs = jnp.concatenate([st(qkv[:, 2 * WIDTH:] * beta_d), st(kb_d * e_g)], axis=-1)
    uw = _batched_unit_lower_solve(-a_mat, rhs)
    qkuw = _bmm(qk, uw)
    o0 = qkuw[:, :, :HEAD_DIM]
    oq = st(q * e_g) - qkuw[:, :, HEAD_DIM:]
    kuw = _bmm_tn(st(k * jnp.exp(g_last_d - gc_d)), uw)
    gamma = jnp.exp(gc[:, c - 1:c, :])
    o = _run_chunks(st_ref, -kuw[:, :, HEAD_DIM:], kuw[:, :, :HEAD_DIM], oq, o0, gamma, nc)

    o = _unstack_heads(o, nc)
    o = o * lax.rsqrt(_head_sums(o * o, seg) * (1.0 / HEAD_DIM) + RMS_EPS) * _lane_tile(nw_ref[...], N_HEADS)
    o_ref[0] = o * _silu(z)


def _gdn(p, conv_w, a_log, dt_bias, norm_w, tb=SCAN_BLOCK):
    b, t, cols = p.shape
    tb = min(tb, t)
    assert t % tb == 0 and tb % SCAN_CHUNK == 0 and cols == GDN_COLS
    params = [conv_w.astype(F32), a_log.reshape(1, -1), dt_bias.reshape(1, -1), a_log.reshape(-1, 1),
              dt_bias.reshape(-1, 1), norm_w.reshape(1, -1)]
    const = lambda i, j: (0, 0)
    return pl.pallas_call(
        _gdn_kernel,
        grid=(b, t // tb),
        in_specs=[pl.BlockSpec((1, tb, cols), lambda i, j: (i, j, 0))] + [pl.BlockSpec(q.shape, const) for q in params],
        out_specs=pl.BlockSpec((1, tb, WIDTH), lambda i, j: (i, j, 0)),
        out_shape=jax.ShapeDtypeStruct((b, t, WIDTH), F32),
        scratch_shapes=[pltpu.VMEM((8 + tb, 3 * WIDTH), F32), pltpu.VMEM((N_HEADS, HEAD_DIM, HEAD_DIM), F32)],
        compiler_params=pltpu.CompilerParams(dimension_semantics=("parallel", "arbitrary"),
                                             vmem_limit_bytes=VMEM_LIMIT),
        name="gdn",
    )(p, *params)


def _key_to_float(key):
    return pltpu.bitcast(jnp.where(key >= 0, key, key ^ jnp.int32(0x7FFFFFFF)), F32)


NEG_INF_KEY = -2139095041
COARSE_DTYPE = jnp.bfloat16
SOFTMAX_FLOOR = -1e30


def _dsa_kernel(dq_ref, iq_ref, kv_ref, ikw_ref, iwq_ref, qn_ref, kn_ref, lnw_ref, lnb_ref, o_ref,
                k_s, vt_s, ik_s, score_s, coarse_s, distm_s, *, top_k):
    qb = dq_ref.shape[1]
    t_len = kv_ref.shape[1]
    kc = DSA_KEY_CHUNK
    n_chunks = t_len // kc
    kvw = DSA_KV_HEADS * HEAD_DIM
    i = pl.program_id(1)
    n_live = (i * qb + qb + kc - 1) // kc

    @pl.when(i == 0)
    def _():
        kv = kv_ref[0]
        kraw = kv[:, :kvw]
        kn = kraw * lax.rsqrt(_head_sums(kraw * kraw, _head_segments(kvw)) * (1.0 / HEAD_DIM) + RMS_EPS)
        k_s[...] = (kn * _lane_tile(kn_ref[...], DSA_KV_HEADS)).astype(MXU_DTYPE)
        for c in range(n_chunks):
            vt_s[c] = kv[c * kc:(c + 1) * kc, kvw:].T.astype(MXU_DTYPE)
        ik = ikw_ref[0][:, :IDX_DIM]
        ikc = ik - jnp.mean(ik, axis=-1, keepdims=True)
        ik = ikc * lax.rsqrt(jnp.mean(ikc * ikc, axis=-1, keepdims=True) + 1e-6) * lnw_ref[...] + lnb_ref[...]
        hi, lo = _split_hi_lo(ik)
        ik_s[...] = jnp.concatenate([hi, hi, lo], axis=-1)
        score_s[...] = jnp.full(score_s.shape, NEG_INF, F32)

    t_row = i * qb + _iota((1, qb), 1)
    s_loc = _iota((kc, 1), 0)

    eye_h = (_iota((IDX_HEADS, IDX_HEADS), 0) == _iota((IDX_HEADS, IDX_HEADS), 1)).astype(F32)
    iw_t = _mm32_nt(eye_h, iwq_ref[0][:, IDX_DIM:IDX_DIM + IDX_HEADS]) * (IDX_HEADS ** -0.5 * IDX_DIM ** -0.5)
    iw_wide = jnp.concatenate([iw_t[h:h + 1, :] for h in range(IDX_HEADS)], axis=1)
    iq = iq_ref[0]
    iq_cat = []
    for h in range(IDX_HEADS):
        hi, lo = _split_hi_lo(_head(iq, h))
        iq_cat.append(jnp.concatenate([hi, lo, hi], axis=-1))
    iq_all = jnp.concatenate(iq_cat, axis=0)

    def score_chunk(c, carry):
        rows = pl.ds(pl.multiple_of(c * kc, kc), kc)
        dots = lax.dot_general(ik_s[rows, :], iq_all, (((1,), (1,)), ((), ())), preferred_element_type=F32)
        w = iw_wide * jnp.maximum(dots, 0.0)
        sc = _tree_sum([w[:, h * qb:(h + 1) * qb] for h in range(IDX_HEADS)])
        sc = jnp.where(c * kc + s_loc <= t_row, sc, NEG_INF)
        score_s[rows, :] = sc
        coarse_s[rows, :] = sc.astype(COARSE_DTYPE)
        return carry

    lax.fori_loop(0, n_live, score_chunk, 0)

    kf = float(top_k)
    grp = DSA_COUNT_ROWS
    n_grp = (n_live * kc + grp - 1) // grp

    def count(pred):
        def body(g, acc):
            blk = score_s[pl.ds(pl.multiple_of(g * grp, grp), grp), :]
            ones = jnp.where(pred(blk), 1.0, 0.0)
            return acc + _tree_sum([ones[r * 8:(r + 1) * 8] for r in range(grp // 8)])
        acc = lax.fori_loop(0, n_grp, body, jnp.zeros((8, qb), F32))
        return jnp.sum(acc, axis=0, keepdims=True)

    def count_coarse(cand):
        def body(g, acc):
            blk = coarse_s[pl.ds(pl.multiple_of(g * grp, grp), grp), :]
            ones = jnp.where(blk >= cand, jnp.ones((), COARSE_DTYPE), jnp.zeros((), COARSE_DTYPE))
            return acc + _tree_sum([ones[r * 16:(r + 1) * 16] for r in range(grp // 16)]).astype(F32)
        acc = lax.fori_loop(0, n_grp, body, jnp.zeros((16, qb), F32))
        return jnp.sum(acc, axis=0, keepdims=True)

    def coarse_key(base):
        return jnp.where(base >= 0, base, base | jnp.int32(0xFFFF))

    int_min = jnp.int32(-2 ** 31)
    base = jnp.where(count_coarse(jnp.zeros((1, qb), COARSE_DTYPE)) >= kf, jnp.int32(0), int_min)

    def coarse_step(b, base):
        cand = base | (jnp.int32(1) << (30 - b))
        cand_f = _key_to_float(coarse_key(cand)).astype(COARSE_DTYPE)
        return jnp.where(count_coarse(cand_f) >= kf, cand, base)

    base = lax.fori_loop(0, 15, coarse_step, base)
    lo_key = coarse_key(base) - jnp.int32(2 ** 15)

    def fine_step(b, off):
        cand_off = off | (jnp.int32(1) << (16 - b))
        cand_f = _key_to_float(lo_key + cand_off)
        return jnp.where(count(lambda blk: blk >= cand_f) >= kf, cand_off, off)

    tau = lo_key + lax.fori_loop(0, 17, fine_step, jnp.zeros((1, qb), jnp.int32))
    tau_f = _key_to_float(jnp.maximum(tau, jnp.int32(NEG_INF_KEY)))
    need = kf - count(lambda blk: blk > tau_f)

    n_ge = count(lambda blk: blk >= tau_f)
    no_partial_tie = jnp.min(jnp.where((n_ge <= kf) | (tau_f == NEG_INF), 1.0, 0.0)) > 0.5

    def masked_distance(c, sel_fn):
        rows = pl.ds(pl.multiple_of(c * kc, kc), kc)
        s_pos = c * kc + s_loc
        sel = sel_fn(score_s[rows, :]) & (s_pos <= t_row)
        distm_s[rows, :] = jnp.where(sel, (t_row - s_pos).astype(F32), jnp.inf)

    @pl.when(no_partial_tie)
    def _():
        def select_chunk(c, carry):
            masked_distance(c, lambda blk: blk >= tau_f)
            return carry
        lax.fori_loop(0, n_live, select_chunk, 0)

    @pl.when(jnp.logical_not(no_partial_tie))
    def _():
        lower = (_iota((kc, kc), 0) >= _iota((kc, kc), 1)).astype(MXU_DTYPE)

        def select_chunk(c, run):
            def sel_fn(blk):
                eq = blk == tau_f
                pref = run + jnp.dot(lower, jnp.where(eq, 1.0, 0.0).astype(MXU_DTYPE), preferred_element_type=F32)
                return (blk > tau_f) | (eq & (pref <= need))
            masked_distance(c, sel_fn)
            rows = pl.ds(pl.multiple_of(c * kc, kc), kc)
            return run + jnp.sum(jnp.where(score_s[rows, :] == tau_f, 1.0, 0.0), axis=0, keepdims=True)
        lax.fori_loop(0, n_live, select_chunk, jnp.zeros((1, qb), F32))

    dq = dq_ref[0]
    dq = dq * lax.rsqrt(_head_sums(dq * dq, _head_segments(WIDTH)) * (1.0 / HEAD_DIM) + RMS_EPS)
    dq = (dq * _lane_tile(qn_ref[...] * (HEAD_DIM ** -0.5), N_HEADS)).astype(MXU_DTYPE)
    qs = [_head(dq, h) for h in range(N_HEADS)]
    q_grp = [jnp.concatenate(qs[g * DSA_GROUP:(g + 1) * DSA_GROUP], axis=0) for g in range(DSA_KV_HEADS)]
    slopes = jnp.concatenate([jnp.full((1, qb), DSA_SLOPES[h], F32) for h in range(N_HEADS)], axis=1)
    gw = DSA_GROUP * qb

    def attend_chunk(c, carry):
        m, l, acc = carry
        rows = pl.ds(pl.multiple_of(c * kc, kc), kc)
        kch = k_s[rows, :]
        vt = vt_s[c]
        sc = jnp.concatenate(
            [lax.dot_general(kch[:, g * HEAD_DIM:(g + 1) * HEAD_DIM], q_grp[g], (((1,), (1,)), ((), ())),
                             preferred_element_type=F32) for g in range(DSA_KV_HEADS)], axis=1)
        sc = sc - slopes * _lane_tile(distm_s[rows, :], N_HEADS)
        m_new = jnp.maximum(m, jnp.max(sc, axis=0, keepdims=True))
        alpha = jnp.exp(m - m_new)
        e = jnp.exp(sc - m_new)
        l = alpha * l + jnp.sum(e, axis=0, keepdims=True)
        eb = e.astype(MXU_DTYPE)
        pv = jnp.concatenate(
            [jnp.dot(vt[g * HEAD_DIM:(g + 1) * HEAD_DIM, :], eb[:, g * gw:(g + 1) * gw], preferred_element_type=F32)
             for g in range(DSA_KV_HEADS)], axis=1)
        return m_new, l, alpha * acc + pv

    init = (jnp.full((1, N_HEADS * qb), SOFTMAX_FLOOR, F32), jnp.zeros((1, N_HEADS * qb), F32),
            jnp.zeros((HEAD_DIM, N_HEADS * qb), F32))
    _, l, acc = lax.fori_loop(0, n_live, attend_chunk, init)
    o_t = jnp.concatenate([acc[:, h * qb:(h + 1) * qb] / l[:, h * qb:(h + 1) * qb] for h in range(N_HEADS)], axis=0)
    o_ref[0] = o_t.T


def _dsa(p, q_norm, k_norm, ln_w, ln_b):
    b, t, cols = p.shape
    qb = DSA_Q_BLOCK
    kc = DSA_KEY_CHUNK
    assert t % qb == 0 and cols == DSA_COLS and t % DSA_COUNT_ROWS == 0 and t % kc == 0 and kc % DSA_COUNT_ROWS == 0
    top_k = min(DSA_TOPK_MAX, t // 4)
    kvw = 2 * DSA_KV_HEADS * HEAD_DIM
    row = lambda v: v.reshape(1, -1).astype(F32)
    params = [row(q_norm), row(k_norm), row(ln_w), row(ln_b)]
    const = lambda i, j: (0, 0)
    return pl.pallas_call(
        functools.partial(_dsa_kernel, top_k=top_k),
        grid=(b, t // qb),
        in_specs=[pl.BlockSpec((1, qb, WIDTH), lambda i, j: (i, j, 0)),
                  pl.BlockSpec((1, qb, WIDTH), lambda i, j: (i, j, 1)),
                  pl.BlockSpec((1, t, kvw), lambda i, j: (i, 0, 2 * WIDTH // kvw)),
                  pl.BlockSpec((1, t, LANES), lambda i, j: (i, 0, (2 * WIDTH + kvw) // LANES)),
                  pl.BlockSpec((1, qb, LANES), lambda i, j: (i, j, (2 * WIDTH + kvw) // LANES))]
        + [pl.BlockSpec(q.shape, const) for q in params],
        out_specs=pl.BlockSpec((1, qb, WIDTH), lambda i, j: (i, j, 0)),
        out_shape=jax.ShapeDtypeStruct((b, t, WIDTH), F32),
        scratch_shapes=[pltpu.VMEM((t, DSA_KV_HEADS * HEAD_DIM), MXU_DTYPE),
                        pltpu.VMEM((t // kc, DSA_KV_HEADS * HEAD_DIM, kc), MXU_DTYPE),
                        pltpu.VMEM((t, 3 * IDX_DIM), MXU_DTYPE),
                        pltpu.VMEM((t, qb), F32),
                        pltpu.VMEM((t, qb), COARSE_DTYPE),
                        pltpu.VMEM((t, qb), F32)],
        compiler_params=pltpu.CompilerParams(dimension_semantics=("parallel", "arbitrary"),
                                             vmem_limit_bytes=VMEM_LIMIT),
        name="dsa",
    )(p, p, p, p, p, *params)


MOBA_PENALTY = 1e30


def _moba_kernel(q_ref, k_ref, v_ref, qn_ref, kn_ref, o_ref, k_s, vt_s, km_s, causal_s):
    bs = q_ref.shape[1]
    t_len = k_ref.shape[1]
    n_kb = t_len // bs
    i = pl.program_id(1)
    fcol = _iota((1, HEAD_DIM), 1)

    @pl.when(i == 0)
    def _():
        kraw = k_ref[0]
        kn = kraw * lax.rsqrt(_head_sums(kraw * kraw, _head_segments(WIDTH)) * (1.0 / HEAD_DIM) + RMS_EPS)
        kn = kn * _lane_tile(kn_ref[...], N_HEADS)
        s_pos = _iota((t_len, 1), 0)
        extra = jnp.where(fcol == 0, (s_pos % bs).astype(F32),
                          jnp.where((fcol == 1) | (fcol == 2 + s_pos // bs), 1.0, 0.0)).astype(MXU_DTYPE)
        for h in range(N_HEADS):
            k_s[:, 2 * h * HEAD_DIM:(2 * h + 1) * HEAD_DIM] = _head(kn, h).astype(MXU_DTYPE)
            k_s[:, (2 * h + 1) * HEAD_DIM:(2 * h + 2) * HEAD_DIM] = extra
        means = jnp.concatenate([jnp.mean(kn[j * bs:(j + 1) * bs], axis=0, keepdims=True) for j in range(n_kb)], axis=0)
        for h in range(N_HEADS):
            km_s[h] = _head(means, h)
        v = v_ref[0]
        for j in range(n_kb):
            vt_s[j] = v[j * bs:(j + 1) * bs, :].T.astype(MXU_DTYPE)
        above = _iota((bs, 1), 0) > _iota((1, bs), 1)
        causal_s[...] = _lane_tile(jnp.where(above, NEG_INF, 0.0), N_HEADS)

    qraw = q_ref[0]
    jcol = _iota((n_kb, 1), 0)
    own = pl.multiple_of(i * bs, bs)
    qn = qraw * lax.rsqrt(_head_sums(qraw * qraw, _head_segments(WIDTH)) * (1.0 / HEAD_DIM) + RMS_EPS)
    qn = qn * _lane_tile(qn_ref[...], N_HEADS)
    gates = [_mm32_nt(km_s[h], _head(qn, h)) for h in range(N_HEADS)]
    gate = jnp.where(jcol < i, jnp.concatenate(gates, axis=1), NEG_INF)
    rank = jnp.zeros(gate.shape, F32)
    for j2 in range(n_kb):
        other = gate[j2:j2 + 1, :]
        beats = (other > gate) | ((other == gate) & (j2 < jcol))
        rank = rank + jnp.where(beats, 1.0, 0.0)
    picked = (rank < float(MOBA_TOPK)) & (jcol < i)
    slopes = jnp.concatenate([jnp.full((1, bs), MOBA_SLOPES[h], F32) for h in range(N_HEADS)], axis=1)
    block_term = jnp.where(jcol < i, -(slopes * ((i - jcol) * bs).astype(F32) + jnp.where(picked, 0.0, MOBA_PENALTY)),
                           0.0).astype(MXU_DTYPE)
    place = (_iota((n_kb, HEAD_DIM), 1) == 2 + _iota((n_kb, HEAD_DIM), 0)).astype(MXU_DTYPE)
    t_loc = _iota((bs, 1), 0).astype(F32)
    q_aug = []
    for h in range(N_HEADS):
        extra = lax.dot_general(block_term[:, h * bs:(h + 1) * bs], place, (((0,), (0,)), ((), ())),
                                preferred_element_type=F32)
        extra = extra + jnp.where(fcol == 0, MOBA_SLOPES[h], jnp.where(fcol == 1, -MOBA_SLOPES[h] * t_loc, 0.0))
        q_aug.append(jnp.concatenate([_head(qn, h) * (HEAD_DIM ** -0.5), extra], axis=1).astype(MXU_DTYPE))

    def scores(start):
        return jnp.concatenate(
            [lax.dot_general(k_s[pl.ds(start, bs), 2 * h * HEAD_DIM:(2 * h + 2) * HEAD_DIM], q_aug[h],
                             (((1,), (1,)), ((), ())), preferred_element_type=F32) for h in range(N_HEADS)], axis=1)

    def weighted_values(j, e):
        vt = vt_s[j]
        eb = e.astype(MXU_DTYPE)
        return jnp.concatenate(
            [jnp.dot(vt[h * HEAD_DIM:(h + 1) * HEAD_DIM, :], eb[:, h * bs:(h + 1) * bs], preferred_element_type=F32)
             for h in range(N_HEADS)], axis=1)

    sc = scores(own) + causal_s[...]
    m0 = jnp.max(sc, axis=0, keepdims=True)
    e = jnp.exp(sc - m0)
    l0 = jnp.sum(e, axis=0, keepdims=True)
    acc0 = weighted_values(i, e)

    def block_step(j, carry):
        m, l, acc = carry
        s2 = scores(pl.multiple_of(j * bs, bs))
        m_new = jnp.maximum(m, jnp.max(s2, axis=0, keepdims=True))
        alpha = jnp.exp(m - m_new)
        e2 = jnp.exp(s2 - m_new)
        l = alpha * l + jnp.sum(e2, axis=0, keepdims=True)
        return m_new, l, alpha * acc + weighted_values(j, e2)

    _, l, acc = lax.fori_loop(0, i, block_step, (m0, l0, acc0))
    o_t = jnp.concatenate([acc[:, h * bs:(h + 1) * bs] / l[:, h * bs:(h + 1) * bs] for h in range(N_HEADS)], axis=0)
    o_ref[0] = o_t.T


def _moba(p, q_norm, k_norm):
    b, t, cols = p.shape
    bs = MOBA_BLOCK
    assert t % bs == 0 and cols == MOBA_COLS and 2 + t // bs <= HEAD_DIM and bs <= 256
    row = lambda v: v.reshape(1, -1).astype(F32)
    const = lambda i, j: (0, 0)
    return pl.pallas_call(
        _moba_kernel,
        grid=(b, t // bs),
        in_specs=[pl.BlockSpec((1, bs, WIDTH), lambda i, j: (i, j, 0)),
                  pl.BlockSpec((1, t, WIDTH), lambda i, j: (i, 0, 1)),
                  pl.BlockSpec((1, t, WIDTH), lambda i, j: (i, 0, 2)),
                  pl.BlockSpec((1, HEAD_DIM), const), pl.BlockSpec((1, HEAD_DIM), const)],
        out_specs=pl.BlockSpec((1, bs, WIDTH), lambda i, j: (i, j, 0)),
        out_shape=jax.ShapeDtypeStruct((b, t, WIDTH), F32),
        scratch_shapes=[pltpu.VMEM((t, 2 * WIDTH), MXU_DTYPE),
                        pltpu.VMEM((t // bs, WIDTH, bs), MXU_DTYPE),
                        pltpu.VMEM((N_HEADS, t // bs, HEAD_DIM), F32),
                        pltpu.VMEM((bs, N_HEADS * bs), F32)],
        compiler_params=pltpu.CompilerParams(dimension_semantics=("parallel", "arbitrary"),
                                             vmem_limit_bytes=VMEM_LIMIT),
        name="moba",
    )(p, p, p, row(q_norm), row(k_norm))


def _pad_cols(w, n):
    return jnp.pad(w, ((0, 0), (0, n - w.shape[1])))


def _even_layer(x, b, t, norm_g, norm2_g, mlp_w1, mlp_w2, w_in, w_out, mu, w0, w2, a0, a2, g2, k_k, k_a, r_k,
                lnx_w, lnx_b, conv_w, a_log, dt_bias, gdn_norm_w):
    qkv_w = 3 * WIDTH
    w_rwkv = w_in[:, :RWKV_COLS]
    w_g = w_in[:, RWKV_COLS:]
    w_gdn = jnp.concatenate([w_g[:, :qkv_w], w_g[:, qkv_w + 2 * N_HEADS:],
                             _pad_cols(w_g[:, qkv_w:qkv_w + 2 * N_HEADS], LANES)], axis=1)
    p_rwkv, p_gdn = _norm_proj(x, norm_g, [w_rwkv.astype(MXU_DTYPE), w_gdn.astype(MXU_DTYPE)])
    o_a = _rwkv(p_rwkv.reshape(b, t, -1), mu, w0, w2, a0, a2, g2, k_k, k_a, r_k, lnx_w, lnx_b)
    o_b = _gdn(p_gdn.reshape(b, t, -1), conv_w, a_log, dt_bias, gdn_norm_w)
    return _out_mlp(x, o_a.reshape(b * t, -1), o_b.reshape(b * t, -1), w_out, norm2_g, mlp_w1, mlp_w2)


def _odd_layer(x, b, t, norm_g, norm2_g, mlp_w1, mlp_w2, w_in, w_out, dsa_q_norm, dsa_k_norm, idx_ln_w, idx_ln_b,
               moba_q_norm, moba_k_norm):
    kvw = 2 * DSA_KV_HEADS * HEAD_DIM
    sizes = (WIDTH, kvw, IDX_HEADS * IDX_DIM, IDX_DIM, IDX_HEADS, WIDTH, WIDTH, WIDTH)
    offs = [0]
    for s in sizes:
        offs.append(offs[-1] + s)
    dq, dkv, iq, ik, iw, mq, mk, mv = (w_in[:, offs[n]:offs[n + 1]] for n in range(len(sizes)))
    w_dsa = jnp.concatenate([dq, iq, dkv, _pad_cols(jnp.concatenate([ik, iw], axis=1), LANES)], axis=1)
    w_moba = jnp.concatenate([mq, mk, mv], axis=1)
    p_dsa, p_moba = _norm_proj(x, norm_g, [w_dsa.astype(MXU_DTYPE), w_moba.astype(MXU_DTYPE)])
    o_c = _dsa(p_dsa.reshape(b, t, -1), dsa_q_norm, dsa_k_norm, idx_ln_w, idx_ln_b)
    o_d = _moba(p_moba.reshape(b, t, -1), moba_q_norm, moba_k_norm)
    return _out_mlp(x, o_c.reshape(b * t, -1), o_d.reshape(b * t, -1), w_out, norm2_g, mlp_w1, mlp_w2)


def kernel(x, norm1_g, norm2_g, mlp_w1, mlp_w2, ev_w_in, ev_w_out, rwkv_mu, rwkv_w0, rwkv_w2, rwkv_a0, rwkv_a2, rwkv_g2, rwkv_k_k, rwkv_k_a, rwkv_r_k, rwkv_lnx_w, rwkv_lnx_b, gdn_conv_w, gdn_a_log, gdn_dt_bias, gdn_norm_w, od_w_in, od_w_out, dsa_q_norm, dsa_k_norm, idx_k_ln_w, idx_k_ln_b, moba_q_norm, moba_k_norm):
    b, t, d = x.shape
    depth = norm1_g.shape[0]
    h = x.reshape(b * t, d)
    for i in range(depth):
        j = i // 2
        if i % 2 == 0:
            h = _even_layer(h, b, t, norm1_g[i], norm2_g[i], mlp_w1[i], mlp_w2[i], ev_w_in[j], ev_w_out[j],
                            rwkv_mu[j], rwkv_w0[j], rwkv_w2[j], rwkv_a0[j], rwkv_a2[j], rwkv_g2[j], rwkv_k_k[j], rwkv_k_a[j], rwkv_r_k[j],
                            rwkv_lnx_w[j], rwkv_lnx_b[j], gdn_conv_w[j], gdn_a_log[j], gdn_dt_bias[j],
                            gdn_norm_w[j])
        else:
            h = _odd_layer(h, b, t, norm1_g[i], norm2_g[i], mlp_w1[i], mlp_w2[i], od_w_in[j], od_w_out[j],
                           dsa_q_norm[j], dsa_k_norm[j], idx_k_ln_w[j], idx_k_ln_b[j], moba_q_norm[j], moba_k_norm[j])
    return h.reshape(b, t, d)
```

```python
import functools
import math

import jax
import jax.numpy as jnp
from jax import lax
from jax.experimental import pallas as pl
from jax.experimental.pallas import tpu as pltpu

F32 = jnp.float32
MXU_DTYPE = jnp.bfloat16
HIGHEST = lax.Precision.HIGHEST

LANES = 128
VMEM_LIMIT = 56 * 1024 * 1024

HEAD_DIM = 64
N_HEADS = 8
WIDTH = N_HEADS * HEAD_DIM
RMS_EPS = 1e-6

RWKV_DECAY_LORA = 64
RWKV_A_LORA = 64
RWKV_GATE_LORA = 128
RWKV_COLS = 3 * WIDTH + RWKV_DECAY_LORA + RWKV_A_LORA + RWKV_GATE_LORA
RWKV_GN_EPS = 6.4e-4

GDN_CONV = 4
GDN_COLS = 3 * WIDTH + WIDTH + LANES
SCAN_CHUNK = 64
SCAN_BLOCK = 256

DSA_KV_HEADS = 2
DSA_GROUP = N_HEADS // DSA_KV_HEADS
IDX_HEADS = 8
IDX_DIM = 64
DSA_TOPK_MAX = 256
DSA_Q_BLOCK = 256
DSA_KEY_CHUNK = 512
DSA_COUNT_ROWS = 256
DSA_COLS = 2 * WIDTH + 2 * DSA_KV_HEADS * HEAD_DIM + LANES

MOBA_BLOCK = 256
MOBA_TOPK = 3
MOBA_COLS = 3 * WIDTH

PROJ_ROWS = 512
MLP_ROWS = 1024
MLP_FF_COLS = 1024

ALIBI_HEADS = 2 * N_HEADS
NEG_INF = float("-inf")
LOG2_E = math.log2(math.e)


def _alibi_slope(i):
    return 2.0 ** (-8.0 * (i + 1) / ALIBI_HEADS)


DSA_SLOPES = tuple(_alibi_slope(2 * h) for h in range(N_HEADS))
MOBA_SLOPES = tuple(_alibi_slope(2 * h + 1) for h in range(N_HEADS))


def _mm(a, b):
    return jnp.dot(a.astype(MXU_DTYPE), b.astype(MXU_DTYPE), preferred_element_type=F32)


def _mm32_nt(a, b):
    return lax.dot_general(a, b, (((1,), (1,)), ((), ())), preferred_element_type=F32, precision=HIGHEST)


def _iota(shape, dim):
    return lax.broadcasted_iota(jnp.int32, shape, dim)


def _sigmoid(x):
    return 1.0 / (1.0 + jnp.exp(-x))


def _silu(x):
    return x * _sigmoid(x)


def _softplus(x):
    return jnp.maximum(x, 0.0) + jnp.log1p(jnp.exp(-jnp.abs(x)))


def _head(x, h):
    return x[:, h * HEAD_DIM:(h + 1) * HEAD_DIM]


def _split_hi_lo(x):
    hi = x.astype(MXU_DTYPE)
    lo = (x - hi.astype(F32)).astype(MXU_DTYPE)
    return hi, lo


def _head_segments(width):
    return (_iota((width, width), 0) // HEAD_DIM == _iota((width, width), 1) // HEAD_DIM).astype(MXU_DTYPE)


def _head_sums(x, seg):
    hi, lo = _split_hi_lo(x)
    return jnp.dot(hi, seg, preferred_element_type=F32) + jnp.dot(lo, seg, preferred_element_type=F32)


def _tree_sum(parts):
    parts = list(parts)
    while len(parts) > 1:
        nxt = [parts[a] + parts[a + 1] for a in range(0, len(parts) - 1, 2)]
        if len(parts) % 2:
            nxt.append(parts[-1])
        parts = nxt
    return parts[0]


def _lane_tile(x, n):
    return jnp.concatenate([x] * n, axis=1)


def _norm_proj_kernel(x_ref, g_ref, *refs):
    n = len(refs) // 2
    x = x_ref[...]
    h = x * lax.rsqrt(jnp.mean(x * x, axis=-1, keepdims=True) + RMS_EPS) * g_ref[...]
    h = h.astype(MXU_DTYPE)
    for w_ref, o_ref in zip(refs[:n], refs[n:]):
        o_ref[...] = jnp.dot(h, w_ref[...], preferred_element_type=F32)


def _norm_proj(x, g, ws, tm=PROJ_ROWS):
    n, d = x.shape
    assert n % tm == 0
    const = lambda i: (0, 0)
    return pl.pallas_call(
        _norm_proj_kernel,
        grid=(n // tm,),
        in_specs=[pl.BlockSpec((tm, d), lambda i: (i, 0)), pl.BlockSpec((1, d), const)]
        + [pl.BlockSpec(w.shape, const) for w in ws],
        out_specs=[pl.BlockSpec((tm, w.shape[1]), lambda i: (i, 0)) for w in ws],
        out_shape=[jax.ShapeDtypeStruct((n, w.shape[1]), F32) for w in ws],
        compiler_params=pltpu.CompilerParams(dimension_semantics=("parallel",), vmem_limit_bytes=VMEM_LIMIT),
        name="norm_proj",
    )(x, g.reshape(1, d), *ws)


def _out_mlp_kernel(x_ref, a_ref, b_ref, wa_ref, wb_ref, g_ref, w1_ref, w2_ref, o_ref, h_ref, acc_ref):
    j = pl.program_id(1)

    @pl.when(j == 0)
    def _():
        x = x_ref[...] + _mm(a_ref[...], wa_ref[...]) + _mm(b_ref[...], wb_ref[...])
        h = x * lax.rsqrt(jnp.mean(x * x, axis=-1, keepdims=True) + RMS_EPS) * g_ref[...]
        h_ref[...] = h.astype(MXU_DTYPE)
        acc_ref[...] = x

    u = jnp.maximum(jnp.dot(h_ref[...], w1_ref[...], preferred_element_type=F32), 0.0)
    acc_ref[...] += jnp.dot((u * u).astype(MXU_DTYPE), w2_ref[...], preferred_element_type=F32)

    @pl.when(j == pl.num_programs(1) - 1)
    def _():
        o_ref[...] = acc_ref[...]


def _out_mlp(x, a, b, w_out, g, w1, w2, tm=MLP_ROWS, tf=MLP_FF_COLS):
    n, d = x.shape
    f = w1.shape[1]
    tm = min(tm, n)
    tf = min(tf, f)
    assert n % tm == 0 and f % tf == 0
    wa = w_out[:a.shape[1]].astype(MXU_DTYPE)
    wb = w_out[a.shape[1]:].astype(MXU_DTYPE)
    row = lambda i, j: (i, 0)
    const = lambda i, j: (0, 0)
    return pl.pallas_call(
        _out_mlp_kernel,
        grid=(n // tm, f // tf),
        in_specs=[pl.BlockSpec((tm, d), row), pl.BlockSpec((tm, a.shape[1]), row), pl.BlockSpec((tm, b.shape[1]), row),
                  pl.BlockSpec(wa.shape, const), pl.BlockSpec(wb.shape, const), pl.BlockSpec((1, d), const),
                  pl.BlockSpec((d, tf), lambda i, j: (0, j)), pl.BlockSpec((tf, d), lambda i, j: (j, 0))],
        out_specs=pl.BlockSpec((tm, d), row),
        out_shape=jax.ShapeDtypeStruct((n, d), F32),
        scratch_shapes=[pltpu.VMEM((tm, d), MXU_DTYPE), pltpu.VMEM((tm, d), F32)],
        compiler_params=pltpu.CompilerParams(dimension_semantics=("parallel", "arbitrary"),
                                             vmem_limit_bytes=VMEM_LIMIT),
        name="out_mlp",
    )(x, a, b, wa, wb, g.reshape(1, d), w1.astype(MXU_DTYPE), w2.astype(MXU_DTYPE))


def _bmm(a, b):
    return lax.dot_general(a.astype(MXU_DTYPE), b.astype(MXU_DTYPE), (((2,), (1,)), ((0,), (0,))),
                           preferred_element_type=F32)


def _bmm_nt(a, b):
    return lax.dot_general(a.astype(MXU_DTYPE), b.astype(MXU_DTYPE), (((2,), (2,)), ((0,), (0,))),
                           preferred_element_type=F32)


def _bmm_tn(a, b):
    return lax.dot_general(a.astype(MXU_DTYPE), b.astype(MXU_DTYPE), (((1,), (1,)), ((0,), (0,))),
                           preferred_element_type=F32)


def _stack_heads(x, nc, c):
    return jnp.stack([x[i * c:(i + 1) * c, h * HEAD_DIM:(h + 1) * HEAD_DIM] for i in range(nc) for h in range(N_HEADS)])


def _unstack_heads(y, nc):
    return jnp.concatenate([jnp.concatenate([y[i * N_HEADS + h] for h in range(N_HEADS)], axis=-1)
                            for i in range(nc)], axis=0)


def _batched_unit_lower_solve(x, y):
    c = x.shape[1]
    y = y + _bmm(x, y)
    p = 2
    while p < c:
        x = _bmm(x, x)
        y = y + _bmm(x, y)
        p *= 2
    return y


def _split3(x):
    hi = x.astype(MXU_DTYPE)
    r = x - hi.astype(F32)
    mid = r.astype(MXU_DTYPE)
    lo = (r - mid.astype(F32)).astype(MXU_DTYPE)
    return hi, mid, lo


def _chunk_cumsum_rows(x, c):
    tri = (_iota((c, c), 0) >= _iota((c, c), 1)).astype(MXU_DTYPE)
    parts = _split3(x)
    return jnp.concatenate(
        [_tree_sum([jnp.dot(tri, p[i * c:(i + 1) * c], preferred_element_type=F32) for p in parts])
         for i in range(x.shape[0] // c)], axis=0)


def _chunk_cumsum_cols(x, c):
    tri = (_iota((c, c), 0) <= _iota((c, c), 1)).astype(MXU_DTYPE)
    parts = _split3(x)
    return jnp.concatenate(
        [_tree_sum([jnp.dot(p[:, i * c:(i + 1) * c], tri, preferred_element_type=F32) for p in parts])
         for i in range(x.shape[1] // c)], axis=1)


def _run_chunks(st_ref, m, q, oq, o0, gamma, nc):
    s = st_ref[...]
    outs = []
    for i in range(nc):
        g = slice(i * N_HEADS, (i + 1) * N_HEADS)
        mo = _bmm(jnp.concatenate([m[g], oq[g]], axis=1), s)
        outs.append(mo[:, HEAD_DIM:] + o0[g])
        s = gamma[g] * s + mo[:, :HEAD_DIM] + q[g]
    st_ref[...] = s
    return jnp.concatenate(outs, axis=0)


def _rwkv_kernel(p_ref, mu_ref, w0_ref, w2_ref, a0_ref, a2_ref, g2_ref, kk_ref, ka_ref, rk_ref, lnw_ref, lnb_ref,
                 o_ref, prev_ref, st_ref):
    tb = p_ref.shape[1]
    c = SCAN_CHUNK
    nc = tb // c

    @pl.when(pl.program_id(1) == 0)
    def _():
        prev_ref[...] = jnp.zeros_like(prev_ref)
        st_ref[...] = jnp.zeros_like(st_ref)

    p = p_ref[0]
    shifted = jnp.where(_iota((tb, 1), 0) == 0, prev_ref[...], pltpu.roll(p, 1, 0))
    prev_ref[...] = p[tb - 1:tb, :]
    x = p + (shifted - p) * mu_ref[...]

    r = x[:, 0:WIDTH]
    k = x[:, WIDTH:2 * WIDTH]
    v = x[:, 2 * WIDTH:3 * WIDTH]
    off = 3 * WIDTH
    wd = x[:, off:off + RWKV_DECAY_LORA]
    ad = x[:, off + RWKV_DECAY_LORA:off + RWKV_DECAY_LORA + RWKV_A_LORA]
    gd = x[:, off + RWKV_DECAY_LORA + RWKV_A_LORA:]

    w_pre = w0_ref[...] + _mm(jnp.tanh(wd), w2_ref[...])
    log_w = -(_sigmoid(w_pre) * math.exp(-0.5))
    a = _sigmoid(a0_ref[...] + _mm(ad, a2_ref[...]))
    g = _mm(_sigmoid(gd), g2_ref[...])
    kk_all = k * kk_ref[...]
    k = k * (1.0 + (a - 1.0) * ka_ref[...])

    gam = _chunk_cumsum_rows(log_w, c)
    gam_last = jnp.concatenate([jnp.broadcast_to(gam[(i + 1) * c - 1:(i + 1) * c, :], (c, WIDTH)) for i in range(nc)],
                               axis=0)

    seg = _head_segments(WIDTH)
    kk = kk_all * lax.rsqrt(_head_sums(kk_all * kk_all, seg) + 1e-6)
    b = kk * a
    e_neg = jnp.exp(-gam)
    tail = jnp.exp(gam_last - gam)

    st = lambda y: _stack_heads(y, nc, c)
    v_h = st(v)
    a_t = st(-kk * jnp.exp(gam - log_w))
    r_t = st(r * jnp.exp(gam))
    b_t = st(b * e_neg)
    k_t = st(k * e_neg)
    e_last = st(jnp.exp(gam_last))[:, 0:1, :]

    eye = (_iota((c, c), 0) == _iota((c, c), 1))[None]

    inter = _bmm_nt(jnp.concatenate([a_t, r_t], axis=1), jnp.concatenate([b_t, k_t], axis=1))
    row2 = _iota((c, 2 * c), 0)
    col2 = _iota((c, 2 * c), 1)
    col2 = jnp.where(col2 >= c, col2 - c, col2)
    top = jnp.where((row2 > col2)[None], inter[:, :c], 0.0)
    bot = jnp.where((row2 >= col2)[None], inter[:, c:], 0.0)
    akv = _bmm(top[:, :, c:], v_h)
    wu = _batched_unit_lower_solve(top[:, :, :c], jnp.concatenate([a_t, akv], axis=-1))
    wu_v = jnp.concatenate([wu, jnp.concatenate([jnp.zeros_like(v_h), v_h], axis=-1)], axis=1)
    ro = _bmm(bot, wu_v)
    oq = r_t + ro[:, :, :HEAD_DIM]
    o0 = ro[:, :, HEAD_DIM:]
    mq = _bmm_tn(jnp.concatenate([st(b * tail), st(k * tail)], axis=1), wu_v)
    m = mq[:, :, :HEAD_DIM]
    q = mq[:, :, HEAD_DIM:]
    gamma = jnp.sum(jnp.where(eye, e_last, 0.0), axis=2, keepdims=True)

    o = _unstack_heads(_run_chunks(st_ref, m, q, oq, o0, gamma, nc), nc)

    oc = o - _head_sums(o, seg) * (1.0 / HEAD_DIM)
    on = oc * lax.rsqrt(_head_sums(oc * oc, seg) * (1.0 / HEAD_DIM) + RWKV_GN_EPS)
    bonus = _head_sums(r * k * rk_ref[...], seg) * v
    o_ref[0] = (on * lnw_ref[...] + lnb_ref[...] + bonus) * g


def _rwkv(p, mu, w0, w2, a0, a2, g2, k_k, k_a, r_k, lnx_w, lnx_b, tb=SCAN_BLOCK):
    b, t, cols = p.shape
    tb = min(tb, t)
    assert t % tb == 0 and tb % SCAN_CHUNK == 0 and cols == RWKV_COLS
    row = lambda v: v.reshape(1, -1).astype(F32)
    params = [row(mu), row(w0), w2.astype(MXU_DTYPE), row(a0), a2.astype(MXU_DTYPE), g2.astype(MXU_DTYPE),
              row(k_k), row(k_a), row(r_k), row(lnx_w), row(lnx_b)]
    const = lambda i, j: (0, 0)
    return pl.pallas_call(
        _rwkv_kernel,
        grid=(b, t // tb),
        in_specs=[pl.BlockSpec((1, tb, cols), lambda i, j: (i, j, 0))] + [pl.BlockSpec(q.shape, const) for q in params],
        out_specs=pl.BlockSpec((1, tb, WIDTH), lambda i, j: (i, j, 0)),
        out_shape=jax.ShapeDtypeStruct((b, t, WIDTH), F32),
        scratch_shapes=[pltpu.VMEM((1, cols), F32), pltpu.VMEM((N_HEADS, HEAD_DIM, HEAD_DIM), F32)],
        compiler_params=pltpu.CompilerParams(dimension_semantics=("parallel", "arbitrary"),
                                             vmem_limit_bytes=VMEM_LIMIT),
        name="rwkv7",
    )(p, *params)


def _gdn_kernel(p_ref, cw_ref, alog_r_ref, dt_r_ref, alog_c_ref, dt_c_ref, nw_ref, o_ref, xpad_ref, st_ref):
    tb = p_ref.shape[1]
    c = SCAN_CHUNK
    nc = tb // c
    qkv_w = 3 * WIDTH

    @pl.when(pl.program_id(1) == 0)
    def _():
        xpad_ref[:8, :] = jnp.zeros((8, xpad_ref.shape[1]), F32)
        st_ref[...] = jnp.zeros_like(st_ref)

    p = p_ref[0]
    xin = p[:, :qkv_w]
    z = p[:, qkv_w:qkv_w + WIDTH]
    b_in = p[:, qkv_w + WIDTH:qkv_w + WIDTH + N_HEADS]
    a_in = p[:, qkv_w + WIDTH + N_HEADS:qkv_w + WIDTH + 2 * N_HEADS]

    xpad_ref[8:, :] = xin
    conv = xin * cw_ref[GDN_CONV - 1:GDN_CONV, :]
    for s in range(1, GDN_CONV):
        conv = conv + xpad_ref[8 - s:8 - s + tb, :] * cw_ref[GDN_CONV - 1 - s:GDN_CONV - s, :]
    xpad_ref[:8, :] = xin[tb - 8:, :]
    qkv = _silu(conv)

    beta = _sigmoid(b_in)
    g_col = -jnp.exp(alog_r_ref[...]) * _softplus(a_in + dt_r_ref[...])
    gc_col = _chunk_cumsum_rows(g_col, c)
    eye_h = (_iota((N_HEADS, N_HEADS), 0) == _iota((N_HEADS, N_HEADS), 1)).astype(F32)
    a_row = _mm32_nt(eye_h, a_in)
    g_row = -jnp.exp(alog_c_ref[...]) * _softplus(a_row + dt_c_ref[...])
    gc_row = _chunk_cumsum_cols(g_row, c)

    def per_head_cols(y):
        return jnp.stack([y[i * c:(i + 1) * c, h:h + 1] for i in range(nc) for h in range(N_HEADS)])

    def over_head_lanes(y):
        expand = (_iota((N_HEADS, WIDTH), 1) // HEAD_DIM == _iota((N_HEADS, WIDTH), 0)).astype(MXU_DTYPE)
        return _tree_sum([jnp.dot(part, expand, preferred_element_type=F32) for part in _split3(y)])

    gc = per_head_cols(gc_col)
    g_rows = jnp.stack([gc_row[h:h + 1, i * c:(i + 1) * c] for i in range(nc) for h in range(N_HEADS)])
    gc_d = over_head_lanes(gc_col)
    beta_d = over_head_lanes(beta)
    g_last_d = jnp.concatenate([jnp.broadcast_to(gc_d[(i + 1) * c - 1:(i + 1) * c, :], (c, WIDTH)) for i in range(nc)],
                               axis=0)

    seg = _head_segments(WIDTH)
    q = qkv[:, :WIDTH]
    k = qkv[:, WIDTH:2 * WIDTH]
    q = q * lax.rsqrt(_head_sums(q * q, seg) + 1e-6) * (HEAD_DIM ** -0.5)
    k = k * lax.rsqrt(_head_sums(k * k, seg) + 1e-6)
    kb_d = k * beta_d
    e_g = jnp.exp(gc_d)

    st = lambda y: _stack_heads(y, nc, c)
    q_h, k_h, kb = st(q), st(k), st(kb_d)

    ri = _iota((c, c), 0)
    ci = _iota((c, c), 1)
    incl = (ri >= ci)[None]
    strict = (ri > ci)[None]
    decay = jnp.exp(jnp.where(incl, gc - g_rows, NEG_INF))
    mm = _bmm_nt(jnp.concatenate([kb, q_h], axis=1), k_h)
    a_mat = jnp.where(strict, mm[:, :c] * decay, 0.0)
    qk = jnp.where(incl, mm[:, c:] * decay, 0.0)
    rhs = jnp.concatenate([st(qkv[:, 2 * WIDTH:] * beta_d), st(kb_d * e_g)], axis=-1)
    uw = _batched_unit_lower_solve(-a_mat, rhs)
    qkuw = _bmm(qk, uw)
    o0 = qkuw[:, :, :HEAD_DIM]
    oq = st(q * e_g) - qkuw[:, :, HEAD_DIM:]
    kuw = _bmm_tn(st(k * jnp.exp(g_last_d - gc_d)), uw)
    gamma = jnp.exp(gc[:, c - 1:c, :])
    o = _run_chunks(st_ref, -kuw[:, :, HEAD_DIM:], kuw[:, :, :HEAD_DIM], oq, o0, gamma, nc)

    o = _unstack_heads(o, nc)
    o = o * lax.rsqrt(_head_sums(o * o, seg) * (1.0 / HEAD_DIM) + RMS_EPS) * _lane_tile(nw_ref[...], N_HEADS)
    o_ref[0] = o * _silu(z)


def _gdn(p, conv_w, a_log, dt_bias, norm_w, tb=SCAN_BLOCK):
    b, t, cols = p.shape
    tb = min(tb, t)
    assert t % tb == 0 and tb % SCAN_CHUNK == 0 and cols == GDN_COLS
    params = [conv_w.astype(F32), a_log.reshape(1, -1), dt_bias.reshape(1, -1), a_log.reshape(-1, 1),
              dt_bias.reshape(-1, 1), norm_w.reshape(1, -1)]
    const = lambda i, j: (0, 0)
    return pl.pallas_call(
        _gdn_kernel,
        grid=(b, t // tb),
        in_specs=[pl.BlockSpec((1, tb, cols), lambda i, j: (i, j, 0))] + [pl.BlockSpec(q.shape, const) for q in params],
        out_specs=pl.BlockSpec((1, tb, WIDTH), lambda i, j: (i, j, 0)),
        out_shape=jax.ShapeDtypeStruct((b, t, WIDTH), F32),
        scratch_shapes=[pltpu.VMEM((8 + tb, 3 * WIDTH), F32), pltpu.VMEM((N_HEADS, HEAD_DIM, HEAD_DIM), F32)],
        compiler_params=pltpu.CompilerParams(dimension_semantics=("parallel", "arbitrary"),
                                             vmem_limit_bytes=VMEM_LIMIT),
        name="gdn",
    )(p, *params)


def _key_to_float(key):
    return pltpu.bitcast(jnp.where(key >= 0, key, key ^ jnp.int32(0x7FFFFFFF)), F32)


NEG_INF_KEY = -2139095041
COARSE_DTYPE = jnp.bfloat16
SOFTMAX_FLOOR = -1e30


def _dsa_kernel(dq_ref, iq_ref, kv_ref, ikw_ref, iwq_ref, qn_ref, kn_ref, lnw_ref, lnb_ref, o_ref,
                k_s, vt_s, ik_s, score_s, coarse_s, distm_s, *, top_k):
    qb = dq_ref.shape[1]
    t_len = kv_ref.shape[1]
    kc = DSA_KEY_CHUNK
    n_chunks = t_len // kc
    kvw = DSA_KV_HEADS * HEAD_DIM
    i = pl.program_id(1)
    n_live = (i * qb + qb + kc - 1) // kc

    @pl.when(i == 0)
    def _():
        kv = kv_ref[0]
        kraw = kv[:, :kvw]
        kn = kraw * lax.rsqrt(_head_sums(kraw * kraw, _head_segments(kvw)) * (1.0 / HEAD_DIM) + RMS_EPS)
        k_s[...] = (kn * _lane_tile(kn_ref[...], DSA_KV_HEADS)).astype(MXU_DTYPE)
        for c in range(n_chunks):
            vt_s[c] = kv[c * kc:(c + 1) * kc, kvw:].T.astype(MXU_DTYPE)
        ik = ikw_ref[0][:, :IDX_DIM]
        ikc = ik - jnp.mean(ik, axis=-1, keepdims=True)
        ik = ikc * lax.rsqrt(jnp.mean(ikc * ikc, axis=-1, keepdims=True) + 1e-6) * lnw_ref[...] + lnb_ref[...]
        hi, lo = _split_hi_lo(ik)
        ik_s[...] = jnp.concatenate([hi, hi, lo], axis=-1)
        score_s[...] = jnp.full(score_s.shape, NEG_INF, F32)

    t_row = i * qb + _iota((1, qb), 1)
    s_loc = _iota((kc, 1), 0)

    eye_h = (_iota((IDX_HEADS, IDX_HEADS), 0) == _iota((IDX_HEADS, IDX_HEADS), 1)).astype(F32)
    iw_t = _mm32_nt(eye_h, iwq_ref[0][:, IDX_DIM:IDX_DIM + IDX_HEADS]) * (IDX_HEADS ** -0.5 * IDX_DIM ** -0.5)
    iw_wide = jnp.concatenate([iw_t[h:h + 1, :] for h in range(IDX_HEADS)], axis=1)
    iq = iq_ref[0]
    iq_cat = []
    for h in range(IDX_HEADS):
        hi, lo = _split_hi_lo(_head(iq, h))
        iq_cat.append(jnp.concatenate([hi, lo, hi], axis=-1))
    iq_all = jnp.concatenate(iq_cat, axis=0)

    def score_chunk(c, carry):
        rows = pl.ds(pl.multiple_of(c * kc, kc), kc)
        dots = lax.dot_general(ik_s[rows, :], iq_all, (((1,), (1,)), ((), ())), preferred_element_type=F32)
        w = iw_wide * jnp.maximum(dots, 0.0)
        sc = _tree_sum([w[:, h * qb:(h + 1) * qb] for h in range(IDX_HEADS)])
        sc = jnp.where(c * kc + s_loc <= t_row, sc, NEG_INF)
        score_s[rows, :] = sc
        coarse_s[rows, :] = sc.astype(COARSE_DTYPE)
        return carry

    lax.fori_loop(0, n_live, score_chunk, 0)

    kf = float(top_k)
    grp = DSA_COUNT_ROWS
    n_grp = (i * qb + qb + grp - 1) // grp

    def count(pred):
        def body(g, acc):
            blk = score_s[pl.ds(pl.multiple_of(g * grp, grp), grp), :]
            ones = jnp.where(pred(blk), 1.0, 0.0)
            return acc + _tree_sum([ones[r * 8:(r + 1) * 8] for r in range(grp // 8)])
        acc = lax.fori_loop(0, n_grp, body, jnp.zeros((8, qb), F32))
        return jnp.sum(acc, axis=0, keepdims=True)

    def count_coarse(cand):
        def body(g, acc):
            blk = coarse_s[pl.ds(pl.multiple_of(g * grp, grp), grp), :]
            ones = jnp.where(blk >= cand, jnp.ones((), COARSE_DTYPE), jnp.zeros((), COARSE_DTYPE))
            return acc + _tree_sum([ones[r * 16:(r + 1) * 16] for r in range(grp // 16)]).astype(F32)
        acc = lax.fori_loop(0, n_grp, body, jnp.zeros((16, qb), F32))
        return jnp.sum(acc, axis=0, keepdims=True)

    def coarse_key(base):
        return jnp.where(base >= 0, base, base | jnp.int32(0xFFFF))

    int_min = jnp.int32(-2 ** 31)
    base = jnp.where(count_coarse(jnp.zeros((1, qb), COARSE_DTYPE)) >= kf, jnp.int32(0), int_min)

    def coarse_step(b, base):
        cand = base | (jnp.int32(1) << (30 - b))
        cand_f = _key_to_float(coarse_key(cand)).astype(COARSE_DTYPE)
        return jnp.where(count_coarse(cand_f) >= kf, cand, base)

    base = lax.fori_loop(0, 15, coarse_step, base)
    lo_key = coarse_key(base) - jnp.int32(2 ** 15)

    def fine_step(b, off):
        cand_off = off | (jnp.int32(1) << (16 - b))
        cand_f = _key_to_float(lo_key + cand_off)
        return jnp.where(count(lambda blk: blk >= cand_f) >= kf, cand_off, off)

    tau = lo_key + lax.fori_loop(0, 17, fine_step, jnp.zeros((1, qb), jnp.int32))
    tau_f = _key_to_float(jnp.maximum(tau, jnp.int32(NEG_INF_KEY)))
    need = kf - count(lambda blk: blk > tau_f)

    n_ge = count(lambda blk: blk >= tau_f)
    no_partial_tie = jnp.min(jnp.where((n_ge <= kf) | (tau_f == NEG_INF), 1.0, 0.0)) > 0.5

    def masked_distance(c, sel_fn):
        rows = pl.ds(pl.multiple_of(c * kc, kc), kc)
        s_pos = c * kc + s_loc
        sel = sel_fn(score_s[rows, :]) & (s_pos <= t_row)
        distm_s[rows, :] = jnp.where(sel, (t_row - s_pos).astype(F32), jnp.inf)

    @pl.when(no_partial_tie)
    def _():
        def select_chunk(c, carry):
            masked_distance(c, lambda blk: blk >= tau_f)
            return carry
        lax.fori_loop(0, n_live, select_chunk, 0)

    @pl.when(jnp.logical_not(no_partial_tie))
    def _():
        lower = (_iota((kc, kc), 0) >= _iota((kc, kc), 1)).astype(MXU_DTYPE)

        def select_chunk(c, run):
            def sel_fn(blk):
                eq = blk == tau_f
                pref = run + jnp.dot(lower, jnp.where(eq, 1.0, 0.0).astype(MXU_DTYPE), preferred_element_type=F32)
                return (blk > tau_f) | (eq & (pref <= need))
            masked_distance(c, sel_fn)
            rows = pl.ds(pl.multiple_of(c * kc, kc), kc)
            return run + jnp.sum(jnp.where(score_s[rows, :] == tau_f, 1.0, 0.0), axis=0, keepdims=True)
        lax.fori_loop(0, n_live, select_chunk, jnp.zeros((1, qb), F32))

    dq = dq_ref[0]
    dq = dq * lax.rsqrt(_head_sums(dq * dq, _head_segments(WIDTH)) * (1.0 / HEAD_DIM) + RMS_EPS)
    dq = (dq * _lane_tile(qn_ref[...] * (HEAD_DIM ** -0.5 * LOG2_E), N_HEADS)).astype(MXU_DTYPE)
    qs = [_head(dq, h) for h in range(N_HEADS)]
    q_grp = [jnp.concatenate(qs[g * DSA_GROUP:(g + 1) * DSA_GROUP], axis=0) for g in range(DSA_KV_HEADS)]
    slopes = jnp.concatenate([jnp.full((1, qb), DSA_SLOPES[h] * LOG2_E, F32) for h in range(N_HEADS)], axis=1)
    gw = DSA_GROUP * qb

    def attend_chunk(c, carry):
        m, l, acc = carry
        rows = pl.ds(pl.multiple_of(c * kc, kc), kc)
        kch = k_s[rows, :]
        vt = vt_s[c]
        sc = jnp.concatenate(
            [lax.dot_general(kch[:, g * HEAD_DIM:(g + 1) * HEAD_DIM], q_grp[g], (((1,), (1,)), ((), ())),
                             preferred_element_type=F32) for g in range(DSA_KV_HEADS)], axis=1)
        sc = sc - slopes * _lane_tile(distm_s[rows, :], N_HEADS)
        m_new = jnp.maximum(m, jnp.max(sc, axis=0, keepdims=True))
        alpha = jnp.exp2(m - m_new)
        e = jnp.exp2(sc - m_new)
        l = alpha * l + jnp.sum(e, axis=0, keepdims=True)
        eb = e.astype(MXU_DTYPE)
        pv = jnp.concatenate(
            [jnp.dot(vt[g * HEAD_DIM:(g + 1) * HEAD_DIM, :], eb[:, g * gw:(g + 1) * gw], preferred_element_type=F32)
             for g in range(DSA_KV_HEADS)], axis=1)
        return m_new, l, alpha * acc + pv

    init = (jnp.full((1, N_HEADS * qb), SOFTMAX_FLOOR, F32), jnp.zeros((1, N_HEADS * qb), F32),
            jnp.zeros((HEAD_DIM, N_HEADS * qb), F32))
    _, l, acc = lax.fori_loop(0, n_live, attend_chunk, init)
    o_t = jnp.concatenate([acc[:, h * qb:(h + 1) * qb] / l[:, h * qb:(h + 1) * qb] for h in range(N_HEADS)], axis=0)
    o_ref[0] = o_t.T


def _dsa(p, q_norm, k_norm, ln_w, ln_b):
    b, t, cols = p.shape
    qb = DSA_Q_BLOCK
    kc = DSA_KEY_CHUNK
    assert t % qb == 0 and cols == DSA_COLS and t % DSA_COUNT_ROWS == 0 and t % kc == 0 and kc % DSA_COUNT_ROWS == 0
    top_k = min(DSA_TOPK_MAX, t // 4)
    kvw = 2 * DSA_KV_HEADS * HEAD_DIM
    row = lambda v: v.reshape(1, -1).astype(F32)
    params = [row(q_norm), row(k_norm), row(ln_w), row(ln_b)]
    const = lambda i, j: (0, 0)
    return pl.pallas_call(
        functools.partial(_dsa_kernel, top_k=top_k),
        grid=(b, t // qb),
        in_specs=[pl.BlockSpec((1, qb, WIDTH), lambda i, j: (i, j, 0)),
                  pl.BlockSpec((1, qb, WIDTH), lambda i, j: (i, j, 1)),
                  pl.BlockSpec((1, t, kvw), lambda i, j: (i, 0, 2 * WIDTH // kvw)),
                  pl.BlockSpec((1, t, LANES), lambda i, j: (i, 0, (2 * WIDTH + kvw) // LANES)),
                  pl.BlockSpec((1, qb, LANES), lambda i, j: (i, j, (2 * WIDTH + kvw) // LANES))]
        + [pl.BlockSpec(q.shape, const) for q in params],
        out_specs=pl.BlockSpec((1, qb, WIDTH), lambda i, j: (i, j, 0)),
        out_shape=jax.ShapeDtypeStruct((b, t, WIDTH), F32),
        scratch_shapes=[pltpu.VMEM((t, DSA_KV_HEADS * HEAD_DIM), MXU_DTYPE),
                        pltpu.VMEM((t // kc, DSA_KV_HEADS * HEAD_DIM, kc), MXU_DTYPE),
                        pltpu.VMEM((t, 3 * IDX_DIM), MXU_DTYPE),
                        pltpu.VMEM((t, qb), F32),
                        pltpu.VMEM((t, qb), COARSE_DTYPE),
                        pltpu.VMEM((t, qb), F32)],
        compiler_params=pltpu.CompilerParams(dimension_semantics=("parallel", "arbitrary"),
                                             vmem_limit_bytes=VMEM_LIMIT),
        name="dsa",
    )(p, p, p, p, p, *params)


MOBA_PENALTY = 1e30


def _moba_kernel(q_ref, k_ref, v_ref, qn_ref, kn_ref, o_ref, k_s, vt_s, km_s, causal_s):
    bs = q_ref.shape[1]
    t_len = k_ref.shape[1]
    n_kb = t_len // bs
    i = pl.program_id(1)
    fcol = _iota((1, HEAD_DIM), 1)

    @pl.when(i == 0)
    def _():
        kraw = k_ref[0]
        kn = kraw * lax.rsqrt(_head_sums(kraw * kraw, _head_segments(WIDTH)) * (1.0 / HEAD_DIM) + RMS_EPS)
        kn = kn * _lane_tile(kn_ref[...], N_HEADS)
        s_pos = _iota((t_len, 1), 0)
        extra = jnp.where(fcol == 0, (s_pos % bs).astype(F32),
                          jnp.where((fcol == 1) | (fcol == 2 + s_pos // bs), 1.0, 0.0)).astype(MXU_DTYPE)
        for h in range(N_HEADS):
            k_s[:, 2 * h * HEAD_DIM:(2 * h + 1) * HEAD_DIM] = _head(kn, h).astype(MXU_DTYPE)
            k_s[:, (2 * h + 1) * HEAD_DIM:(2 * h + 2) * HEAD_DIM] = extra
        means = jnp.concatenate([jnp.mean(kn[j * bs:(j + 1) * bs], axis=0, keepdims=True) for j in range(n_kb)], axis=0)
        for h in range(N_HEADS):
            km_s[h] = _head(means, h)
        v = v_ref[0]
        for j in range(n_kb):
            vt_s[j] = v[j * bs:(j + 1) * bs, :].T.astype(MXU_DTYPE)
        above = _iota((bs, 1), 0) > _iota((1, bs), 1)
        causal_s[...] = _lane_tile(jnp.where(above, NEG_INF, 0.0), N_HEADS)

    qraw = q_ref[0]
    jcol = _iota((n_kb, 1), 0)
    own = pl.multiple_of(i * bs, bs)
    qn = qraw * lax.rsqrt(_head_sums(qraw * qraw, _head_segments(WIDTH)) * (1.0 / HEAD_DIM) + RMS_EPS)
    qn = qn * _lane_tile(qn_ref[...], N_HEADS)
    gates = [_mm32_nt(km_s[h], _head(qn, h)) for h in range(N_HEADS)]
    gate = jnp.where(jcol < i, jnp.concatenate(gates, axis=1), NEG_INF)
    rank = jnp.zeros(gate.shape, F32)
    for j2 in range(n_kb):
        other = gate[j2:j2 + 1, :]
        beats = (other > gate) | ((other == gate) & (j2 < jcol))
        rank = rank + jnp.where(beats, 1.0, 0.0)
    picked = (rank < float(MOBA_TOPK)) & (jcol < i)
    slopes = jnp.concatenate([jnp.full((1, bs), MOBA_SLOPES[h], F32) for h in range(N_HEADS)], axis=1)
    block_term = jnp.where(jcol < i, -(slopes * ((i - jcol) * bs).astype(F32) + jnp.where(picked, 0.0, MOBA_PENALTY)),
                           0.0).astype(MXU_DTYPE)
    place = (_iota((n_kb, HEAD_DIM), 1) == 2 + _iota((n_kb, HEAD_DIM), 0)).astype(MXU_DTYPE)
    t_loc = _iota((bs, 1), 0).astype(F32)
    q_aug = []
    for h in range(N_HEADS):
        extra = lax.dot_general(block_term[:, h * bs:(h + 1) * bs], place, (((0,), (0,)), ((), ())),
                                preferred_element_type=F32)
        extra = extra + jnp.where(fcol == 0, MOBA_SLOPES[h], jnp.where(fcol == 1, -MOBA_SLOPES[h] * t_loc, 0.0))
        q_aug.append(jnp.concatenate([_head(qn, h) * (HEAD_DIM ** -0.5), extra], axis=1).astype(MXU_DTYPE))

    def scores(start):
        return jnp.concatenate(
            [lax.dot_general(k_s[pl.ds(start, bs), 2 * h * HEAD_DIM:(2 * h + 2) * HEAD_DIM], q_aug[h],
                             (((1,), (1,)), ((), ())), preferred_element_type=F32) for h in range(N_HEADS)], axis=1)

    def weighted_values(j, e):
        vt = vt_s[j]
        eb = e.astype(MXU_DTYPE)
        return jnp.concatenate(
            [jnp.dot(vt[h * HEAD_DIM:(h + 1) * HEAD_DIM, :], eb[:, h * bs:(h + 1) * bs], preferred_element_type=F32)
             for h in range(N_HEADS)], axis=1)

    sc = scores(own) + causal_s[...]
    m0 = jnp.max(sc, axis=0, keepdims=True)
    e = jnp.exp(sc - m0)
    l0 = jnp.sum(e, axis=0, keepdims=True)
    acc0 = weighted_values(i, e)

    def block_step(j, carry):
        m, l, acc = carry
        s2 = scores(pl.multiple_of(j * bs, bs))
        m_new = jnp.maximum(m, jnp.max(s2, axis=0, keepdims=True))
        alpha = jnp.exp(m - m_new)
        e2 = jnp.exp(s2 - m_new)
        l = alpha * l + jnp.sum(e2, axis=0, keepdims=True)
        return m_new, l, alpha * acc + weighted_values(j, e2)

    _, l, acc = lax.fori_loop(0, i, block_step, (m0, l0, acc0))
    o_t = jnp.concatenate([acc[:, h * bs:(h + 1) * bs] / l[:, h * bs:(h + 1) * bs] for h in range(N_HEADS)], axis=0)
    o_ref[0] = o_t.T


def _moba(p, q_norm, k_norm):
    b, t, cols = p.shape
    bs = MOBA_BLOCK
    assert t % bs == 0 and cols == MOBA_COLS and 2 + t // bs <= HEAD_DIM and bs <= 256
    row = lambda v: v.reshape(1, -1).astype(F32)
    const = lambda i, j: (0, 0)
    return pl.pallas_call(
        _moba_kernel,
        grid=(b, t // bs),
        in_specs=[pl.BlockSpec((1, bs, WIDTH), lambda i, j: (i, j, 0)),
                  pl.BlockSpec((1, t, WIDTH), lambda i, j: (i, 0, 1)),
                  pl.BlockSpec((1, t, WIDTH), lambda i, j: (i, 0, 2)),
                  pl.BlockSpec((1, HEAD_DIM), const), pl.BlockSpec((1, HEAD_DIM), const)],
        out_specs=pl.BlockSpec((1, bs, WIDTH), lambda i, j: (i, j, 0)),
        out_shape=jax.ShapeDtypeStruct((b, t, WIDTH), F32),
        scratch_shapes=[pltpu.VMEM((t, 2 * WIDTH), MXU_DTYPE),
                        pltpu.VMEM((t // bs, WIDTH, bs), MXU_DTYPE),
                        pltpu.VMEM((N_HEADS, t // bs, HEAD_DIM), F32),
                        pltpu.VMEM((bs, N_HEADS * bs), F32)],
        compiler_params=pltpu.CompilerParams(dimension_semantics=("parallel", "arbitrary"),
                                             vmem_limit_bytes=VMEM_LIMIT),
        name="moba",
    )(p, p, p, row(q_norm), row(k_norm))


def _pad_cols(w, n):
    return jnp.pad(w, ((0, 0), (0, n - w.shape[1])))


def _even_layer(x, b, t, norm_g, norm2_g, mlp_w1, mlp_w2, w_in, w_out, mu, w0, w2, a0, a2, g2, k_k, k_a, r_k,
                lnx_w, lnx_b, conv_w, a_log, dt_bias, gdn_norm_w):
    qkv_w = 3 * WIDTH
    w_rwkv = w_in[:, :RWKV_COLS]
    w_g = w_in[:, RWKV_COLS:]
    w_gdn = jnp.concatenate([w_g[:, :qkv_w], w_g[:, qkv_w + 2 * N_HEADS:],
                             _pad_cols(w_g[:, qkv_w:qkv_w + 2 * N_HEADS], LANES)], axis=1)
    p_rwkv, p_gdn = _norm_proj(x, norm_g, [w_rwkv.astype(MXU_DTYPE), w_gdn.astype(MXU_DTYPE)])
    o_a = _rwkv(p_rwkv.reshape(b, t, -1), mu, w0, w2, a0, a2, g2, k_k, k_a, r_k, lnx_w, lnx_b)
    o_b = _gdn(p_gdn.reshape(b, t, -1), conv_w, a_log, dt_bias, gdn_norm_w)
    return _out_mlp(x, o_a.reshape(b * t, -1), o_b.reshape(b * t, -1), w_out, norm2_g, mlp_w1, mlp_w2)


def _odd_layer(x, b, t, norm_g, norm2_g, mlp_w1, mlp_w2, w_in, w_out, dsa_q_norm, dsa_k_norm, idx_ln_w, idx_ln_b,
               moba_q_norm, moba_k_norm):
    kvw = 2 * DSA_KV_HEADS * HEAD_DIM
    sizes = (WIDTH, kvw, IDX_HEADS * IDX_DIM, IDX_DIM, IDX_HEADS, WIDTH, WIDTH, WIDTH)
    offs = [0]
    for s in sizes:
        offs.append(offs[-1] + s)
    dq, dkv, iq, ik, iw, mq, mk, mv = (w_in[:, offs[n]:offs[n + 1]] for n in range(len(sizes)))
    w_dsa = jnp.concatenate([dq, iq, dkv, _pad_cols(jnp.concatenate([ik, iw], axis=1), LANES)], axis=1)
    w_moba = jnp.concatenate([mq, mk, mv], axis=1)
    p_dsa, p_moba = _norm_proj(x, norm_g, [w_dsa.astype(MXU_DTYPE), w_moba.astype(MXU_DTYPE)])
    o_c = _dsa(p_dsa.reshape(b, t, -1), dsa_q_norm, dsa_k_norm, idx_ln_w, idx_ln_b)
    o_d = _moba(p_moba.reshape(b, t, -1), moba_q_norm, moba_k_norm)
    return _out_mlp(x, o_c.reshape(b * t, -1), o_d.reshape(b * t, -1), w_out, norm2_g, mlp_w1, mlp_w2)


def kernel(x, norm1_g, norm2_g, mlp_w1, mlp_w2, ev_w_in, ev_w_out, rwkv_mu, rwkv_w0, rwkv_w2, rwkv_a0, rwkv_a2, rwkv_g2, rwkv_k_k, rwkv_k_a, rwkv_r_k, rwkv_lnx_w, rwkv_lnx_b, gdn_conv_w, gdn_a_log, gdn_dt_bias, gdn_norm_w, od_w_in, od_w_out, dsa_q_norm, dsa_k_norm, idx_k_ln_w, idx_k_ln_b, moba_q_norm, moba_k_norm):
    b, t, d = x.shape
    depth = norm1_g.shape[0]
    h = x.reshape(b * t, d)
    for i in range(depth):
        j = i // 2
        if i % 2 == 0:
            h = _even_layer(h, b, t, norm1_g[i], norm2_g[i], mlp_w1[i], mlp_w2[i], ev_w_in[j], ev_w_out[j],
                            rwkv_mu[j], rwkv_w0[j], rwkv_w2[j], rwkv_a0[j], rwkv_a2[j], rwkv_g2[j], rwkv_k_k[j], rwkv_k_a[j], rwkv_r_k[j],
                            rwkv_lnx_w[j], rwkv_lnx_b[j], gdn_conv_w[j], gdn_a_log[j], gdn_dt_bias[j],
                            gdn_norm_w[j])
        else:
            h = _odd_layer(h, b, t, norm1_g[i], norm2_g[i], mlp_w1[i], mlp_w2[i], od_w_in[j], od_w_out[j],
                           dsa_q_norm[j], dsa_k_norm[j], idx_k_ln_w[j], idx_k_ln_b[j], moba_q_norm[j], moba_k_norm[j])
    return h.reshape(b, t, d)
```

```python
import functools
import math

import jax
import jax.numpy as jnp
from jax import lax
from jax.experimental import pallas as pl
from jax.experimental.pallas import tpu as pltpu

F32 = jnp.float32
MXU_DTYPE = jnp.bfloat16
HIGHEST = lax.Precision.HIGHEST

LANES = 128
VMEM_LIMIT = 56 * 1024 * 1024

HEAD_DIM = 64
N_HEADS = 8
WIDTH = N_HEADS * HEAD_DIM
RMS_EPS = 1e-6

RWKV_DECAY_LORA = 64
RWKV_A_LORA = 64
RWKV_GATE_LORA = 128
RWKV_COLS = 3 * WIDTH + RWKV_DECAY_LORA + RWKV_A_LORA + RWKV_GATE_LORA
RWKV_GN_EPS = 6.4e-4

GDN_CONV = 4
GDN_COLS = 3 * WIDTH + WIDTH + LANES
SCAN_CHUNK = 64
SCAN_BLOCK = 256

DSA_KV_HEADS = 2
DSA_GROUP = N_HEADS // DSA_KV_HEADS
IDX_HEADS = 8
IDX_DIM = 64
DSA_TOPK_MAX = 256
DSA_Q_BLOCK = 256
DSA_KEY_CHUNK = 512
DSA_COUNT_ROWS = 256
DSA_COLS = 2 * WIDTH + 2 * DSA_KV_HEADS * HEAD_DIM + LANES

MOBA_BLOCK = 256
MOBA_TOPK = 3
MOBA_COLS = 3 * WIDTH

PROJ_ROWS = 512
MLP_ROWS = 1024
MLP_FF_COLS = 1024

ALIBI_HEADS = 2 * N_HEADS
NEG_INF = float("-inf")
LOG2_E = math.log2(math.e)


def _alibi_slope(i):
    return 2.0 ** (-8.0 * (i + 1) / ALIBI_HEADS)


DSA_SLOPES = tuple(_alibi_slope(2 * h) for h in range(N_HEADS))
MOBA_SLOPES = tuple(_alibi_slope(2 * h + 1) for h in range(N_HEADS))


def _mm(a, b):
    return jnp.dot(a.astype(MXU_DTYPE), b.astype(MXU_DTYPE), preferred_element_type=F32)


def _mm32_nt(a, b):
    return lax.dot_general(a, b, (((1,), (1,)), ((), ())), preferred_element_type=F32, precision=HIGHEST)


def _iota(shape, dim):
    return lax.broadcasted_iota(jnp.int32, shape, dim)


def _sigmoid(x):
    return 1.0 / (1.0 + jnp.exp(-x))


def _silu(x):
    return x * _sigmoid(x)


def _softplus(x):
    return jnp.maximum(x, 0.0) + jnp.log1p(jnp.exp(-jnp.abs(x)))


def _head(x, h):
    return x[:, h * HEAD_DIM:(h + 1) * HEAD_DIM]


def _split_hi_lo(x):
    hi = x.astype(MXU_DTYPE)
    lo = (x - hi.astype(F32)).astype(MXU_DTYPE)
    return hi, lo


def _head_segments(width):
    return (_iota((width, width), 0) // HEAD_DIM == _iota((width, width), 1) // HEAD_DIM).astype(MXU_DTYPE)


def _head_sums(x, seg):
    hi, lo = _split_hi_lo(x)
    return jnp.dot(hi, seg, preferred_element_type=F32) + jnp.dot(lo, seg, preferred_element_type=F32)


def _tree_sum(parts):
    parts = list(parts)
    while len(parts) > 1:
        nxt = [parts[a] + parts[a + 1] for a in range(0, len(parts) - 1, 2)]
        if len(parts) % 2:
            nxt.append(parts[-1])
        parts = nxt
    return parts[0]


def _lane_tile(x, n):
    return jnp.concatenate([x] * n, axis=1)


def _norm_proj_kernel(x_ref, g_ref, *refs):
    n = len(refs) // 2
    x = x_ref[...]
    h = x * lax.rsqrt(jnp.mean(x * x, axis=-1, keepdims=True) + RMS_EPS) * g_ref[...]
    h = h.astype(MXU_DTYPE)
    for w_ref, o_ref in zip(refs[:n], refs[n:]):
        o_ref[...] = jnp.dot(h, w_ref[...], preferred_element_type=F32)


def _norm_proj(x, g, ws, tm=PROJ_ROWS):
    n, d = x.shape
    assert n % tm == 0
    const = lambda i: (0, 0)
    return pl.pallas_call(
        _norm_proj_kernel,
        grid=(n // tm,),
        in_specs=[pl.BlockSpec((tm, d), lambda i: (i, 0)), pl.BlockSpec((1, d), const)]
        + [pl.BlockSpec(w.shape, const) for w in ws],
        out_specs=[pl.BlockSpec((tm, w.shape[1]), lambda i: (i, 0)) for w in ws],
        out_shape=[jax.ShapeDtypeStruct((n, w.shape[1]), F32) for w in ws],
        compiler_params=pltpu.CompilerParams(dimension_semantics=("parallel",), vmem_limit_bytes=VMEM_LIMIT),
        name="norm_proj",
    )(x, g.reshape(1, d), *ws)


def _out_mlp_kernel(x_ref, a_ref, b_ref, wa_ref, wb_ref, g_ref, w1_ref, w2_ref, o_ref, h_ref, acc_ref):
    j = pl.program_id(1)

    @pl.when(j == 0)
    def _():
        x = x_ref[...] + _mm(a_ref[...], wa_ref[...]) + _mm(b_ref[...], wb_ref[...])
        h = x * lax.rsqrt(jnp.mean(x * x, axis=-1, keepdims=True) + RMS_EPS) * g_ref[...]
        h_ref[...] = h.astype(MXU_DTYPE)
        acc_ref[...] = x

    u = jnp.maximum(jnp.dot(h_ref[...], w1_ref[...], preferred_element_type=F32), 0.0)
    acc_ref[...] += jnp.dot((u * u).astype(MXU_DTYPE), w2_ref[...], preferred_element_type=F32)

    @pl.when(j == pl.num_programs(1) - 1)
    def _():
        o_ref[...] = acc_ref[...]


def _out_mlp(x, a, b, w_out, g, w1, w2, tm=MLP_ROWS, tf=MLP_FF_COLS):
    n, d = x.shape
    f = w1.shape[1]
    tm = min(tm, n)
    tf = min(tf, f)
    assert n % tm == 0 and f % tf == 0
    wa = w_out[:a.shape[1]].astype(MXU_DTYPE)
    wb = w_out[a.shape[1]:].astype(MXU_DTYPE)
    row = lambda i, j: (i, 0)
    const = lambda i, j: (0, 0)
    return pl.pallas_call(
        _out_mlp_kernel,
        grid=(n // tm, f // tf),
        in_specs=[pl.BlockSpec((tm, d), row), pl.BlockSpec((tm, a.shape[1]), row), pl.BlockSpec((tm, b.shape[1]), row),
                  pl.BlockSpec(wa.shape, const), pl.BlockSpec(wb.shape, const), pl.BlockSpec((1, d), const),
                  pl.BlockSpec((d, tf), lambda i, j: (0, j)), pl.BlockSpec((tf, d), lambda i, j: (j, 0))],
        out_specs=pl.BlockSpec((tm, d), row),
        out_shape=jax.ShapeDtypeStruct((n, d), F32),
        scratch_shapes=[pltpu.VMEM((tm, d), MXU_DTYPE), pltpu.VMEM((tm, d), F32)],
        compiler_params=pltpu.CompilerParams(dimension_semantics=("parallel", "arbitrary"),
                                             vmem_limit_bytes=VMEM_LIMIT),
        name="out_mlp",
    )(x, a, b, wa, wb, g.reshape(1, d), w1.astype(MXU_DTYPE), w2.astype(MXU_DTYPE))


def _bmm(a, b):
    return lax.dot_general(a.astype(MXU_DTYPE), b.astype(MXU_DTYPE), (((2,), (1,)), ((0,), (0,))),
                           preferred_element_type=F32)


def _bmm_nt(a, b):
    return lax.dot_general(a.astype(MXU_DTYPE), b.astype(MXU_DTYPE), (((2,), (2,)), ((0,), (0,))),
                           preferred_element_type=F32)


def _bmm_tn(a, b):
    return lax.dot_general(a.astype(MXU_DTYPE), b.astype(MXU_DTYPE), (((1,), (1,)), ((0,), (0,))),
                           preferred_element_type=F32)


def _stack_heads(x, nc, c):
    return jnp.stack([x[i * c:(i + 1) * c, h * HEAD_DIM:(h + 1) * HEAD_DIM] for i in range(nc) for h in range(N_HEADS)])


def _unstack_heads(y, nc):
    return jnp.concatenate([jnp.concatenate([y[i * N_HEADS + h] for h in range(N_HEADS)], axis=-1)
                            for i in range(nc)], axis=0)


def _batched_unit_lower_solve(x, y):
    c = x.shape[1]
    y = y + _bmm(x, y)
    p = 2
    while p < c:
        x = _bmm(x, x)
        y = y + _bmm(x, y)
        p *= 2
    return y


def _split3(x):
    hi = x.astype(MXU_DTYPE)
    r = x - hi.astype(F32)
    mid = r.astype(MXU_DTYPE)
    lo = (r - mid.astype(F32)).astype(MXU_DTYPE)
    return hi, mid, lo


def _chunk_cumsum_rows(x, c):
    tri = (_iota((c, c), 0) >= _iota((c, c), 1)).astype(MXU_DTYPE)
    parts = _split3(x)
    return jnp.concatenate(
        [_tree_sum([jnp.dot(tri, p[i * c:(i + 1) * c], preferred_element_type=F32) for p in parts])
         for i in range(x.shape[0] // c)], axis=0)


def _chunk_cumsum_cols(x, c):
    tri = (_iota((c, c), 0) <= _iota((c, c), 1)).astype(MXU_DTYPE)
    parts = _split3(x)
    return jnp.concatenate(
        [_tree_sum([jnp.dot(p[:, i * c:(i + 1) * c], tri, preferred_element_type=F32) for p in parts])
         for i in range(x.shape[1] // c)], axis=1)


def _run_chunks(st_ref, m, q, oq, o0, gamma, nc):
    s = st_ref[...]
    outs = []
    for i in range(nc):
        g = slice(i * N_HEADS, (i + 1) * N_HEADS)
        mo = _bmm(jnp.concatenate([m[g], oq[g]], axis=1), s)
        outs.append(mo[:, HEAD_DIM:] + o0[g])
        s = gamma[g] * s + mo[:, :HEAD_DIM] + q[g]
    st_ref[...] = s
    return jnp.concatenate(outs, axis=0)


def _rwkv_kernel(p_ref, mu_ref, w0_ref, w2_ref, a0_ref, a2_ref, g2_ref, kk_ref, ka_ref, rk_ref, lnw_ref, lnb_ref,
                 o_ref, prev_ref, st_ref):
    tb = p_ref.shape[1]
    c = SCAN_CHUNK
    nc = tb // c

    @pl.when(pl.program_id(1) == 0)
    def _():
        prev_ref[...] = jnp.zeros_like(prev_ref)
        st_ref[...] = jnp.zeros_like(st_ref)

    p = p_ref[0]
    shifted = jnp.where(_iota((tb, 1), 0) == 0, prev_ref[...], pltpu.roll(p, 1, 0))
    prev_ref[...] = p[tb - 1:tb, :]
    x = p + (shifted - p) * mu_ref[...]

    r = x[:, 0:WIDTH]
    k = x[:, WIDTH:2 * WIDTH]
    v = x[:, 2 * WIDTH:3 * WIDTH]
    off = 3 * WIDTH
    wd = x[:, off:off + RWKV_DECAY_LORA]
    ad = x[:, off + RWKV_DECAY_LORA:off + RWKV_DECAY_LORA + RWKV_A_LORA]
    gd = x[:, off + RWKV_DECAY_LORA + RWKV_A_LORA:]

    w_pre = w0_ref[...] + _mm(jnp.tanh(wd), w2_ref[...])
    log_w = -(_sigmoid(w_pre) * math.exp(-0.5))
    a = _sigmoid(a0_ref[...] + _mm(ad, a2_ref[...]))
    g = _mm(_sigmoid(gd), g2_ref[...])
    kk_all = k * kk_ref[...]
    k = k * (1.0 + (a - 1.0) * ka_ref[...])

    gam = _chunk_cumsum_rows(log_w, c)
    gam_last = jnp.concatenate([jnp.broadcast_to(gam[(i + 1) * c - 1:(i + 1) * c, :], (c, WIDTH)) for i in range(nc)],
                               axis=0)

    seg = _head_segments(WIDTH)
    kk = kk_all * lax.rsqrt(_head_sums(kk_all * kk_all, seg) + 1e-6)
    b = kk * a
    e_neg = jnp.exp(-gam)
    tail = jnp.exp(gam_last - gam)

    st = lambda y: _stack_heads(y, nc, c)
    v_h = st(v)
    a_t = st(-kk * jnp.exp(gam - log_w))
    r_t = st(r * jnp.exp(gam))
    b_t = st(b * e_neg)
    k_t = st(k * e_neg)
    e_last = st(jnp.exp(gam_last))[:, 0:1, :]

    eye = (_iota((c, c), 0) == _iota((c, c), 1))[None]

    inter = _bmm_nt(jnp.concatenate([a_t, r_t], axis=1), jnp.concatenate([b_t, k_t], axis=1))
    row2 = _iota((c, 2 * c), 0)
    col2 = _iota((c, 2 * c), 1)
    col2 = jnp.where(col2 >= c, col2 - c, col2)
    top = jnp.where((row2 > col2)[None], inter[:, :c], 0.0)
    bot = jnp.where((row2 >= col2)[None], inter[:, c:], 0.0)
    akv = _bmm(top[:, :, c:], v_h)
    wu = _batched_unit_lower_solve(top[:, :, :c], jnp.concatenate([a_t, akv], axis=-1))
    wu_v = jnp.concatenate([wu, jnp.concatenate([jnp.zeros_like(v_h), v_h], axis=-1)], axis=1)
    ro = _bmm(bot, wu_v)
    oq = r_t + ro[:, :, :HEAD_DIM]
    o0 = ro[:, :, HEAD_DIM:]
    mq = _bmm_tn(jnp.concatenate([st(b * tail), st(k * tail)], axis=1), wu_v)
    m = mq[:, :, :HEAD_DIM]
    q = mq[:, :, HEAD_DIM:]
    gamma = jnp.sum(jnp.where(eye, e_last, 0.0), axis=2, keepdims=True)

    o = _unstack_heads(_run_chunks(st_ref, m, q, oq, o0, gamma, nc), nc)

    oc = o - _head_sums(o, seg) * (1.0 / HEAD_DIM)
    on = oc * lax.rsqrt(_head_sums(oc * oc, seg) * (1.0 / HEAD_DIM) + RWKV_GN_EPS)
    bonus = _head_sums(r * k * rk_ref[...], seg) * v
    o_ref[0] = (on * lnw_ref[...] + lnb_ref[...] + bonus) * g


def _rwkv(p, mu, w0, w2, a0, a2, g2, k_k, k_a, r_k, lnx_w, lnx_b, tb=SCAN_BLOCK):
    b, t, cols = p.shape
    tb = min(tb, t)
    assert t % tb == 0 and tb % SCAN_CHUNK == 0 and cols == RWKV_COLS
    row = lambda v: v.reshape(1, -1).astype(F32)
    params = [row(mu), row(w0), w2.astype(MXU_DTYPE), row(a0), a2.astype(MXU_DTYPE), g2.astype(MXU_DTYPE),
              row(k_k), row(k_a), row(r_k), row(lnx_w), row(lnx_b)]
    const = lambda i, j: (0, 0)
    return pl.pallas_call(
        _rwkv_kernel,
        grid=(b, t // tb),
        in_specs=[pl.BlockSpec((1, tb, cols), lambda i, j: (i, j, 0))] + [pl.BlockSpec(q.shape, const) for q in params],
        out_specs=pl.BlockSpec((1, tb, WIDTH), lambda i, j: (i, j, 0)),
        out_shape=jax.ShapeDtypeStruct((b, t, WIDTH), F32),
        scratch_shapes=[pltpu.VMEM((1, cols), F32), pltpu.VMEM((N_HEADS, HEAD_DIM, HEAD_DIM), F32)],
        compiler_params=pltpu.CompilerParams(dimension_semantics=("parallel", "arbitrary"),
                                             vmem_limit_bytes=VMEM_LIMIT),
        name="rwkv7",
    )(p, *params)


def _gdn_kernel(p_ref, cw_ref, alog_r_ref, dt_r_ref, alog_c_ref, dt_c_ref, nw_ref, o_ref, xpad_ref, st_ref):
    tb = p_ref.shape[1]
    c = SCAN_CHUNK
    nc = tb // c
    qkv_w = 3 * WIDTH

    @pl.when(pl.program_id(1) == 0)
    def _():
        xpad_ref[:8, :] = jnp.zeros((8, xpad_ref.shape[1]), F32)
        st_ref[...] = jnp.zeros_like(st_ref)

    p = p_ref[0]
    xin = p[:, :qkv_w]
    z = p[:, qkv_w:qkv_w + WIDTH]
    b_in = p[:, qkv_w + WIDTH:qkv_w + WIDTH + N_HEADS]
    a_in = p[:, qkv_w + WIDTH + N_HEADS:qkv_w + WIDTH + 2 * N_HEADS]

    xpad_ref[8:, :] = xin
    conv = xin * cw_ref[GDN_CONV - 1:GDN_CONV, :]
    for s in range(1, GDN_CONV):
        conv = conv + xpad_ref[8 - s:8 - s + tb, :] * cw_ref[GDN_CONV - 1 - s:GDN_CONV - s, :]
    xpad_ref[:8, :] = xin[tb - 8:, :]
    qkv = _silu(conv)

    beta = _sigmoid(b_in)
    g_col = -jnp.exp(alog_r_ref[...]) * _softplus(a_in + dt_r_ref[...])
    gc_col = _chunk_cumsum_rows(g_col, c)
    eye_h = (_iota((N_HEADS, N_HEADS), 0) == _iota((N_HEADS, N_HEADS), 1)).astype(F32)
    a_row = _mm32_nt(eye_h, a_in)
    g_row = -jnp.exp(alog_c_ref[...]) * _softplus(a_row + dt_c_ref[...])
    gc_row = _chunk_cumsum_cols(g_row, c)

    def per_head_cols(y):
        return jnp.stack([y[i * c:(i + 1) * c, h:h + 1] for i in range(nc) for h in range(N_HEADS)])

    def over_head_lanes(y):
        expand = (_iota((N_HEADS, WIDTH), 1) // HEAD_DIM == _iota((N_HEADS, WIDTH), 0)).astype(MXU_DTYPE)
        return _tree_sum([jnp.dot(part, expand, preferred_element_type=F32) for part in _split3(y)])

    gc = per_head_cols(gc_col)
    g_rows = jnp.stack([gc_row[h:h + 1, i * c:(i + 1) * c] for i in range(nc) for h in range(N_HEADS)])
    gc_d = over_head_lanes(gc_col)
    beta_d = over_head_lanes(beta)
    g_last_d = jnp.concatenate([jnp.broadcast_to(gc_d[(i + 1) * c - 1:(i + 1) * c, :], (c, WIDTH)) for i in range(nc)],
                               axis=0)

    seg = _head_segments(WIDTH)
    q = qkv[:, :WIDTH]
    k = qkv[:, WIDTH:2 * WIDTH]
    q = q * lax.rsqrt(_head_sums(q * q, seg) + 1e-6) * (HEAD_DIM ** -0.5)
    k = k * lax.rsqrt(_head_sums(k * k, seg) + 1e-6)
    kb_d = k * beta_d
    e_g = jnp.exp(gc_d)

    st = lambda y: _stack_heads(y, nc, c)
    q_h, k_h, kb = st(q), st(k), st(kb_d)

    ri = _iota((c, c), 0)
    ci = _iota((c, c), 1)
    incl = (ri >= ci)[None]
    strict = (ri > ci)[None]
    decay = jnp.exp(jnp.where(incl, gc - g_rows, NEG_INF))
    mm = _bmm_nt(jnp.concatenate([kb, q_h], axis=1), k_h)
    a_mat = jnp.where(strict, mm[:, :c] * decay, 0.0)
    qk = jnp.where(incl, mm[:, c:] * decay, 0.0)
    rhs = jnp.concatenate([st(qkv[:, 2 * WIDTH:] * beta_d), st(kb_d * e_g)], axis=-1)
    uw = _batched_unit_lower_solve(-a_mat, rhs)
    qkuw = _bmm(qk, uw)
    o0 = qkuw[:, :, :HEAD_DIM]
    oq = st(q * e_g) - qkuw[:, :, HEAD_DIM:]
    kuw = _bmm_tn(st(k * jnp.exp(g_last_d - gc_d)), uw)
    gamma = jnp.exp(gc[:, c - 1:c, :])
    o = _run_chunks(st_ref, -kuw[:, :, HEAD_DIM:], kuw[:, :, :HEAD_DIM], oq, o0, gamma, nc)

    o = _unstack_heads(o, nc)
    o = o * lax.rsqrt(_head_sums(o * o, seg) * (1.0 / HEAD_DIM) + RMS_EPS) * _lane_tile(nw_ref[...], N_HEADS)
    o_ref[0] = o * _silu(z)


def _gdn(p, conv_w, a_log, dt_bias, norm_w, tb=SCAN_BLOCK):
    b, t, cols = p.shape
    tb = min(tb, t)
    assert t % tb == 0 and tb % SCAN_CHUNK == 0 and cols == GDN_COLS
    params = [conv_w.astype(F32), a_log.reshape(1, -1), dt_bias.reshape(1, -1), a_log.reshape(-1, 1),
              dt_bias.reshape(-1, 1), norm_w.reshape(1, -1)]
    const = lambda i, j: (0, 0)
    return pl.pallas_call(
        _gdn_kernel,
        grid=(b, t // tb),
        in_specs=[pl.BlockSpec((1, tb, cols), lambda i, j: (i, j, 0))] + [pl.BlockSpec(q.shape, const) for q in params],
        out_specs=pl.BlockSpec((1, tb, WIDTH), lambda i, j: (i, j, 0)),
        out_shape=jax.ShapeDtypeStruct((b, t, WIDTH), F32),
        scratch_shapes=[pltpu.VMEM((8 + tb, 3 * WIDTH), F32), pltpu.VMEM((N_HEADS, HEAD_DIM, HEAD_DIM), F32)],
        compiler_params=pltpu.CompilerParams(dimension_semantics=("parallel", "arbitrary"),
                                             vmem_limit_bytes=VMEM_LIMIT),
        name="gdn",
    )(p, *params)


def _key_to_float(key):
    return pltpu.bitcast(jnp.where(key >= 0, key, key ^ jnp.int32(0x7FFFFFFF)), F32)


NEG_INF_KEY = -2139095041
COARSE_DTYPE = jnp.bfloat16
SOFTMAX_FLOOR = -1e30


def _dsa_kernel(dq_ref, iq_ref, kv_ref, ikw_ref, iwq_ref, qn_ref, kn_ref, lnw_ref, lnb_ref, o_ref,
                k_s, vt_s, ik_s, score_s, coarse_s, distm_s, *, top_k):
    qb = dq_ref.shape[1]
    t_len = kv_ref.shape[1]
    kc = DSA_KEY_CHUNK
    n_chunks = t_len // kc
    kvw = DSA_KV_HEADS * HEAD_DIM
    i = pl.program_id(1)
    n_live = (i * qb + qb + kc - 1) // kc

    @pl.when(i == 0)
    def _():
        kv = kv_ref[0]
        kraw = kv[:, :kvw]
        kn = kraw * lax.rsqrt(_head_sums(kraw * kraw, _head_segments(kvw)) * (1.0 / HEAD_DIM) + RMS_EPS)
        k_s[...] = (kn * _lane_tile(kn_ref[...], DSA_KV_HEADS)).astype(MXU_DTYPE)
        for c in range(n_chunks):
            vt_s[c] = kv[c * kc:(c + 1) * kc, kvw:].T.astype(MXU_DTYPE)
        ik = ikw_ref[0][:, :IDX_DIM]
        ikc = ik - jnp.mean(ik, axis=-1, keepdims=True)
        ik = ikc * lax.rsqrt(jnp.mean(ikc * ikc, axis=-1, keepdims=True) + 1e-6) * lnw_ref[...] + lnb_ref[...]
        hi, lo = _split_hi_lo(ik)
        ik_s[...] = jnp.concatenate([hi, hi, lo], axis=-1)
        score_s[...] = jnp.full(score_s.shape, NEG_INF, F32)

    t_row = i * qb + _iota((1, qb), 1)
    s_loc = _iota((kc, 1), 0)

    eye_h = (_iota((IDX_HEADS, IDX_HEADS), 0) == _iota((IDX_HEADS, IDX_HEADS), 1)).astype(F32)
    iw_t = _mm32_nt(eye_h, iwq_ref[0][:, IDX_DIM:IDX_DIM + IDX_HEADS]) * (IDX_HEADS ** -0.5 * IDX_DIM ** -0.5)
    iw_wide = jnp.concatenate([iw_t[h:h + 1, :] for h in range(IDX_HEADS)], axis=1)
    iq = iq_ref[0]
    iq_cat = []
    for h in range(IDX_HEADS):
        hi, lo = _split_hi_lo(_head(iq, h))
        iq_cat.append(jnp.concatenate([hi, lo, hi], axis=-1))
    iq_all = jnp.concatenate(iq_cat, axis=0)

    def score_chunk(c, carry):
        rows = pl.ds(pl.multiple_of(c * kc, kc), kc)
        dots = lax.dot_general(ik_s[rows, :], iq_all, (((1,), (1,)), ((), ())), preferred_element_type=F32)
        w = iw_wide * jnp.maximum(dots, 0.0)
        sc = _tree_sum([w[:, h * qb:(h + 1) * qb] for h in range(IDX_HEADS)])
        sc = jnp.where(c * kc + s_loc <= t_row, sc, NEG_INF)
        score_s[rows, :] = sc
        coarse_s[rows, :] = sc.astype(COARSE_DTYPE)
        return carry

    lax.fori_loop(0, n_live, score_chunk, 0)

    kf = float(top_k)
    grp = DSA_COUNT_ROWS
    n_grp = (i * qb + qb + grp - 1) // grp

    def count(pred):
        def body(g, acc):
            blk = score_s[pl.ds(pl.multiple_of(g * grp, grp), grp), :]
            ones = jnp.where(pred(blk), 1.0, 0.0)
            return acc + _tree_sum([ones[r * 8:(r + 1) * 8] for r in range(grp // 8)])
        acc = lax.fori_loop(0, n_grp, body, jnp.zeros((8, qb), F32))
        return jnp.sum(acc, axis=0, keepdims=True)

    def count_coarse(cand):
        def body(g, acc):
            blk = coarse_s[pl.ds(pl.multiple_of(g * grp, grp), grp), :]
            ones = jnp.where(blk >= cand, jnp.ones((), COARSE_DTYPE), jnp.zeros((), COARSE_DTYPE))
            return acc + _tree_sum([ones[r * 16:(r + 1) * 16] for r in range(grp // 16)]).astype(F32)
        acc = lax.fori_loop(0, n_grp, body, jnp.zeros((16, qb), F32))
        return jnp.sum(acc, axis=0, keepdims=True)

    def coarse_key(base):
        return jnp.where(base >= 0, base, base | jnp.int32(0xFFFF))

    int_min = jnp.int32(-2 ** 31)
    base = jnp.where(count_coarse(jnp.zeros((1, qb), COARSE_DTYPE)) >= kf, jnp.int32(0), int_min)

    def coarse_step(b, base):
        cand = base | (jnp.int32(1) << (30 - b))
        cand_f = _key_to_float(coarse_key(cand)).astype(COARSE_DTYPE)
        return jnp.where(count_coarse(cand_f) >= kf, cand, base)

    base = lax.fori_loop(0, 15, coarse_step, base)
    lo_key = coarse_key(base) - jnp.int32(2 ** 15)

    def fine_step(b, off):
        cand_off = off | (jnp.int32(1) << (16 - b))
        cand_f = _key_to_float(lo_key + cand_off)
        return jnp.where(count(lambda blk: blk >= cand_f) >= kf, cand_off, off)

    tau = lo_key + lax.fori_loop(0, 17, fine_step, jnp.zeros((1, qb), jnp.int32))
    tau_f = _key_to_float(jnp.maximum(tau, jnp.int32(NEG_INF_KEY)))

    n_ge = count(lambda blk: blk >= tau_f)
    no_partial_tie = jnp.min(jnp.where((n_ge <= kf) | (tau_f == NEG_INF), 1.0, 0.0)) > 0.5

    def masked_distance(c, sel_fn):
        rows = pl.ds(pl.multiple_of(c * kc, kc), kc)
        s_pos = c * kc + s_loc
        sel = sel_fn(score_s[rows, :]) & (s_pos <= t_row)
        distm_s[rows, :] = jnp.where(sel, (t_row - s_pos).astype(F32), jnp.inf)

    @pl.when(no_partial_tie)
    def _():
        def select_chunk(c, carry):
            masked_distance(c, lambda blk: blk >= tau_f)
            return carry
        lax.fori_loop(0, n_live, select_chunk, 0)

    @pl.when(jnp.logical_not(no_partial_tie))
    def _():
        lower = (_iota((kc, kc), 0) >= _iota((kc, kc), 1)).astype(MXU_DTYPE)
        need = kf - count(lambda blk: blk > tau_f)

        def select_chunk(c, run):
            def sel_fn(blk):
                eq = blk == tau_f
                pref = run + jnp.dot(lower, jnp.where(eq, 1.0, 0.0).astype(MXU_DTYPE), preferred_element_type=F32)
                return (blk > tau_f) | (eq & (pref <= need))
            masked_distance(c, sel_fn)
            rows = pl.ds(pl.multiple_of(c * kc, kc), kc)
            return run + jnp.sum(jnp.where(score_s[rows, :] == tau_f, 1.0, 0.0), axis=0, keepdims=True)
        lax.fori_loop(0, n_live, select_chunk, jnp.zeros((1, qb), F32))

    dq = dq_ref[0]
    dq = dq * lax.rsqrt(_head_sums(dq * dq, _head_segments(WIDTH)) * (1.0 / HEAD_DIM) + RMS_EPS)
    dq = (dq * _lane_tile(qn_ref[...] * (HEAD_DIM ** -0.5 * LOG2_E), N_HEADS)).astype(MXU_DTYPE)
    qs = [_head(dq, h) for h in range(N_HEADS)]
    q_grp = [jnp.concatenate(qs[g * DSA_GROUP:(g + 1) * DSA_GROUP], axis=0) for g in range(DSA_KV_HEADS)]
    slopes = jnp.concatenate([jnp.full((1, qb), DSA_SLOPES[h] * LOG2_E, F32) for h in range(N_HEADS)], axis=1)
    gw = DSA_GROUP * qb

    def attend_chunk(c, carry):
        m, l, acc = carry
        rows = pl.ds(pl.multiple_of(c * kc, kc), kc)
        kch = k_s[rows, :]
        vt = vt_s[c]
        sc = jnp.concatenate(
            [lax.dot_general(kch[:, g * HEAD_DIM:(g + 1) * HEAD_DIM], q_grp[g], (((1,), (1,)), ((), ())),
                             preferred_element_type=F32) for g in range(DSA_KV_HEADS)], axis=1)
        sc = sc - slopes * _lane_tile(distm_s[rows, :], N_HEADS)
        m_new = jnp.maximum(m, jnp.max(sc, axis=0, keepdims=True))
        alpha = jnp.exp2(m - m_new)
        e = jnp.exp2(sc - m_new)
        l = alpha * l + jnp.sum(e, axis=0, keepdims=True)
        eb = e.astype(MXU_DTYPE)
        pv = jnp.concatenate(
            [jnp.dot(vt[g * HEAD_DIM:(g + 1) * HEAD_DIM, :], eb[:, g * gw:(g + 1) * gw], preferred_element_type=F32)
             for g in range(DSA_KV_HEADS)], axis=1)
        return m_new, l, alpha * acc + pv

    init = (jnp.full((1, N_HEADS * qb), SOFTMAX_FLOOR, F32), jnp.zeros((1, N_HEADS * qb), F32),
            jnp.zeros((HEAD_DIM, N_HEADS * qb), F32))
    _, l, acc = lax.fori_loop(0, n_live, attend_chunk, init)
    o_t = jnp.concatenate([acc[:, h * qb:(h + 1) * qb] / l[:, h * qb:(h + 1) * qb] for h in range(N_HEADS)], axis=0)
    o_ref[0] = o_t.T


def _dsa(p, q_norm, k_norm, ln_w, ln_b):
    b, t, cols = p.shape
    qb = DSA_Q_BLOCK
    kc = DSA_KEY_CHUNK
    assert t % qb == 0 and cols == DSA_COLS and t % DSA_COUNT_ROWS == 0 and t % kc == 0 and kc % DSA_COUNT_ROWS == 0
    top_k = min(DSA_TOPK_MAX, t // 4)
    kvw = 2 * DSA_KV_HEADS * HEAD_DIM
    row = lambda v: v.reshape(1, -1).astype(F32)
    params = [row(q_norm), row(k_norm), row(ln_w), row(ln_b)]
    const = lambda i, j: (0, 0)
    return pl.pallas_call(
        functools.partial(_dsa_kernel, top_k=top_k),
        grid=(b, t // qb),
        in_specs=[pl.BlockSpec((1, qb, WIDTH), lambda i, j: (i, j, 0)),
                  pl.BlockSpec((1, qb, WIDTH), lambda i, j: (i, j, 1)),
                  pl.BlockSpec((1, t, kvw), lambda i, j: (i, 0, 2 * WIDTH // kvw)),
                  pl.BlockSpec((1, t, LANES), lambda i, j: (i, 0, (2 * WIDTH + kvw) // LANES)),
                  pl.BlockSpec((1, qb, LANES), lambda i, j: (i, j, (2 * WIDTH + kvw) // LANES))]
        + [pl.BlockSpec(q.shape, const) for q in params],
        out_specs=pl.BlockSpec((1, qb, WIDTH), lambda i, j: (i, j, 0)),
        out_shape=jax.ShapeDtypeStruct((b, t, WIDTH), F32),
        scratch_shapes=[pltpu.VMEM((t, DSA_KV_HEADS * HEAD_DIM), MXU_DTYPE),
                        pltpu.VMEM((t // kc, DSA_KV_HEADS * HEAD_DIM, kc), MXU_DTYPE),
                        pltpu.VMEM((t, 3 * IDX_DIM), MXU_DTYPE),
                        pltpu.VMEM((t, qb), F32),
                        pltpu.VMEM((t, qb), COARSE_DTYPE),
                        pltpu.VMEM((t, qb), F32)],
        compiler_params=pltpu.CompilerParams(dimension_semantics=("parallel", "arbitrary"),
                                             vmem_limit_bytes=VMEM_LIMIT),
        name="dsa",
    )(p, p, p, p, p, *params)


MOBA_PENALTY = 1e30


def _moba_kernel(q_ref, k_ref, v_ref, qn_ref, kn_ref, o_ref, k_s, vt_s, km_s, causal_s):
    bs = q_ref.shape[1]
    t_len = k_ref.shape[1]
    n_kb = t_len // bs
    n_pad = -(-n_kb // 8) * 8
    i = pl.program_id(1)

    @pl.when(i == 0)
    def _():
        kraw = k_ref[0]
        kn = kraw * lax.rsqrt(_head_sums(kraw * kraw, _head_segments(WIDTH)) * (1.0 / HEAD_DIM) + RMS_EPS)
        kn = kn * _lane_tile(kn_ref[...], N_HEADS)
        s_pos = _iota((t_len, 1), 0)
        fcol = _iota((1, HEAD_DIM), 1)
        extra = jnp.where(fcol == n_pad, (s_pos % bs).astype(F32),
                          jnp.where((fcol == n_pad + 1) | (fcol == s_pos // bs), 1.0, 0.0)).astype(MXU_DTYPE)
        for h in range(N_HEADS):
            k_s[:, 2 * h * HEAD_DIM:(2 * h + 1) * HEAD_DIM] = _head(kn, h).astype(MXU_DTYPE)
            k_s[:, (2 * h + 1) * HEAD_DIM:(2 * h + 2) * HEAD_DIM] = extra
        means = jnp.concatenate([jnp.mean(kn[j * bs:(j + 1) * bs], axis=0, keepdims=True) for j in range(n_kb)]
                                + [jnp.zeros((n_pad - n_kb, WIDTH), F32)] * (n_pad > n_kb), axis=0)
        head_of_lane = _iota((1, WIDTH), 1) // HEAD_DIM
        rows = [jnp.where(head_of_lane == h, means, 0.0) for h in range(N_HEADS)]
        rows.append(jnp.zeros((LANES - N_HEADS * n_pad, WIDTH), F32))
        table = jnp.concatenate(rows, axis=0).T
        hi, lo = _split_hi_lo(table)
        km_s[...] = jnp.concatenate([hi, lo, hi], axis=0)
        v = v_ref[0]
        for j in range(n_kb):
            vt_s[j] = v[j * bs:(j + 1) * bs, :].T.astype(MXU_DTYPE)
        above = _iota((bs, 1), 0) > _iota((1, bs), 1)
        causal_s[...] = _lane_tile(jnp.where(above, NEG_INF, 0.0), N_HEADS)

    qraw = q_ref[0]
    jcol = _iota((n_pad, 1), 0)
    own = pl.multiple_of(i * bs, bs)
    qn = qraw * lax.rsqrt(_head_sums(qraw * qraw, _head_segments(WIDTH)) * (1.0 / HEAD_DIM) + RMS_EPS)
    qn = qn * _lane_tile(qn_ref[...], N_HEADS)
    q_hi, q_lo = _split_hi_lo(qn)
    gate_qm = jnp.dot(jnp.concatenate([q_hi, q_hi, q_lo], axis=1), km_s[...], preferred_element_type=F32)
    gate_rows = gate_qm.T
    gate = jnp.concatenate([gate_rows[h * n_pad:(h + 1) * n_pad] for h in range(N_HEADS)], axis=1)
    gate = jnp.where(jcol < i, gate, NEG_INF)
    rank = jnp.zeros(gate.shape, F32)
    for j2 in range(n_kb):
        other = gate[j2:j2 + 1, :]
        beats = (other > gate) | ((other == gate) & (j2 < jcol))
        rank = rank + jnp.where(beats, 1.0, 0.0)
    picked = (rank < float(MOBA_TOPK)) & (jcol < i)
    slopes = jnp.concatenate([jnp.full((1, bs), MOBA_SLOPES[h], F32) for h in range(N_HEADS)], axis=1)
    block_term = jnp.where(jcol < i, -(slopes * ((i - jcol) * bs).astype(F32) + jnp.where(picked, 0.0, MOBA_PENALTY)),
                           0.0)
    q_t = (qn * (HEAD_DIM ** -0.5)).T
    frow = _iota((HEAD_DIM - n_pad, 1), 0)
    t_loc = _iota((1, bs), 1).astype(F32)
    q_aug = []
    for h in range(N_HEADS):
        tail = jnp.where(frow == 0, MOBA_SLOPES[h], jnp.where(frow == 1, -MOBA_SLOPES[h] * t_loc, 0.0))
        q_aug.append(jnp.concatenate([q_t[h * HEAD_DIM:(h + 1) * HEAD_DIM], block_term[:, h * bs:(h + 1) * bs], tail],
                                     axis=0).astype(MXU_DTYPE))

    def scores(start):
        return jnp.concatenate(
            [jnp.dot(k_s[pl.ds(start, bs), 2 * h * HEAD_DIM:(2 * h + 2) * HEAD_DIM], q_aug[h],
                     preferred_element_type=F32) for h in range(N_HEADS)], axis=1)

    def weighted_values(j, e):
        vt = vt_s[j]
        eb = e.astype(MXU_DTYPE)
        return jnp.concatenate(
            [jnp.dot(vt[h * HEAD_DIM:(h + 1) * HEAD_DIM, :], eb[:, h * bs:(h + 1) * bs], preferred_element_type=F32)
             for h in range(N_HEADS)], axis=1)

    sc = scores(own) + causal_s[...]
    m0 = jnp.max(sc, axis=0, keepdims=True)
    e = jnp.exp(sc - m0)
    l0 = jnp.sum(e, axis=0, keepdims=True)
    acc0 = weighted_values(i, e)

    def block_step(j, carry):
        m, l, acc = carry
        s2 = scores(pl.multiple_of(j * bs, bs))
        m_new = jnp.maximum(m, jnp.max(s2, axis=0, keepdims=True))
        alpha = jnp.exp(m - m_new)
        e2 = jnp.exp(s2 - m_new)
        l = alpha * l + jnp.sum(e2, axis=0, keepdims=True)
        return m_new, l, alpha * acc + weighted_values(j, e2)

    _, l, acc = lax.fori_loop(0, i, block_step, (m0, l0, acc0))
    o_t = jnp.concatenate([acc[:, h * bs:(h + 1) * bs] / l[:, h * bs:(h + 1) * bs] for h in range(N_HEADS)], axis=0)
    o_ref[0] = o_t.T


def _moba(p, q_norm, k_norm):
    b, t, cols = p.shape
    bs = MOBA_BLOCK
    n_pad = -(-(t // bs) // 8) * 8
    assert t % bs == 0 and cols == MOBA_COLS and n_pad + 2 <= HEAD_DIM and N_HEADS * n_pad <= LANES and bs <= 256
    row = lambda v: v.reshape(1, -1).astype(F32)
    const = lambda i, j: (0, 0)
    return pl.pallas_call(
        _moba_kernel,
        grid=(b, t // bs),
        in_specs=[pl.BlockSpec((1, bs, WIDTH), lambda i, j: (i, j, 0)),
                  pl.BlockSpec((1, t, WIDTH), lambda i, j: (i, 0, 1)),
                  pl.BlockSpec((1, t, WIDTH), lambda i, j: (i, 0, 2)),
                  pl.BlockSpec((1, HEAD_DIM), const), pl.BlockSpec((1, HEAD_DIM), const)],
        out_specs=pl.BlockSpec((1, bs, WIDTH), lambda i, j: (i, j, 0)),
        out_shape=jax.ShapeDtypeStruct((b, t, WIDTH), F32),
        scratch_shapes=[pltpu.VMEM((t, 2 * WIDTH), MXU_DTYPE),
                        pltpu.VMEM((t // bs, WIDTH, bs), MXU_DTYPE),
                        pltpu.VMEM((3 * WIDTH, LANES), MXU_DTYPE),
                        pltpu.VMEM((bs, N_HEADS * bs), F32)],
        compiler_params=pltpu.CompilerParams(dimension_semantics=("parallel", "arbitrary"),
                                             vmem_limit_bytes=VMEM_LIMIT),
        name="moba",
    )(p, p, p, row(q_norm), row(k_norm))


def _pad_cols(w, n):
    return jnp.pad(w, ((0, 0), (0, n - w.shape[1])))


def _even_layer(x, b, t, norm_g, norm2_g, mlp_w1, mlp_w2, w_in, w_out, mu, w0, w2, a0, a2, g2, k_k, k_a, r_k,
                lnx_w, lnx_b, conv_w, a_log, dt_bias, gdn_norm_w):
    qkv_w = 3 * WIDTH
    w_rwkv = w_in[:, :RWKV_COLS]
    w_g = w_in[:, RWKV_COLS:]
    w_gdn = jnp.concatenate([w_g[:, :qkv_w], w_g[:, qkv_w + 2 * N_HEADS:],
                             _pad_cols(w_g[:, qkv_w:qkv_w + 2 * N_HEADS], LANES)], axis=1)
    p_rwkv, p_gdn = _norm_proj(x, norm_g, [w_rwkv.astype(MXU_DTYPE), w_gdn.astype(MXU_DTYPE)])
    o_a = _rwkv(p_rwkv.reshape(b, t, -1), mu, w0, w2, a0, a2, g2, k_k, k_a, r_k, lnx_w, lnx_b)
    o_b = _gdn(p_gdn.reshape(b, t, -1), conv_w, a_log, dt_bias, gdn_norm_w)
    return _out_mlp(x, o_a.reshape(b * t, -1), o_b.reshape(b * t, -1), w_out, norm2_g, mlp_w1, mlp_w2)


def _odd_layer(x, b, t, norm_g, norm2_g, mlp_w1, mlp_w2, w_in, w_out, dsa_q_norm, dsa_k_norm, idx_ln_w, idx_ln_b,
               moba_q_norm, moba_k_norm):
    kvw = 2 * DSA_KV_HEADS * HEAD_DIM
    sizes = (WIDTH, kvw, IDX_HEADS * IDX_DIM, IDX_DIM, IDX_HEADS, WIDTH, WIDTH, WIDTH)
    offs = [0]
    for s in sizes:
        offs.append(offs[-1] + s)
    dq, dkv, iq, ik, iw, mq, mk, mv = (w_in[:, offs[n]:offs[n + 1]] for n in range(len(sizes)))
    w_dsa = jnp.concatenate([dq, iq, dkv, _pad_cols(jnp.concatenate([ik, iw], axis=1), LANES)], axis=1)
    w_moba = jnp.concatenate([mq, mk, mv], axis=1)
    p_dsa, p_moba = _norm_proj(x, norm_g, [w_dsa.astype(MXU_DTYPE), w_moba.astype(MXU_DTYPE)])
    o_c = _dsa(p_dsa.reshape(b, t, -1), dsa_q_norm, dsa_k_norm, idx_ln_w, idx_ln_b)
    o_d = _moba(p_moba.reshape(b, t, -1), moba_q_norm, moba_k_norm)
    return _out_mlp(x, o_c.reshape(b * t, -1), o_d.reshape(b * t, -1), w_out, norm2_g, mlp_w1, mlp_w2)


def kernel(x, norm1_g, norm2_g, mlp_w1, mlp_w2, ev_w_in, ev_w_out, rwkv_mu, rwkv_w0, rwkv_w2, rwkv_a0, rwkv_a2, rwkv_g2, rwkv_k_k, rwkv_k_a, rwkv_r_k, rwkv_lnx_w, rwkv_lnx_b, gdn_conv_w, gdn_a_log, gdn_dt_bias, gdn_norm_w, od_w_in, od_w_out, dsa_q_norm, dsa_k_norm, idx_k_ln_w, idx_k_ln_b, moba_q_norm, moba_k_norm):
    b, t, d = x.shape
    depth = norm1_g.shape[0]
    h = x.reshape(b * t, d)
    for i in range(depth):
        j = i // 2
        if i % 2 == 0:
            h = _even_layer(h, b, t, norm1_g[i], norm2_g[i], mlp_w1[i], mlp_w2[i], ev_w_in[j], ev_w_out[j],
                            rwkv_mu[j], rwkv_w0[j], rwkv_w2[j], rwkv_a0[j], rwkv_a2[j], rwkv_g2[j], rwkv_k_k[j], rwkv_k_a[j], rwkv_r_k[j],
                            rwkv_lnx_w[j], rwkv_lnx_b[j], gdn_conv_w[j], gdn_a_log[j], gdn_dt_bias[j],
                            gdn_norm_w[j])
        else:
            h = _odd_layer(h, b, t, norm1_g[i], norm2_g[i], mlp_w1[i], mlp_w2[i], od_w_in[j], od_w_out[j],
                           dsa_q_norm[j], dsa_k_norm[j], idx_k_ln_w[j], idx_k_ln_b[j], moba_q_norm[j], moba_k_norm[j])
    return h.reshape(b, t, d)
```

```python
import functools
import math

import jax
import jax.numpy as jnp
from jax import lax
from jax.experimental import pallas as pl
from jax.experimental.pallas import tpu as pltpu

F32 = jnp.float32
MXU_DTYPE = jnp.bfloat16
HIGHEST = lax.Precision.HIGHEST

LANES = 128
VMEM_LIMIT = 56 * 1024 * 1024

HEAD_DIM = 64
N_HEADS = 8
WIDTH = N_HEADS * HEAD_DIM
RMS_EPS = 1e-6

RWKV_DECAY_LORA = 64
RWKV_A_LORA = 64
RWKV_GATE_LORA = 128
RWKV_COLS = 3 * WIDTH + RWKV_DECAY_LORA + RWKV_A_LORA + RWKV_GATE_LORA
RWKV_GN_EPS = 6.4e-4

GDN_CONV = 4
GDN_COLS = 3 * WIDTH + WIDTH + LANES
SCAN_CHUNK = 64
SCAN_BLOCK = 256

DSA_KV_HEADS = 2
DSA_GROUP = N_HEADS // DSA_KV_HEADS
IDX_HEADS = 8
IDX_DIM = 64
DSA_TOPK_MAX = 256
DSA_Q_BLOCK = 256
DSA_KEY_CHUNK = 512
DSA_COUNT_ROWS = 256
DSA_COLS = 2 * WIDTH + 2 * DSA_KV_HEADS * HEAD_DIM + LANES

MOBA_BLOCK = 256
MOBA_TOPK = 3
MOBA_COLS = 3 * WIDTH

PROJ_ROWS = 512
MLP_ROWS = 1024
MLP_FF_COLS = 1024

ALIBI_HEADS = 2 * N_HEADS
NEG_INF = float("-inf")
LOG2_E = math.log2(math.e)


def _alibi_slope(i):
    return 2.0 ** (-8.0 * (i + 1) / ALIBI_HEADS)


DSA_SLOPES = tuple(_alibi_slope(2 * h) for h in range(N_HEADS))
MOBA_SLOPES = tuple(_alibi_slope(2 * h + 1) for h in range(N_HEADS))


def _mm(a, b):
    return jnp.dot(a.astype(MXU_DTYPE), b.astype(MXU_DTYPE), preferred_element_type=F32)


def _mm32_nt(a, b):
    return lax.dot_general(a, b, (((1,), (1,)), ((), ())), preferred_element_type=F32, precision=HIGHEST)


def _iota(shape, dim):
    return lax.broadcasted_iota(jnp.int32, shape, dim)


def _sigmoid(x):
    return 1.0 / (1.0 + jnp.exp(-x))


def _silu(x):
    return x * _sigmoid(x)


def _softplus(x):
    return jnp.maximum(x, 0.0) + jnp.log1p(jnp.exp(-jnp.abs(x)))


def _head(x, h):
    return x[:, h * HEAD_DIM:(h + 1) * HEAD_DIM]


def _split_hi_lo(x):
    hi = x.astype(MXU_DTYPE)
    lo = (x - hi.astype(F32)).astype(MXU_DTYPE)
    return hi, lo


def _head_segments(width):
    return (_iota((width, width), 0) // HEAD_DIM == _iota((width, width), 1) // HEAD_DIM).astype(MXU_DTYPE)


def _head_sums(x, seg):
    hi, lo = _split_hi_lo(x)
    return jnp.dot(hi, seg, preferred_element_type=F32) + jnp.dot(lo, seg, preferred_element_type=F32)


def _tree_sum(parts):
    parts = list(parts)
    while len(parts) > 1:
        nxt = [parts[a] + parts[a + 1] for a in range(0, len(parts) - 1, 2)]
        if len(parts) % 2:
            nxt.append(parts[-1])
        parts = nxt
    return parts[0]


def _lane_tile(x, n):
    return jnp.concatenate([x] * n, axis=1)


def _norm_proj_kernel(x_ref, g_ref, *refs):
    n = len(refs) // 2
    x = x_ref[...]
    h = x * lax.rsqrt(jnp.mean(x * x, axis=-1, keepdims=True) + RMS_EPS) * g_ref[...]
    h = h.astype(MXU_DTYPE)
    for w_ref, o_ref in zip(refs[:n], refs[n:]):
        o_ref[...] = jnp.dot(h, w_ref[...], preferred_element_type=F32)


def _norm_proj(x, g, ws, tm=PROJ_ROWS):
    n, d = x.shape
    assert n % tm == 0
    const = lambda i: (0, 0)
    return pl.pallas_call(
        _norm_proj_kernel,
        grid=(n // tm,),
        in_specs=[pl.BlockSpec((tm, d), lambda i: (i, 0)), pl.BlockSpec((1, d), const)]
        + [pl.BlockSpec(w.shape, const) for w in ws],
        out_specs=[pl.BlockSpec((tm, w.shape[1]), lambda i: (i, 0)) for w in ws],
        out_shape=[jax.ShapeDtypeStruct((n, w.shape[1]), F32) for w in ws],
        compiler_params=pltpu.CompilerParams(dimension_semantics=("parallel",), vmem_limit_bytes=VMEM_LIMIT),
        name="norm_proj",
    )(x, g.reshape(1, d), *ws)


def _out_mlp_kernel(x_ref, a_ref, b_ref, wa_ref, wb_ref, g_ref, w1_ref, w2_ref, o_ref, h_ref, acc_ref):
    j = pl.program_id(1)

    @pl.when(j == 0)
    def _():
        x = x_ref[...] + _mm(a_ref[...], wa_ref[...]) + _mm(b_ref[...], wb_ref[...])
        h = x * lax.rsqrt(jnp.mean(x * x, axis=-1, keepdims=True) + RMS_EPS) * g_ref[...]
        h_ref[...] = h.astype(MXU_DTYPE)
        acc_ref[...] = x

    u = jnp.maximum(jnp.dot(h_ref[...], w1_ref[...], preferred_element_type=F32), 0.0)
    acc_ref[...] += jnp.dot((u * u).astype(MXU_DTYPE), w2_ref[...], preferred_element_type=F32)

    @pl.when(j == pl.num_programs(1) - 1)
    def _():
        o_ref[...] = acc_ref[...]


def _out_mlp(x, a, b, w_out, g, w1, w2, tm=MLP_ROWS, tf=MLP_FF_COLS):
    n, d = x.shape
    f = w1.shape[1]
    tm = min(tm, n)
    tf = min(tf, f)
    assert n % tm == 0 and f % tf == 0
    wa = w_out[:a.shape[1]].astype(MXU_DTYPE)
    wb = w_out[a.shape[1]:].astype(MXU_DTYPE)
    row = lambda i, j: (i, 0)
    const = lambda i, j: (0, 0)
    return pl.pallas_call(
        _out_mlp_kernel,
        grid=(n // tm, f // tf),
        in_specs=[pl.BlockSpec((tm, d), row), pl.BlockSpec((tm, a.shape[1]), row), pl.BlockSpec((tm, b.shape[1]), row),
                  pl.BlockSpec(wa.shape, const), pl.BlockSpec(wb.shape, const), pl.BlockSpec((1, d), const),
                  pl.BlockSpec((d, tf), lambda i, j: (0, j)), pl.BlockSpec((tf, d), lambda i, j: (j, 0))],
        out_specs=pl.BlockSpec((tm, d), row),
        out_shape=jax.ShapeDtypeStruct((n, d), F32),
        scratch_shapes=[pltpu.VMEM((tm, d), MXU_DTYPE), pltpu.VMEM((tm, d), F32)],
        compiler_params=pltpu.CompilerParams(dimension_semantics=("parallel", "arbitrary"),
                                             vmem_limit_bytes=VMEM_LIMIT),
        name="out_mlp",
    )(x, a, b, wa, wb, g.reshape(1, d), w1.astype(MXU_DTYPE), w2.astype(MXU_DTYPE))


def _bmm(a, b):
    return lax.dot_general(a.astype(MXU_DTYPE), b.astype(MXU_DTYPE), (((2,), (1,)), ((0,), (0,))),
                           preferred_element_type=F32)


def _bmm_nt(a, b):
    return lax.dot_general(a.astype(MXU_DTYPE), b.astype(MXU_DTYPE), (((2,), (2,)), ((0,), (0,))),
                           preferred_element_type=F32)


def _bmm_tn(a, b):
    return lax.dot_general(a.astype(MXU_DTYPE), b.astype(MXU_DTYPE), (((1,), (1,)), ((0,), (0,))),
                           preferred_element_type=F32)


def _stack_heads(x, nc, c):
    return jnp.stack([x[i * c:(i + 1) * c, h * HEAD_DIM:(h + 1) * HEAD_DIM] for i in range(nc) for h in range(N_HEADS)])


def _unstack_heads(y, nc):
    return jnp.concatenate([jnp.concatenate([y[i * N_HEADS + h] for h in range(N_HEADS)], axis=-1)
                            for i in range(nc)], axis=0)


def _batched_unit_lower_solve(x, y):
    c = x.shape[1]
    y = y + _bmm(x, y)
    p = 2
    while p < c:
        x = _bmm(x, x)
        y = y + _bmm(x, y)
        p *= 2
    return y


def _split3(x):
    hi = x.astype(MXU_DTYPE)
    r = x - hi.astype(F32)
    mid = r.astype(MXU_DTYPE)
    lo = (r - mid.astype(F32)).astype(MXU_DTYPE)
    return hi, mid, lo


def _chunk_cumsum_rows(x, c):
    tri = (_iota((c, c), 0) >= _iota((c, c), 1)).astype(MXU_DTYPE)
    parts = _split3(x)
    return jnp.concatenate(
        [_tree_sum([jnp.dot(tri, p[i * c:(i + 1) * c], preferred_element_type=F32) for p in parts])
         for i in range(x.shape[0] // c)], axis=0)


def _chunk_cumsum_cols(x, c):
    tri = (_iota((c, c), 0) <= _iota((c, c), 1)).astype(MXU_DTYPE)
    parts = _split3(x)
    return jnp.concatenate(
        [_tree_sum([jnp.dot(p[:, i * c:(i + 1) * c], tri, preferred_element_type=F32) for p in parts])
         for i in range(x.shape[1] // c)], axis=1)


def _run_chunks(st_ref, m, q, oq, o0, gamma, nc):
    s = st_ref[...]
    outs = []
    for i in range(nc):
        g = slice(i * N_HEADS, (i + 1) * N_HEADS)
        mo = _bmm(jnp.concatenate([m[g], oq[g]], axis=1), s)
        outs.append(mo[:, HEAD_DIM:] + o0[g])
        s = gamma[g] * s + mo[:, :HEAD_DIM] + q[g]
    st_ref[...] = s
    return jnp.concatenate(outs, axis=0)


def _rwkv_kernel(p_ref, mu_ref, w0_ref, w2_ref, a0_ref, a2_ref, g2_ref, kk_ref, ka_ref, rk_ref, lnw_ref, lnb_ref,
                 o_ref, prev_ref, st_ref):
    tb = p_ref.shape[1]
    c = SCAN_CHUNK
    nc = tb // c

    @pl.when(pl.program_id(1) == 0)
    def _():
        prev_ref[...] = jnp.zeros_like(prev_ref)
        st_ref[...] = jnp.zeros_like(st_ref)

    p = p_ref[0]
    shifted = jnp.where(_iota((tb, 1), 0) == 0, prev_ref[...], pltpu.roll(p, 1, 0))
    prev_ref[...] = p[tb - 1:tb, :]
    x = p + (shifted - p) * mu_ref[...]

    r = x[:, 0:WIDTH]
    k = x[:, WIDTH:2 * WIDTH]
    v = x[:, 2 * WIDTH:3 * WIDTH]
    off = 3 * WIDTH
    wd = x[:, off:off + RWKV_DECAY_LORA]
    ad = x[:, off + RWKV_DECAY_LORA:off + RWKV_DECAY_LORA + RWKV_A_LORA]
    gd = x[:, off + RWKV_DECAY_LORA + RWKV_A_LORA:]

    w_pre = w0_ref[...] + _mm(jnp.tanh(wd), w2_ref[...])
    log_w = -(_sigmoid(w_pre) * math.exp(-0.5))
    a = _sigmoid(a0_ref[...] + _mm(ad, a2_ref[...]))
    g = _mm(_sigmoid(gd), g2_ref[...])
    kk_all = k * kk_ref[...]
    k = k * (1.0 + (a - 1.0) * ka_ref[...])

    gam = _chunk_cumsum_rows(log_w, c)
    gam_last = jnp.concatenate([jnp.broadcast_to(gam[(i + 1) * c - 1:(i + 1) * c, :], (c, WIDTH)) for i in range(nc)],
                               axis=0)

    seg = _head_segments(WIDTH)
    kk = kk_all * lax.rsqrt(_head_sums(kk_all * kk_all, seg) + 1e-6)
    b = kk * a
    e_neg = jnp.exp(-gam)
    tail = jnp.exp(gam_last - gam)

    st = lambda y: _stack_heads(y, nc, c)
    v_h = st(v)
    a_t = st(-kk * jnp.exp(gam - log_w))
    r_t = st(r * jnp.exp(gam))
    b_t = st(b * e_neg)
    k_t = st(k * e_neg)
    e_last = st(jnp.exp(gam_last))[:, 0:1, :]

    eye = (_iota((c, c), 0) == _iota((c, c), 1))[None]

    inter = _bmm_nt(jnp.concatenate([a_t, r_t], axis=1), jnp.concatenate([b_t, k_t], axis=1))
    row2 = _iota((c, 2 * c), 0)
    col2 = _iota((c, 2 * c), 1)
    col2 = jnp.where(col2 >= c, col2 - c, col2)
    top = jnp.where((row2 > col2)[None], inter[:, :c], 0.0)
    bot = jnp.where((row2 >= col2)[None], inter[:, c:], 0.0)
    akv = _bmm(top[:, :, c:], v_h)
    wu = _batched_unit_lower_solve(top[:, :, :c], jnp.concatenate([a_t, akv], axis=-1))
    wu_v = jnp.concatenate([wu, jnp.concatenate([jnp.zeros_like(v_h), v_h], axis=-1)], axis=1)
    ro = _bmm(bot, wu_v)
    oq = r_t + ro[:, :, :HEAD_DIM]
    o0 = ro[:, :, HEAD_DIM:]
    mq = _bmm_tn(jnp.concatenate([st(b * tail), st(k * tail)], axis=1), wu_v)
    m = mq[:, :, :HEAD_DIM]
    q = mq[:, :, HEAD_DIM:]
    gamma = jnp.sum(jnp.where(eye, e_last, 0.0), axis=2, keepdims=True)

    o = _unstack_heads(_run_chunks(st_ref, m, q, oq, o0, gamma, nc), nc)

    oc = o - _head_sums(o, seg) * (1.0 / HEAD_DIM)
    on = oc * lax.rsqrt(_head_sums(oc * oc, seg) * (1.0 / HEAD_DIM) + RWKV_GN_EPS)
    bonus = _head_sums(r * k * rk_ref[...], seg) * v
    o_ref[0] = (on * lnw_ref[...] + lnb_ref[...] + bonus) * g


def _rwkv(p, mu, w0, w2, a0, a2, g2, k_k, k_a, r_k, lnx_w, lnx_b, tb=SCAN_BLOCK):
    b, t, cols = p.shape
    tb = min(tb, t)
    assert t % tb == 0 and tb % SCAN_CHUNK == 0 and cols == RWKV_COLS
    row = lambda v: v.reshape(1, -1).astype(F32)
    params = [row(mu), row(w0), w2.astype(MXU_DTYPE), row(a0), a2.astype(MXU_DTYPE), g2.astype(MXU_DTYPE),
              row(k_k), row(k_a), row(r_k), row(lnx_w), row(lnx_b)]
    const = lambda i, j: (0, 0)
    return pl.pallas_call(
        _rwkv_kernel,
        grid=(b, t // tb),
        in_specs=[pl.BlockSpec((1, tb, cols), lambda i, j: (i, j, 0))] + [pl.BlockSpec(q.shape, const) for q in params],
        out_specs=pl.BlockSpec((1, tb, WIDTH), lambda i, j: (i, j, 0)),
        out_shape=jax.ShapeDtypeStruct((b, t, WIDTH), F32),
        scratch_shapes=[pltpu.VMEM((1, cols), F32), pltpu.VMEM((N_HEADS, HEAD_DIM, HEAD_DIM), F32)],
        compiler_params=pltpu.CompilerParams(dimension_semantics=("parallel", "arbitrary"),
                                             vmem_limit_bytes=VMEM_LIMIT),
        name="rwkv7",
    )(p, *params)


def _gdn_kernel(p_ref, cw_ref, alog_r_ref, dt_r_ref, alog_c_ref, dt_c_ref, nw_ref, o_ref, xpad_ref, st_ref):
    tb = p_ref.shape[1]
    c = SCAN_CHUNK
    nc = tb // c
    qkv_w = 3 * WIDTH

    @pl.when(pl.program_id(1) == 0)
    def _():
        xpad_ref[:8, :] = jnp.zeros((8, xpad_ref.shape[1]), F32)
        st_ref[...] = jnp.zeros_like(st_ref)

    p = p_ref[0]
    xin = p[:, :qkv_w]
    z = p[:, qkv_w:qkv_w + WIDTH]
    b_in = p[:, qkv_w + WIDTH:qkv_w + WIDTH + N_HEADS]
    a_in = p[:, qkv_w + WIDTH + N_HEADS:qkv_w + WIDTH + 2 * N_HEADS]

    xpad_ref[8:, :] = xin
    conv = xin * cw_ref[GDN_CONV - 1:GDN_CONV, :]
    for s in range(1, GDN_CONV):
        conv = conv + xpad_ref[8 - s:8 - s + tb, :] * cw_ref[GDN_CONV - 1 - s:GDN_CONV - s, :]
    xpad_ref[:8, :] = xin[tb - 8:, :]
    qkv = _silu(conv)

    beta = _sigmoid(b_in)
    g_col = -jnp.exp(alog_r_ref[...]) * _softplus(a_in + dt_r_ref[...])
    gc_col = _chunk_cumsum_rows(g_col, c)
    eye_h = (_iota((N_HEADS, N_HEADS), 0) == _iota((N_HEADS, N_HEADS), 1)).astype(F32)
    a_row = _mm32_nt(eye_h, a_in)
    g_row = -jnp.exp(alog_c_ref[...]) * _softplus(a_row + dt_c_ref[...])
    gc_row = _chunk_cumsum_cols(g_row, c)

    def per_head_cols(y):
        return jnp.stack([y[i * c:(i + 1) * c, h:h + 1] for i in range(nc) for h in range(N_HEADS)])

    def over_head_lanes(y):
        expand = (_iota((N_HEADS, WIDTH), 1) // HEAD_DIM == _iota((N_HEADS, WIDTH), 0)).astype(MXU_DTYPE)
        return _tree_sum([jnp.dot(part, expand, preferred_element_type=F32) for part in _split3(y)])

    gc = per_head_cols(gc_col)
    g_rows = jnp.stack([gc_row[h:h + 1, i * c:(i + 1) * c] for i in range(nc) for h in range(N_HEADS)])
    gc_d = over_head_lanes(gc_col)
    beta_d = over_head_lanes(beta)
    g_last_d = jnp.concatenate([jnp.broadcast_to(gc_d[(i + 1) * c - 1:(i + 1) * c, :], (c, WIDTH)) for i in range(nc)],
                               axis=0)

    seg = _head_segments(WIDTH)
    q = qkv[:, :WIDTH]
    k = qkv[:, WIDTH:2 * WIDTH]
    q = q * lax.rsqrt(_head_sums(q * q, seg) + 1e-6) * (HEAD_DIM ** -0.5)
    k = k * lax.rsqrt(_head_sums(k * k, seg) + 1e-6)
    kb_d = k * beta_d
    e_g = jnp.exp(gc_d)

    st = lambda y: _stack_heads(y, nc, c)
    q_h, k_h, kb = st(q), st(k), st(kb_d)

    ri = _iota((c, c), 0)
    ci = _iota((c, c), 1)
    incl = (ri >= ci)[None]
    strict = (ri > ci)[None]
    decay = jnp.exp(jnp.where(incl, gc - g_rows, NEG_INF))
    mm = _bmm_nt(jnp.concatenate([kb, q_h], axis=1), k_h)
    a_mat = jnp.where(strict, mm[:, :c] * decay, 0.0)
    qk = jnp.where(incl, mm[:, c:] * decay, 0.0)
    rhs = jnp.concatenate([st(qkv[:, 2 * WIDTH:] * beta_d), st(kb_d * e_g)], axis=-1)
    uw = _batched_unit_lower_solve(-a_mat, rhs)
    qkuw = _bmm(qk, uw)
    o0 = qkuw[:, :, :HEAD_DIM]
    oq = st(q * e_g) - qkuw[:, :, HEAD_DIM:]
    kuw = _bmm_tn(st(k * jnp.exp(g_last_d - gc_d)), uw)
    gamma = jnp.exp(gc[:, c - 1:c, :])
    o = _run_chunks(st_ref, -kuw[:, :, HEAD_DIM:], kuw[:, :, :HEAD_DIM], oq, o0, gamma, nc)

    o = _unstack_heads(o, nc)
    o = o * lax.rsqrt(_head_sums(o * o, seg) * (1.0 / HEAD_DIM) + RMS_EPS) * _lane_tile(nw_ref[...], N_HEADS)
    o_ref[0] = o * _silu(z)


def _gdn(p, conv_w, a_log, dt_bias, norm_w, tb=SCAN_BLOCK):
    b, t, cols = p.shape
    tb = min(tb, t)
    assert t % tb == 0 and tb % SCAN_CHUNK == 0 and cols == GDN_COLS
    params = [conv_w.astype(F32), a_log.reshape(1, -1), dt_bias.reshape(1, -1), a_log.reshape(-1, 1),
              dt_bias.reshape(-1, 1), norm_w.reshape(1, -1)]
    const = lambda i, j: (0, 0)
    return pl.pallas_call(
        _gdn_kernel,
        grid=(b, t // tb),
        in_specs=[pl.BlockSpec((1, tb, cols), lambda i, j: (i, j, 0))] + [pl.BlockSpec(q.shape, const) for q in params],
        out_specs=pl.BlockSpec((1, tb, WIDTH), lambda i, j: (i, j, 0)),
        out_shape=jax.ShapeDtypeStruct((b, t, WIDTH), F32),
        scratch_shapes=[pltpu.VMEM((8 + tb, 3 * WIDTH), F32), pltpu.VMEM((N_HEADS, HEAD_DIM, HEAD_DIM), F32)],
        compiler_params=pltpu.CompilerParams(dimension_semantics=("parallel", "arbitrary"),
                                             vmem_limit_bytes=VMEM_LIMIT),
        name="gdn",
    )(p, *params)


def _key_to_float(key):
    return pltpu.bitcast(jnp.where(key >= 0, key, key ^ jnp.int32(0x7FFFFFFF)), F32)


NEG_INF_KEY = -2139095041
COARSE_DTYPE = jnp.bfloat16
SOFTMAX_FLOOR = -1e30


def _dsa_kernel(dq_ref, iq_ref, kv_ref, ikw_ref, iwq_ref, qn_ref, kn_ref, lnw_ref, lnb_ref, o_ref,
                k_s, vt_s, ik_s, score_s, coarse_s, distm_s, *, top_k):
    qb = dq_ref.shape[1]
    t_len = kv_ref.shape[1]
    kc = DSA_KEY_CHUNK
    n_chunks = t_len // kc
    kvw = DSA_KV_HEADS * HEAD_DIM
    i = pl.program_id(1)
    n_live = (i * qb + qb + kc - 1) // kc

    @pl.when(i == 0)
    def _():
        kv = kv_ref[0]
        kraw = kv[:, :kvw]
        kn = kraw * lax.rsqrt(_head_sums(kraw * kraw, _head_segments(kvw)) * (1.0 / HEAD_DIM) + RMS_EPS)
        k_s[...] = (kn * _lane_tile(kn_ref[...], DSA_KV_HEADS)).astype(MXU_DTYPE)
        for c in range(n_chunks):
            vt_s[c] = kv[c * kc:(c + 1) * kc, kvw:].T.astype(MXU_DTYPE)
        ik = ikw_ref[0][:, :IDX_DIM]
        ikc = ik - jnp.mean(ik, axis=-1, keepdims=True)
        ik = ikc * lax.rsqrt(jnp.mean(ikc * ikc, axis=-1, keepdims=True) + 1e-6) * lnw_ref[...] + lnb_ref[...]
        hi, lo = _split_hi_lo(ik)
        ik_s[...] = jnp.concatenate([hi, hi, lo], axis=-1)
        score_s[...] = jnp.full(score_s.shape, NEG_INF, F32)

    t_row = i * qb + _iota((1, qb), 1)
    s_loc = _iota((kc, 1), 0)

    eye_h = (_iota((IDX_HEADS, IDX_HEADS), 0) == _iota((IDX_HEADS, IDX_HEADS), 1)).astype(F32)
    iw_t = _mm32_nt(eye_h, iwq_ref[0][:, IDX_DIM:IDX_DIM + IDX_HEADS]) * (IDX_HEADS ** -0.5 * IDX_DIM ** -0.5)
    iw_wide = jnp.concatenate([iw_t[h:h + 1, :] for h in range(IDX_HEADS)], axis=1)
    hi, lo = _split_hi_lo(iq_ref[0])
    iq_all = jnp.concatenate([jnp.concatenate([_head(hi, h), _head(lo, h), _head(hi, h)], axis=-1)
                              for h in range(IDX_HEADS)], axis=0)

    def score_chunk(c, carry):
        rows = pl.ds(pl.multiple_of(c * kc, kc), kc)
        dots = lax.dot_general(ik_s[rows, :], iq_all, (((1,), (1,)), ((), ())), preferred_element_type=F32)
        w = iw_wide * jnp.maximum(dots, 0.0)
        sc = _tree_sum([w[:, h * qb:(h + 1) * qb] for h in range(IDX_HEADS)])
        sc = jnp.where(c * kc + s_loc <= t_row, sc, NEG_INF)
        score_s[rows, :] = sc
        coarse_s[rows, :] = sc.astype(COARSE_DTYPE)
        return carry

    lax.fori_loop(0, n_live, score_chunk, 0)

    kf = float(top_k)
    grp = DSA_COUNT_ROWS
    n_grp = (i * qb + qb + grp - 1) // grp

    def count(pred):
        def body(g, acc):
            blk = score_s[pl.ds(pl.multiple_of(g * grp, grp), grp), :]
            ones = jnp.where(pred(blk), 1.0, 0.0)
            return acc + _tree_sum([ones[r * 8:(r + 1) * 8] for r in range(grp // 8)])
        acc = lax.fori_loop(0, n_grp, body, jnp.zeros((8, qb), F32))
        return jnp.sum(acc, axis=0, keepdims=True)

    def count_coarse(cand):
        def body(g, acc):
            blk = coarse_s[pl.ds(pl.multiple_of(g * grp, grp), grp), :]
            ones = jnp.where(blk >= cand, jnp.ones((), COARSE_DTYPE), jnp.zeros((), COARSE_DTYPE))
            return acc + _tree_sum([ones[r * 16:(r + 1) * 16] for r in range(grp // 16)]).astype(F32)
        acc = lax.fori_loop(0, n_grp, body, jnp.zeros((16, qb), F32))
        return jnp.sum(acc, axis=0, keepdims=True)

    def coarse_key(base):
        return jnp.where(base >= 0, base, base | jnp.int32(0xFFFF))

    int_min = jnp.int32(-2 ** 31)
    base = jnp.where(count_coarse(jnp.zeros((1, qb), COARSE_DTYPE)) >= kf, jnp.int32(0), int_min)

    def coarse_step(b, base):
        cand = base | (jnp.int32(1) << (30 - b))
        cand_f = _key_to_float(coarse_key(cand)).astype(COARSE_DTYPE)
        return jnp.where(count_coarse(cand_f) >= kf, cand, base)

    base = lax.fori_loop(0, 15, coarse_step, base)
    lo_key = coarse_key(base) - jnp.int32(2 ** 15)

    def fine_step(b, off):
        cand_off = off | (jnp.int32(1) << (16 - b))
        cand_f = _key_to_float(lo_key + cand_off)
        return jnp.where(count(lambda blk: blk >= cand_f) >= kf, cand_off, off)

    tau = lo_key + lax.fori_loop(0, 17, fine_step, jnp.zeros((1, qb), jnp.int32))
    tau_f = _key_to_float(jnp.maximum(tau, jnp.int32(NEG_INF_KEY)))

    n_ge = count(lambda blk: blk >= tau_f)
    no_partial_tie = jnp.min(jnp.where((n_ge <= kf) | (tau_f == NEG_INF), 1.0, 0.0)) > 0.5

    def masked_distance(c, sel_fn):
        rows = pl.ds(pl.multiple_of(c * kc, kc), kc)
        s_pos = c * kc + s_loc
        sel = sel_fn(score_s[rows, :]) & (s_pos <= t_row)
        distm_s[rows, :] = jnp.where(sel, (t_row - s_pos).astype(F32), jnp.inf)

    @pl.when(no_partial_tie)
    def _():
        def select_chunk(c, carry):
            masked_distance(c, lambda blk: blk >= tau_f)
            return carry
        lax.fori_loop(0, n_live, select_chunk, 0)

    @pl.when(jnp.logical_not(no_partial_tie))
    def _():
        lower = (_iota((kc, kc), 0) >= _iota((kc, kc), 1)).astype(MXU_DTYPE)
        need = kf - count(lambda blk: blk > tau_f)

        def select_chunk(c, run):
            def sel_fn(blk):
                eq = blk == tau_f
                pref = run + jnp.dot(lower, jnp.where(eq, 1.0, 0.0).astype(MXU_DTYPE), preferred_element_type=F32)
                return (blk > tau_f) | (eq & (pref <= need))
            masked_distance(c, sel_fn)
            rows = pl.ds(pl.multiple_of(c * kc, kc), kc)
            return run + jnp.sum(jnp.where(score_s[rows, :] == tau_f, 1.0, 0.0), axis=0, keepdims=True)
        lax.fori_loop(0, n_live, select_chunk, jnp.zeros((1, qb), F32))

    dq = dq_ref[0]
    dq = dq * lax.rsqrt(_head_sums(dq * dq, _head_segments(WIDTH)) * (1.0 / HEAD_DIM) + RMS_EPS)
    dq = (dq * _lane_tile(qn_ref[...] * (HEAD_DIM ** -0.5 * LOG2_E), N_HEADS)).astype(MXU_DTYPE)
    qs = [_head(dq, h) for h in range(N_HEADS)]
    q_grp = [jnp.concatenate(qs[g * DSA_GROUP:(g + 1) * DSA_GROUP], axis=0) for g in range(DSA_KV_HEADS)]
    slopes = jnp.concatenate([jnp.full((1, qb), DSA_SLOPES[h] * LOG2_E, F32) for h in range(N_HEADS)], axis=1)
    gw = DSA_GROUP * qb

    def attend_chunk(c, carry):
        m, l, acc = carry
        rows = pl.ds(pl.multiple_of(c * kc, kc), kc)
        kch = k_s[rows, :]
        vt = vt_s[c]
        sc = jnp.concatenate(
            [lax.dot_general(kch[:, g * HEAD_DIM:(g + 1) * HEAD_DIM], q_grp[g], (((1,), (1,)), ((), ())),
                             preferred_element_type=F32) for g in range(DSA_KV_HEADS)], axis=1)
        sc = sc - slopes * _lane_tile(distm_s[rows, :], N_HEADS)
        m_new = jnp.maximum(m, jnp.max(sc, axis=0, keepdims=True))
        alpha = jnp.exp2(m - m_new)
        e = jnp.exp2(sc - m_new)
        l = alpha * l + jnp.sum(e, axis=0, keepdims=True)
        eb = e.astype(MXU_DTYPE)
        pv = jnp.concatenate(
            [jnp.dot(vt[g * HEAD_DIM:(g + 1) * HEAD_DIM, :], eb[:, g * gw:(g + 1) * gw], preferred_element_type=F32)
             for g in range(DSA_KV_HEADS)], axis=1)
        return m_new, l, alpha * acc + pv

    init = (jnp.full((1, N_HEADS * qb), SOFTMAX_FLOOR, F32), jnp.zeros((1, N_HEADS * qb), F32),
            jnp.zeros((HEAD_DIM, N_HEADS * qb), F32))
    _, l, acc = lax.fori_loop(0, n_live, attend_chunk, init)
    o_t = jnp.concatenate([acc[:, h * qb:(h + 1) * qb] / l[:, h * qb:(h + 1) * qb] for h in range(N_HEADS)], axis=0)
    o_ref[0] = o_t.T


def _dsa(p, q_norm, k_norm, ln_w, ln_b):
    b, t, cols = p.shape
    qb = DSA_Q_BLOCK
    kc = DSA_KEY_CHUNK
    assert t % qb == 0 and cols == DSA_COLS and t % DSA_COUNT_ROWS == 0 and t % kc == 0 and kc % DSA_COUNT_ROWS == 0
    top_k = min(DSA_TOPK_MAX, t // 4)
    kvw = 2 * DSA_KV_HEADS * HEAD_DIM
    row = lambda v: v.reshape(1, -1).astype(F32)
    params = [row(q_norm), row(k_norm), row(ln_w), row(ln_b)]
    const = lambda i, j: (0, 0)
    return pl.pallas_call(
        functools.partial(_dsa_kernel, top_k=top_k),
        grid=(b, t // qb),
        in_specs=[pl.BlockSpec((1, qb, WIDTH), lambda i, j: (i, j, 0)),
                  pl.BlockSpec((1, qb, WIDTH), lambda i, j: (i, j, 1)),
                  pl.BlockSpec((1, t, kvw), lambda i, j: (i, 0, 2 * WIDTH // kvw)),
                  pl.BlockSpec((1, t, LANES), lambda i, j: (i, 0, (2 * WIDTH + kvw) // LANES)),
                  pl.BlockSpec((1, qb, LANES), lambda i, j: (i, j, (2 * WIDTH + kvw) // LANES))]
        + [pl.BlockSpec(q.shape, const) for q in params],
        out_specs=pl.BlockSpec((1, qb, WIDTH), lambda i, j: (i, j, 0)),
        out_shape=jax.ShapeDtypeStruct((b, t, WIDTH), F32),
        scratch_shapes=[pltpu.VMEM((t, DSA_KV_HEADS * HEAD_DIM), MXU_DTYPE),
                        pltpu.VMEM((t // kc, DSA_KV_HEADS * HEAD_DIM, kc), MXU_DTYPE),
                        pltpu.VMEM((t, 3 * IDX_DIM), MXU_DTYPE),
                        pltpu.VMEM((t, qb), F32),
                        pltpu.VMEM((t, qb), COARSE_DTYPE),
                        pltpu.VMEM((t, qb), F32)],
        compiler_params=pltpu.CompilerParams(dimension_semantics=("parallel", "arbitrary"),
                                             vmem_limit_bytes=VMEM_LIMIT),
        name="dsa",
    )(p, p, p, p, p, *params)


MOBA_PENALTY = 1e30


def _moba_kernel(q_ref, k_ref, v_ref, qn_ref, kn_ref, o_ref, k_s, vt_s, km_s, causal_s):
    bs = q_ref.shape[1]
    t_len = k_ref.shape[1]
    n_kb = t_len // bs
    n_pad = -(-n_kb // 8) * 8
    i = pl.program_id(1)

    @pl.when(i == 0)
    def _():
        kraw = k_ref[0]
        kn = kraw * lax.rsqrt(_head_sums(kraw * kraw, _head_segments(WIDTH)) * (1.0 / HEAD_DIM) + RMS_EPS)
        kn = kn * _lane_tile(kn_ref[...], N_HEADS)
        s_pos = _iota((t_len, 1), 0)
        fcol = _iota((1, HEAD_DIM), 1)
        extra = jnp.where(fcol == n_pad, (s_pos % bs).astype(F32),
                          jnp.where((fcol == n_pad + 1) | (fcol == s_pos // bs), 1.0, 0.0)).astype(MXU_DTYPE)
        for h in range(N_HEADS):
            k_s[:, 2 * h * HEAD_DIM:(2 * h + 1) * HEAD_DIM] = _head(kn, h).astype(MXU_DTYPE)
            k_s[:, (2 * h + 1) * HEAD_DIM:(2 * h + 2) * HEAD_DIM] = extra
        means = jnp.concatenate([jnp.mean(kn[j * bs:(j + 1) * bs], axis=0, keepdims=True) for j in range(n_kb)]
                                + [jnp.zeros((n_pad - n_kb, WIDTH), F32)] * (n_pad > n_kb), axis=0)
        head_of_lane = _iota((1, WIDTH), 1) // HEAD_DIM
        rows = [jnp.where(head_of_lane == h, means, 0.0) for h in range(N_HEADS)]
        rows.append(jnp.zeros((LANES - N_HEADS * n_pad, WIDTH), F32))
        table = jnp.concatenate(rows, axis=0).T
        hi, lo = _split_hi_lo(table)
        km_s[...] = jnp.concatenate([hi, lo, hi], axis=0)
        v = v_ref[0]
        for j in range(n_kb):
            vt_s[j] = v[j * bs:(j + 1) * bs, :].T.astype(MXU_DTYPE)
        above = _iota((bs, 1), 0) > _iota((1, bs), 1)
        causal_s[...] = _lane_tile(jnp.where(above, NEG_INF, 0.0), N_HEADS)

    qraw = q_ref[0]
    jcol = _iota((n_pad, 1), 0)
    own = pl.multiple_of(i * bs, bs)
    qn = qraw * lax.rsqrt(_head_sums(qraw * qraw, _head_segments(WIDTH)) * (1.0 / HEAD_DIM) + RMS_EPS)
    qn = qn * _lane_tile(qn_ref[...], N_HEADS)
    q_hi, q_lo = _split_hi_lo(qn)
    gate_qm = jnp.dot(jnp.concatenate([q_hi, q_hi, q_lo], axis=1), km_s[...], preferred_element_type=F32)
    gate_rows = gate_qm.T
    gate = jnp.concatenate([gate_rows[h * n_pad:(h + 1) * n_pad] for h in range(N_HEADS)], axis=1)
    gate = jnp.where(jcol < i, gate, NEG_INF)
    rank = jnp.zeros(gate.shape, F32)
    for j2 in range(n_kb):
        other = gate[j2:j2 + 1, :]
        beats = (other > gate) | ((other == gate) & (j2 < jcol))
        rank = rank + jnp.where(beats, 1.0, 0.0)
    picked = (rank < float(MOBA_TOPK)) & (jcol < i)
    slopes = jnp.concatenate([jnp.full((1, bs), MOBA_SLOPES[h], F32) for h in range(N_HEADS)], axis=1)
    block_term = jnp.where(jcol < i, -(slopes * ((i - jcol) * bs).astype(F32) + jnp.where(picked, 0.0, MOBA_PENALTY)),
                           0.0)
    q_t = (qn * (HEAD_DIM ** -0.5)).T
    frow = _iota((HEAD_DIM - n_pad, 1), 0)
    t_loc = _iota((1, bs), 1).astype(F32)
    q_aug = []
    for h in range(N_HEADS):
        tail = jnp.where(frow == 0, MOBA_SLOPES[h], jnp.where(frow == 1, -MOBA_SLOPES[h] * t_loc, 0.0))
        q_aug.append(jnp.concatenate([q_t[h * HEAD_DIM:(h + 1) * HEAD_DIM], block_term[:, h * bs:(h + 1) * bs], tail],
                                     axis=0).astype(MXU_DTYPE))

    def scores(start):
        return jnp.concatenate(
            [jnp.dot(k_s[pl.ds(start, bs), 2 * h * HEAD_DIM:(2 * h + 2) * HEAD_DIM], q_aug[h],
                     preferred_element_type=F32) for h in range(N_HEADS)], axis=1)

    def weighted_values(j, e):
        vt = vt_s[j]
        eb = e.astype(MXU_DTYPE)
        return jnp.concatenate(
            [jnp.dot(vt[h * HEAD_DIM:(h + 1) * HEAD_DIM, :], eb[:, h * bs:(h + 1) * bs], preferred_element_type=F32)
             for h in range(N_HEADS)], axis=1)

    sc = scores(own) + causal_s[...]
    m0 = jnp.max(sc, axis=0, keepdims=True)
    e = jnp.exp(sc - m0)
    l0 = jnp.sum(e, axis=0, keepdims=True)
    acc0 = weighted_values(i, e)

    def block_step(j, carry):
        m, l, acc = carry
        start = pl.multiple_of(j * bs, bs)
        again = pl.multiple_of(start + jnp.minimum(i, 0) * bs, bs)
        m_new = jnp.maximum(m, jnp.max(scores(start), axis=0, keepdims=True))
        alpha = jnp.exp(m - m_new)
        e2 = jnp.exp(scores(again) - m_new)
        l = alpha * l + jnp.sum(e2, axis=0, keepdims=True)
        return m_new, l, alpha * acc + weighted_values(j, e2)

    _, l, acc = lax.fori_loop(0, i, block_step, (m0, l0, acc0))
    o_t = jnp.concatenate([acc[:, h * bs:(h + 1) * bs] / l[:, h * bs:(h + 1) * bs] for h in range(N_HEADS)], axis=0)
    o_ref[0] = o_t.T


def _moba(p, q_norm, k_norm):
    b, t, cols = p.shape
    bs = MOBA_BLOCK
    n_pad = -(-(t // bs) // 8) * 8
    assert t % bs == 0 and cols == MOBA_COLS and n_pad + 2 <= HEAD_DIM and N_HEADS * n_pad <= LANES and bs <= 256
    row = lambda v: v.reshape(1, -1).astype(F32)
    const = lambda i, j: (0, 0)
    return pl.pallas_call(
        _moba_kernel,
        grid=(b, t // bs),
        in_specs=[pl.BlockSpec((1, bs, WIDTH), lambda i, j: (i, j, 0)),
                  pl.BlockSpec((1, t, WIDTH), lambda i, j: (i, 0, 1)),
                  pl.BlockSpec((1, t, WIDTH), lambda i, j: (i, 0, 2)),
                  pl.BlockSpec((1, HEAD_DIM), const), pl.BlockSpec((1, HEAD_DIM), const)],
        out_specs=pl.BlockSpec((1, bs, WIDTH), lambda i, j: (i, j, 0)),
        out_shape=jax.ShapeDtypeStruct((b, t, WIDTH), F32),
        scratch_shapes=[pltpu.VMEM((t, 2 * WIDTH), MXU_DTYPE),
                        pltpu.VMEM((t // bs, WIDTH, bs), MXU_DTYPE),
                        pltpu.VMEM((3 * WIDTH, LANES), MXU_DTYPE),
                        pltpu.VMEM((bs, N_HEADS * bs), F32)],
        compiler_params=pltpu.CompilerParams(dimension_semantics=("parallel", "arbitrary"),
                                             vmem_limit_bytes=VMEM_LIMIT),
        name="moba",
    )(p, p, p, row(q_norm), row(k_norm))


def _pad_cols(w, n):
    return jnp.pad(w, ((0, 0), (0, n - w.shape[1])))


def _even_layer(x, b, t, norm_g, norm2_g, mlp_w1, mlp_w2, w_in, w_out, mu, w0, w2, a0, a2, g2, k_k, k_a, r_k,
                lnx_w, lnx_b, conv_w, a_log, dt_bias, gdn_norm_w):
    qkv_w = 3 * WIDTH
    w_rwkv = w_in[:, :RWKV_COLS]
    w_g = w_in[:, RWKV_COLS:]
    w_gdn = jnp.concatenate([w_g[:, :qkv_w], w_g[:, qkv_w + 2 * N_HEADS:],
                             _pad_cols(w_g[:, qkv_w:qkv_w + 2 * N_HEADS], LANES)], axis=1)
    p_rwkv, p_gdn = _norm_proj(x, norm_g, [w_rwkv.astype(MXU_DTYPE), w_gdn.astype(MXU_DTYPE)])
    o_a = _rwkv(p_rwkv.reshape(b, t, -1), mu, w0, w2, a0, a2, g2, k_k, k_a, r_k, lnx_w, lnx_b)
    o_b = _gdn(p_gdn.reshape(b, t, -1), conv_w, a_log, dt_bias, gdn_norm_w)
    return _out_mlp(x, o_a.reshape(b * t, -1), o_b.reshape(b * t, -1), w_out, norm2_g, mlp_w1, mlp_w2)


def _odd_layer(x, b, t, norm_g, norm2_g, mlp_w1, mlp_w2, w_in, w_out, dsa_q_norm, dsa_k_norm, idx_ln_w, idx_ln_b,
               moba_q_norm, moba_k_norm):
    kvw = 2 * DSA_KV_HEADS * HEAD_DIM
    sizes = (WIDTH, kvw, IDX_HEADS * IDX_DIM, IDX_DIM, IDX_HEADS, WIDTH, WIDTH, WIDTH)
    offs = [0]
    for s in sizes:
        offs.append(offs[-1] + s)
    dq, dkv, iq, ik, iw, mq, mk, mv = (w_in[:, offs[n]:offs[n + 1]] for n in range(len(sizes)))
    w_dsa = jnp.concatenate([dq, iq, dkv, _pad_cols(jnp.concatenate([ik, iw], axis=1), LANES)], axis=1)
    w_moba = jnp.concatenate([mq, mk, mv], axis=1)
    p_dsa, p_moba = _norm_proj(x, norm_g, [w_dsa.astype(MXU_DTYPE), w_moba.astype(MXU_DTYPE)])
    o_c = _dsa(p_dsa.reshape(b, t, -1), dsa_q_norm, dsa_k_norm, idx_ln_w, idx_ln_b)
    o_d = _moba(p_moba.reshape(b, t, -1), moba_q_norm, moba_k_norm)
    return _out_mlp(x, o_c.reshape(b * t, -1), o_d.reshape(b * t, -1), w_out, norm2_g, mlp_w1, mlp_w2)


def kernel(x, norm1_g, norm2_g, mlp_w1, mlp_w2, ev_w_in, ev_w_out, rwkv_mu, rwkv_w0, rwkv_w2, rwkv_a0, rwkv_a2, rwkv_g2, rwkv_k_k, rwkv_k_a, rwkv_r_k, rwkv_lnx_w, rwkv_lnx_b, gdn_conv_w, gdn_a_log, gdn_dt_bias, gdn_norm_w, od_w_in, od_w_out, dsa_q_norm, dsa_k_norm, idx_k_ln_w, idx_k_ln_b, moba_q_norm, moba_k_norm):
    b, t, d = x.shape
    depth = norm1_g.shape[0]
    h = x.reshape(b * t, d)
    for i in range(depth):
        j = i // 2
        if i % 2 == 0:
            h = _even_layer(h, b, t, norm1_g[i], norm2_g[i], mlp_w1[i], mlp_w2[i], ev_w_in[j], ev_w_out[j],
                            rwkv_mu[j], rwkv_w0[j], rwkv_w2[j], rwkv_a0[j], rwkv_a2[j], rwkv_g2[j], rwkv_k_k[j], rwkv_k_a[j], rwkv_r_k[j],
                            rwkv_lnx_w[j], rwkv_lnx_b[j], gdn_conv_w[j], gdn_a_log[j], gdn_dt_bias[j],
                            gdn_norm_w[j])
        else:
            h = _odd_layer(h, b, t, norm1_g[i], norm2_g[i], mlp_w1[i], mlp_w2[i], od_w_in[j], od_w_out[j],
                           dsa_q_norm[j], dsa_k_norm[j], idx_k_ln_w[j], idx_k_ln_b[j], moba_q_norm[j], moba_k_norm[j])
    return h.reshape(b, t, d)
```

```python
import functools
import math

import jax
import jax.numpy as jnp
from jax import lax
from jax.experimental import pallas as pl
from jax.experimental.pallas import tpu as pltpu

F32 = jnp.float32
MXU_DTYPE = jnp.bfloat16
HIGHEST = lax.Precision.HIGHEST

LANES = 128
VMEM_LIMIT = 56 * 1024 * 1024

HEAD_DIM = 64
N_HEADS = 8
WIDTH = N_HEADS * HEAD_DIM
RMS_EPS = 1e-6
VAL_ROWS = HEAD_DIM + 8

RWKV_DECAY_LORA = 64
RWKV_A_LORA = 64
RWKV_GATE_LORA = 128
RWKV_COLS = 3 * WIDTH + RWKV_DECAY_LORA + RWKV_A_LORA + RWKV_GATE_LORA
RWKV_GN_EPS = 6.4e-4

GDN_CONV = 4
GDN_COLS = 3 * WIDTH + WIDTH + LANES
SCAN_CHUNK = 64
SCAN_BLOCK = 256

DSA_KV_HEADS = 2
DSA_GROUP = N_HEADS // DSA_KV_HEADS
IDX_HEADS = 8
IDX_DIM = 64
DSA_TOPK_MAX = 256
DSA_Q_BLOCK = 256
DSA_KEY_CHUNK = 512
DSA_COUNT_ROWS = 256
DSA_COLS = 2 * WIDTH + 2 * DSA_KV_HEADS * HEAD_DIM + LANES

MOBA_BLOCK = 256
MOBA_TOPK = 3
MOBA_COLS = 3 * WIDTH

PROJ_ROWS = 512
MLP_ROWS = 1024
MLP_FF_COLS = 1024

ALIBI_HEADS = 2 * N_HEADS
NEG_INF = float("-inf")
LOG2_E = math.log2(math.e)


def _alibi_slope(i):
    return 2.0 ** (-8.0 * (i + 1) / ALIBI_HEADS)


DSA_SLOPES = tuple(_alibi_slope(2 * h) for h in range(N_HEADS))
MOBA_SLOPES = tuple(_alibi_slope(2 * h + 1) for h in range(N_HEADS))


def _mm(a, b):
    return jnp.dot(a.astype(MXU_DTYPE), b.astype(MXU_DTYPE), preferred_element_type=F32)


def _mm32_nt(a, b):
    return lax.dot_general(a, b, (((1,), (1,)), ((), ())), preferred_element_type=F32, precision=HIGHEST)


def _iota(shape, dim):
    return lax.broadcasted_iota(jnp.int32, shape, dim)


def _sigmoid(x):
    return 1.0 / (1.0 + jnp.exp(-x))


def _silu(x):
    return x * _sigmoid(x)


def _softplus(x):
    return jnp.maximum(x, 0.0) + jnp.log1p(jnp.exp(-jnp.abs(x)))


def _head(x, h):
    return x[:, h * HEAD_DIM:(h + 1) * HEAD_DIM]


def _head_rows(x, h):
    return x[h * HEAD_DIM:(h + 1) * HEAD_DIM]


def _split_hi_lo(x):
    hi = x.astype(MXU_DTYPE)
    lo = (x - hi.astype(F32)).astype(MXU_DTYPE)
    return hi, lo


def _head_segments(width):
    return (_iota((width, width), 0) // HEAD_DIM == _iota((width, width), 1) // HEAD_DIM).astype(MXU_DTYPE)


def _head_sums(x, seg):
    hi, lo = _split_hi_lo(x)
    return jnp.dot(hi, seg, preferred_element_type=F32) + jnp.dot(lo, seg, preferred_element_type=F32)


def _tree_sum(parts):
    parts = list(parts)
    while len(parts) > 1:
        nxt = [parts[a] + parts[a + 1] for a in range(0, len(parts) - 1, 2)]
        if len(parts) % 2:
            nxt.append(parts[-1])
        parts = nxt
    return parts[0]


def _lane_tile(x, n):
    return jnp.concatenate([x] * n, axis=1)


def _norm_proj_kernel(x_ref, g_ref, *refs):
    n = len(refs) // 2
    x = x_ref[...]
    h = x * lax.rsqrt(jnp.mean(x * x, axis=-1, keepdims=True) + RMS_EPS) * g_ref[...]
    h = h.astype(MXU_DTYPE)
    for w_ref, o_ref in zip(refs[:n], refs[n:]):
        o_ref[...] = jnp.dot(h, w_ref[...], preferred_element_type=F32)


def _norm_proj(x, g, ws, tm=PROJ_ROWS):
    n, d = x.shape
    assert n % tm == 0
    const = lambda i: (0, 0)
    return pl.pallas_call(
        _norm_proj_kernel,
        grid=(n // tm,),
        in_specs=[pl.BlockSpec((tm, d), lambda i: (i, 0)), pl.BlockSpec((1, d), const)]
        + [pl.BlockSpec(w.shape, const) for w in ws],
        out_specs=[pl.BlockSpec((tm, w.shape[1]), lambda i: (i, 0)) for w in ws],
        out_shape=[jax.ShapeDtypeStruct((n, w.shape[1]), F32) for w in ws],
        compiler_params=pltpu.CompilerParams(dimension_semantics=("parallel",), vmem_limit_bytes=VMEM_LIMIT),
        name="norm_proj",
    )(x, g.reshape(1, d), *ws)


def _out_mlp_kernel(x_ref, a_ref, b_ref, wa_ref, wb_ref, g_ref, w1_ref, w2_ref, o_ref, h_ref, acc_ref):
    j = pl.program_id(1)

    @pl.when(j == 0)
    def _():
        x = x_ref[...] + _mm(a_ref[...], wa_ref[...]) + _mm(b_ref[...], wb_ref[...])
        h = x * lax.rsqrt(jnp.mean(x * x, axis=-1, keepdims=True) + RMS_EPS) * g_ref[...]
        h_ref[...] = h.astype(MXU_DTYPE)
        acc_ref[...] = x

    u = jnp.maximum(jnp.dot(h_ref[...], w1_ref[...], preferred_element_type=F32), 0.0)
    acc_ref[...] += jnp.dot((u * u).astype(MXU_DTYPE), w2_ref[...], preferred_element_type=F32)

    @pl.when(j == pl.num_programs(1) - 1)
    def _():
        o_ref[...] = acc_ref[...]


def _out_mlp(x, a, b, w_out, g, w1, w2, tm=MLP_ROWS, tf=MLP_FF_COLS):
    n, d = x.shape
    f = w1.shape[1]
    tm = min(tm, n)
    tf = min(tf, f)
    assert n % tm == 0 and f % tf == 0
    wa = w_out[:a.shape[1]].astype(MXU_DTYPE)
    wb = w_out[a.shape[1]:].astype(MXU_DTYPE)
    row = lambda i, j: (i, 0)
    const = lambda i, j: (0, 0)
    return pl.pallas_call(
        _out_mlp_kernel,
        grid=(n // tm, f // tf),
        in_specs=[pl.BlockSpec((tm, d), row), pl.BlockSpec((tm, a.shape[1]), row), pl.BlockSpec((tm, b.shape[1]), row),
                  pl.BlockSpec(wa.shape, const), pl.BlockSpec(wb.shape, const), pl.BlockSpec((1, d), const),
                  pl.BlockSpec((d, tf), lambda i, j: (0, j)), pl.BlockSpec((tf, d), lambda i, j: (j, 0))],
        out_specs=pl.BlockSpec((tm, d), row),
        out_shape=jax.ShapeDtypeStruct((n, d), F32),
        scratch_shapes=[pltpu.VMEM((tm, d), MXU_DTYPE), pltpu.VMEM((tm, d), F32)],
        compiler_params=pltpu.CompilerParams(dimension_semantics=("parallel", "arbitrary"),
                                             vmem_limit_bytes=VMEM_LIMIT),
        name="out_mlp",
    )(x, a, b, wa, wb, g.reshape(1, d), w1.astype(MXU_DTYPE), w2.astype(MXU_DTYPE))


def _bmm(a, b):
    return lax.dot_general(a.astype(MXU_DTYPE), b.astype(MXU_DTYPE), (((2,), (1,)), ((0,), (0,))),
                           preferred_element_type=F32)


def _bmm_nt(a, b):
    return lax.dot_general(a.astype(MXU_DTYPE), b.astype(MXU_DTYPE), (((2,), (2,)), ((0,), (0,))),
                           preferred_element_type=F32)


def _bmm_tn(a, b):
    return lax.dot_general(a.astype(MXU_DTYPE), b.astype(MXU_DTYPE), (((1,), (1,)), ((0,), (0,))),
                           preferred_element_type=F32)


def _stack_heads(x, nc, c):
    return jnp.stack([x[i * c:(i + 1) * c, h * HEAD_DIM:(h + 1) * HEAD_DIM] for i in range(nc) for h in range(N_HEADS)])


def _unstack_heads(y, nc):
    return jnp.concatenate([jnp.concatenate([y[i * N_HEADS + h] for h in range(N_HEADS)], axis=-1)
                            for i in range(nc)], axis=0)


def _batched_unit_lower_solve(x, y):
    c = x.shape[1]
    y = y + _bmm(x, y)
    p = 2
    while p < c:
        x = _bmm(x, x)
        y = y + _bmm(x, y)
        p *= 2
    return y


def _split3(x):
    hi = x.astype(MXU_DTYPE)
    r = x - hi.astype(F32)
    mid = r.astype(MXU_DTYPE)
    lo = (r - mid.astype(F32)).astype(MXU_DTYPE)
    return hi, mid, lo


def _chunk_cumsum_rows(x, c):
    tri = (_iota((c, c), 0) >= _iota((c, c), 1)).astype(MXU_DTYPE)
    parts = _split3(x)
    return jnp.concatenate(
        [_tree_sum([jnp.dot(tri, p[i * c:(i + 1) * c], preferred_element_type=F32) for p in parts])
         for i in range(x.shape[0] // c)], axis=0)


def _chunk_cumsum_cols(x, c):
    tri = (_iota((c, c), 0) <= _iota((c, c), 1)).astype(MXU_DTYPE)
    parts = _split3(x)
    return jnp.concatenate(
        [_tree_sum([jnp.dot(p[:, i * c:(i + 1) * c], tri, preferred_element_type=F32) for p in parts])
         for i in range(x.shape[1] // c)], axis=1)


def _run_chunks(st_ref, m, q, oq, o0, gamma, nc):
    s = st_ref[...]
    outs = []
    for i in range(nc):
        g = slice(i * N_HEADS, (i + 1) * N_HEADS)
        mo = _bmm(jnp.concatenate([m[g], oq[g]], axis=1), s)
        outs.append(mo[:, HEAD_DIM:] + o0[g])
        s = gamma[g] * s + mo[:, :HEAD_DIM] + q[g]
    st_ref[...] = s
    return jnp.concatenate(outs, axis=0)


def _rwkv_kernel(p_ref, mu_ref, w0_ref, w2_ref, a0_ref, a2_ref, g2_ref, kk_ref, ka_ref, rk_ref, lnw_ref, lnb_ref,
                 o_ref, prev_ref, st_ref):
    tb = p_ref.shape[1]
    c = SCAN_CHUNK
    nc = tb // c

    @pl.when(pl.program_id(1) == 0)
    def _():
        prev_ref[...] = jnp.zeros_like(prev_ref)
        st_ref[...] = jnp.zeros_like(st_ref)

    p = p_ref[0]
    shifted = jnp.where(_iota((tb, 1), 0) == 0, prev_ref[...], pltpu.roll(p, 1, 0))
    prev_ref[...] = p[tb - 1:tb, :]
    x = p + (shifted - p) * mu_ref[...]

    r = x[:, 0:WIDTH]
    k = x[:, WIDTH:2 * WIDTH]
    v = x[:, 2 * WIDTH:3 * WIDTH]
    off = 3 * WIDTH
    wd = x[:, off:off + RWKV_DECAY_LORA]
    ad = x[:, off + RWKV_DECAY_LORA:off + RWKV_DECAY_LORA + RWKV_A_LORA]
    gd = x[:, off + RWKV_DECAY_LORA + RWKV_A_LORA:]

    w_pre = w0_ref[...] + _mm(jnp.tanh(wd), w2_ref[...])
    log_w = -(_sigmoid(w_pre) * math.exp(-0.5))
    a = _sigmoid(a0_ref[...] + _mm(ad, a2_ref[...]))
    g = _mm(_sigmoid(gd), g2_ref[...])
    kk_all = k * kk_ref[...]
    k = k * (1.0 + (a - 1.0) * ka_ref[...])

    gam = _chunk_cumsum_rows(log_w, c)
    gam_last = jnp.concatenate([jnp.broadcast_to(gam[(i + 1) * c - 1:(i + 1) * c, :], (c, WIDTH)) for i in range(nc)],
                               axis=0)

    seg = _head_segments(WIDTH)
    kk = kk_all * lax.rsqrt(_head_sums(kk_all * kk_all, seg) + 1e-6)
    b = kk * a
    e_neg = jnp.exp(-gam)
    tail = jnp.exp(gam_last - gam)

    st = lambda y: _stack_heads(y, nc, c)
    st_mxu = lambda y: st(y.astype(MXU_DTYPE))
    v_h = st_mxu(v)
    a_t = st(-kk * jnp.exp(gam - log_w))
    r_t = st(r * jnp.exp(gam))
    b_t = st_mxu(b * e_neg)
    k_t = st_mxu(k * e_neg)
    e_last = st(jnp.exp(gam_last))[:, 0:1, :]

    eye = (_iota((c, c), 0) == _iota((c, c), 1))[None]

    inter = _bmm_nt(jnp.concatenate([a_t, r_t], axis=1), jnp.concatenate([b_t, k_t], axis=1))
    row2 = _iota((c, 2 * c), 0)
    col2 = _iota((c, 2 * c), 1)
    col2 = jnp.where(col2 >= c, col2 - c, col2)
    top = jnp.where((row2 > col2)[None], inter[:, :c], 0.0)
    bot = jnp.where((row2 >= col2)[None], inter[:, c:], 0.0)
    akv = _bmm(top[:, :, c:], v_h)
    wu = _batched_unit_lower_solve(top[:, :, :c], jnp.concatenate([a_t, akv], axis=-1))
    wu_v = jnp.concatenate([wu.astype(MXU_DTYPE), jnp.concatenate([jnp.zeros_like(v_h), v_h], axis=-1)], axis=1)
    ro = _bmm(bot, wu_v)
    oq = r_t + ro[:, :, :HEAD_DIM]
    o0 = ro[:, :, HEAD_DIM:]
    mq = _bmm_tn(jnp.concatenate([st_mxu(b * tail), st_mxu(k * tail)], axis=1), wu_v)
    m = mq[:, :, :HEAD_DIM]
    q = mq[:, :, HEAD_DIM:]
    gamma = jnp.sum(jnp.where(eye, e_last, 0.0), axis=2, keepdims=True)

    o = _unstack_heads(_run_chunks(st_ref, m, q, oq, o0, gamma, nc), nc)

    oc = o - _head_sums(o, seg) * (1.0 / HEAD_DIM)
    on = oc * lax.rsqrt(_head_sums(oc * oc, seg) * (1.0 / HEAD_DIM) + RWKV_GN_EPS)
    bonus = _head_sums(r * k * rk_ref[...], seg) * v
    o_ref[0] = (on * lnw_ref[...] + lnb_ref[...] + bonus) * g


def _rwkv(p, mu, w0, w2, a0, a2, g2, k_k, k_a, r_k, lnx_w, lnx_b, tb=SCAN_BLOCK):
    b, t, cols = p.shape
    tb = min(tb, t)
    assert t % tb == 0 and tb % SCAN_CHUNK == 0 and cols == RWKV_COLS
    row = lambda v: v.reshape(1, -1).astype(F32)
    params = [row(mu), row(w0), w2.astype(MXU_DTYPE), row(a0), a2.astype(MXU_DTYPE), g2.astype(MXU_DTYPE),
              row(k_k), row(k_a), row(r_k), row(lnx_w), row(lnx_b)]
    const = lambda i, j: (0, 0)
    return pl.pallas_call(
        _rwkv_kernel,
        grid=(b, t // tb),
        in_specs=[pl.BlockSpec((1, tb, cols), lambda i, j: (i, j, 0))] + [pl.BlockSpec(q.shape, const) for q in params],
        out_specs=pl.BlockSpec((1, tb, WIDTH), lambda i, j: (i, j, 0)),
        out_shape=jax.ShapeDtypeStruct((b, t, WIDTH), F32),
        scratch_shapes=[pltpu.VMEM((1, cols), F32), pltpu.VMEM((N_HEADS, HEAD_DIM, HEAD_DIM), F32)],
        compiler_params=pltpu.CompilerParams(dimension_semantics=("parallel", "arbitrary"),
                                             vmem_limit_bytes=VMEM_LIMIT),
        name="rwkv7",
    )(p, *params)


def _gdn_kernel(p_ref, cw_ref, alog_r_ref, dt_r_ref, alog_c_ref, dt_c_ref, nw_ref, o_ref, xpad_ref, st_ref):
    tb = p_ref.shape[1]
    c = SCAN_CHUNK
    nc = tb // c
    qkv_w = 3 * WIDTH

    @pl.when(pl.program_id(1) == 0)
    def _():
        xpad_ref[:8, :] = jnp.zeros((8, xpad_ref.shape[1]), F32)
        st_ref[...] = jnp.zeros_like(st_ref)

    p = p_ref[0]
    xin = p[:, :qkv_w]
    z = p[:, qkv_w:qkv_w + WIDTH]
    b_in = p[:, qkv_w + WIDTH:qkv_w + WIDTH + N_HEADS]
    a_in = p[:, qkv_w + WIDTH + N_HEADS:qkv_w + WIDTH + 2 * N_HEADS]

    xpad_ref[8:, :] = xin
    conv = xin * cw_ref[GDN_CONV - 1:GDN_CONV, :]
    for s in range(1, GDN_CONV):
        conv = conv + xpad_ref[8 - s:8 - s + tb, :] * cw_ref[GDN_CONV - 1 - s:GDN_CONV - s, :]
    xpad_ref[:8, :] = xin[tb - 8:, :]
    qkv = _silu(conv)

    beta = _sigmoid(b_in)
    g_col = -jnp.exp(alog_r_ref[...]) * _softplus(a_in + dt_r_ref[...])
    gc_col = _chunk_cumsum_rows(g_col, c)
    eye_h = (_iota((N_HEADS, N_HEADS), 0) == _iota((N_HEADS, N_HEADS), 1)).astype(F32)
    a_row = _mm32_nt(eye_h, a_in)
    g_row = -jnp.exp(alog_c_ref[...]) * _softplus(a_row + dt_c_ref[...])
    gc_row = _chunk_cumsum_cols(g_row, c)

    def per_head_cols(y):
        return jnp.stack([y[i * c:(i + 1) * c, h:h + 1] for i in range(nc) for h in range(N_HEADS)])

    def over_head_lanes(y):
        expand = (_iota((N_HEADS, WIDTH), 1) // HEAD_DIM == _iota((N_HEADS, WIDTH), 0)).astype(MXU_DTYPE)
        return _tree_sum([jnp.dot(part, expand, preferred_element_type=F32) for part in _split3(y)])

    gc = per_head_cols(gc_col)
    g_rows = jnp.stack([gc_row[h:h + 1, i * c:(i + 1) * c] for i in range(nc) for h in range(N_HEADS)])
    gc_d = over_head_lanes(gc_col)
    beta_d = over_head_lanes(beta)
    g_last_d = jnp.concatenate([jnp.broadcast_to(gc_d[(i + 1) * c - 1:(i + 1) * c, :], (c, WIDTH)) for i in range(nc)],
                               axis=0)

    seg = _head_segments(WIDTH)
    q = qkv[:, :WIDTH]
    k = qkv[:, WIDTH:2 * WIDTH]
    q = q * lax.rsqrt(_head_sums(q * q, seg) + 1e-6) * (HEAD_DIM ** -0.5)
    k = k * lax.rsqrt(_head_sums(k * k, seg) + 1e-6)
    kb_d = k * beta_d
    e_g = jnp.exp(gc_d)

    st = lambda y: _stack_heads(y, nc, c)
    q_h, k_h, kb = st(q), st(k), st(kb_d)

    ri = _iota((c, c), 0)
    ci = _iota((c, c), 1)
    incl = (ri >= ci)[None]
    strict = (ri > ci)[None]
    decay = jnp.exp(jnp.where(incl, gc - g_rows, NEG_INF))
    mm = _bmm_nt(jnp.concatenate([kb, q_h], axis=1), k_h)
    a_mat = jnp.where(strict, mm[:, :c] * decay, 0.0)
    qk = jnp.where(incl, mm[:, c:] * decay, 0.0)
    rhs = jnp.concatenate([st(qkv[:, 2 * WIDTH:] * beta_d), st(kb_d * e_g)], axis=-1)
    uw = _batched_unit_lower_solve(-a_mat, rhs)
    qkuw = _bmm(qk, uw)
    o0 = qkuw[:, :, :HEAD_DIM]
    oq = st(q * e_g) - qkuw[:, :, HEAD_DIM:]
    kuw = _bmm_tn(st(k * jnp.exp(g_last_d - gc_d)), uw)
    gamma = jnp.exp(gc[:, c - 1:c, :])
    o = _run_chunks(st_ref, -kuw[:, :, HEAD_DIM:], kuw[:, :, :HEAD_DIM], oq, o0, gamma, nc)

    o = _unstack_heads(o, nc)
    o = o * lax.rsqrt(_head_sums(o * o, seg) * (1.0 / HEAD_DIM) + RMS_EPS) * _lane_tile(nw_ref[...], N_HEADS)
    o_ref[0] = o * _silu(z)


def _gdn(p, conv_w, a_log, dt_bias, norm_w, tb=SCAN_BLOCK):
    b, t, cols = p.shape
    tb = min(tb, t)
    assert t % tb == 0 and tb % SCAN_CHUNK == 0 and cols == GDN_COLS
    params = [conv_w.astype(F32), a_log.reshape(1, -1), dt_bias.reshape(1, -1), a_log.reshape(-1, 1),
              dt_bias.reshape(-1, 1), norm_w.reshape(1, -1)]
    const = lambda i, j: (0, 0)
    return pl.pallas_call(
        _gdn_kernel,
        grid=(b, t // tb),
        in_specs=[pl.BlockSpec((1, tb, cols), lambda i, j: (i, j, 0))] + [pl.BlockSpec(q.shape, const) for q in params],
        out_specs=pl.BlockSpec((1, tb, WIDTH), lambda i, j: (i, j, 0)),
        out_shape=jax.ShapeDtypeStruct((b, t, WIDTH), F32),
        scratch_shapes=[pltpu.VMEM((8 + tb, 3 * WIDTH), F32), pltpu.VMEM((N_HEADS, HEAD_DIM, HEAD_DIM), F32)],
        compiler_params=pltpu.CompilerParams(dimension_semantics=("parallel", "arbitrary"),
                                             vmem_limit_bytes=VMEM_LIMIT),
        name="gdn",
    )(p, *params)


def _key_to_float(key):
    return pltpu.bitcast(jnp.where(key >= 0, key, key ^ jnp.int32(0x7FFFFFFF)), F32)


NEG_INF_KEY = -2139095041
COARSE_DTYPE = jnp.bfloat16
SOFTMAX_FLOOR = -1e30


def _dsa_kernel(dq_ref, iq_ref, kv_ref, ikw_ref, iwq_ref, qn_ref, kn_ref, lnw_ref, lnb_ref, o_ref,
                k_s, vt_s, ik_s, score_s, coarse_s, distm_s, *, top_k):
    qb = dq_ref.shape[1]
    t_len = kv_ref.shape[1]
    kc = DSA_KEY_CHUNK
    n_chunks = t_len // kc
    kvw = DSA_KV_HEADS * HEAD_DIM
    i = pl.program_id(1)
    n_live = (i * qb + qb + kc - 1) // kc

    @pl.when(i == 0)
    def _():
        kv = kv_ref[0]
        kraw = kv[:, :kvw]
        kn = kraw * lax.rsqrt(_head_sums(kraw * kraw, _head_segments(kvw)) * (1.0 / HEAD_DIM) + RMS_EPS)
        k_s[...] = (kn * _lane_tile(kn_ref[...], DSA_KV_HEADS)).astype(MXU_DTYPE)
        ones_rows = jnp.where(_iota((8, kc), 0) == 0, 1.0, 0.0)
        for c in range(n_chunks):
            v_t = kv[c * kc:(c + 1) * kc, kvw:].T
            vt_s[c] = jnp.concatenate([blk for g in range(DSA_KV_HEADS) for blk in (_head_rows(v_t, g), ones_rows)],
                                      axis=0).astype(MXU_DTYPE)
        ik = ikw_ref[0][:, :IDX_DIM]
        ikc = ik - jnp.mean(ik, axis=-1, keepdims=True)
        ik = ikc * lax.rsqrt(jnp.mean(ikc * ikc, axis=-1, keepdims=True) + 1e-6) * lnw_ref[...] + lnb_ref[...]
        hi, lo = _split_hi_lo(ik)
        ik_s[...] = jnp.concatenate([hi, hi, lo], axis=-1)
        score_s[...] = jnp.full(score_s.shape, NEG_INF, F32)

    t_row = i * qb + _iota((1, qb), 1)
    s_loc = _iota((kc, 1), 0)

    eye_h = (_iota((IDX_HEADS, IDX_HEADS), 0) == _iota((IDX_HEADS, IDX_HEADS), 1)).astype(F32)
    iw_t = _mm32_nt(eye_h, iwq_ref[0][:, IDX_DIM:IDX_DIM + IDX_HEADS]) * (IDX_HEADS ** -0.5 * IDX_DIM ** -0.5)
    iw_wide = jnp.concatenate([iw_t[h:h + 1, :] for h in range(IDX_HEADS)], axis=1)
    hi, lo = _split_hi_lo(iq_ref[0])
    iq_all = jnp.concatenate([jnp.concatenate([_head(hi, h), _head(lo, h), _head(hi, h)], axis=-1)
                              for h in range(IDX_HEADS)], axis=0)

    def score_chunk(c, carry):
        rows = pl.ds(pl.multiple_of(c * kc, kc), kc)
        dots = lax.dot_general(ik_s[rows, :], iq_all, (((1,), (1,)), ((), ())), preferred_element_type=F32)
        w = iw_wide * jnp.maximum(dots, 0.0)
        sc = _tree_sum([w[:, h * qb:(h + 1) * qb] for h in range(IDX_HEADS)])
        sc = jnp.where(c * kc + s_loc <= t_row, sc, NEG_INF)
        score_s[rows, :] = sc
        coarse_s[rows, :] = sc.astype(COARSE_DTYPE)
        return carry

    lax.fori_loop(0, n_live, score_chunk, 0)

    kf = float(top_k)
    grp = DSA_COUNT_ROWS
    n_grp = (i * qb + qb + grp - 1) // grp

    def count(pred):
        def body(g, acc):
            blk = score_s[pl.ds(pl.multiple_of(g * grp, grp), grp), :]
            ones = jnp.where(pred(blk), 1.0, 0.0)
            return acc + _tree_sum([ones[r * 8:(r + 1) * 8] for r in range(grp // 8)])
        acc = lax.fori_loop(0, n_grp, body, jnp.zeros((8, qb), F32))
        return jnp.sum(acc, axis=0, keepdims=True)

    def count_coarse(cand):
        def body(g, acc):
            blk = coarse_s[pl.ds(pl.multiple_of(g * grp, grp), grp), :]
            ones = jnp.where(blk >= cand, jnp.ones((), COARSE_DTYPE), jnp.zeros((), COARSE_DTYPE))
            return acc + _tree_sum([ones[r * 16:(r + 1) * 16] for r in range(grp // 16)]).astype(F32)
        acc = lax.fori_loop(0, n_grp, body, jnp.zeros((16, qb), F32))
        return jnp.sum(acc, axis=0, keepdims=True)

    def coarse_key(base):
        return jnp.where(base >= 0, base, base | jnp.int32(0xFFFF))

    int_min = jnp.int32(-2 ** 31)
    base = jnp.where(count_coarse(jnp.zeros((1, qb), COARSE_DTYPE)) >= kf, jnp.int32(0), int_min)

    def coarse_step(b, base):
        cand = base | (jnp.int32(1) << (30 - b))
        cand_f = _key_to_float(coarse_key(cand)).astype(COARSE_DTYPE)
        return jnp.where(count_coarse(cand_f) >= kf, cand, base)

    base = lax.fori_loop(0, 15, coarse_step, base)
    lo_key = coarse_key(base) - jnp.int32(2 ** 15)

    def fine_step(b, off):
        cand_off = off | (jnp.int32(1) << (16 - b))
        cand_f = _key_to_float(lo_key + cand_off)
        return jnp.where(count(lambda blk: blk >= cand_f) >= kf, cand_off, off)

    tau = lo_key + lax.fori_loop(0, 17, fine_step, jnp.zeros((1, qb), jnp.int32))
    tau_f = _key_to_float(jnp.maximum(tau, jnp.int32(NEG_INF_KEY)))

    n_ge = count(lambda blk: blk >= tau_f)
    no_partial_tie = jnp.min(jnp.where((n_ge <= kf) | (tau_f == NEG_INF), 1.0, 0.0)) > 0.5

    def masked_distance(c, sel_fn):
        rows = pl.ds(pl.multiple_of(c * kc, kc), kc)
        s_pos = c * kc + s_loc
        sel = sel_fn(score_s[rows, :]) & (s_pos <= t_row)
        distm_s[rows, :] = jnp.where(sel, (t_row - s_pos).astype(F32), jnp.inf)

    @pl.when(no_partial_tie)
    def _():
        def select_chunk(c, carry):
            masked_distance(c, lambda blk: blk >= tau_f)
            return carry
        lax.fori_loop(0, n_live, select_chunk, 0)

    @pl.when(jnp.logical_not(no_partial_tie))
    def _():
        lower = (_iota((kc, kc), 0) >= _iota((kc, kc), 1)).astype(MXU_DTYPE)
        need = kf - count(lambda blk: blk > tau_f)

        def select_chunk(c, run):
            def sel_fn(blk):
                eq = blk == tau_f
                pref = run + jnp.dot(lower, jnp.where(eq, 1.0, 0.0).astype(MXU_DTYPE), preferred_element_type=F32)
                return (blk > tau_f) | (eq & (pref <= need))
            masked_distance(c, sel_fn)
            rows = pl.ds(pl.multiple_of(c * kc, kc), kc)
            return run + jnp.sum(jnp.where(score_s[rows, :] == tau_f, 1.0, 0.0), axis=0, keepdims=True)
        lax.fori_loop(0, n_live, select_chunk, jnp.zeros((1, qb), F32))

    dq = dq_ref[0]
    dq = dq * lax.rsqrt(_head_sums(dq * dq, _head_segments(WIDTH)) * (1.0 / HEAD_DIM) + RMS_EPS)
    dq = (dq * _lane_tile(qn_ref[...] * (HEAD_DIM ** -0.5 * LOG2_E), N_HEADS)).astype(MXU_DTYPE)
    qs = [_head(dq, h) for h in range(N_HEADS)]
    q_grp = [jnp.concatenate(qs[g * DSA_GROUP:(g + 1) * DSA_GROUP], axis=0) for g in range(DSA_KV_HEADS)]
    slopes = jnp.concatenate([jnp.full((1, qb), DSA_SLOPES[h] * LOG2_E, F32) for h in range(N_HEADS)], axis=1)
    gw = DSA_GROUP * qb

    def attend_chunk(c, carry):
        m, acc = carry
        rows = pl.ds(pl.multiple_of(c * kc, kc), kc)
        kch = k_s[rows, :]
        vt = vt_s[c]
        sc = jnp.concatenate(
            [lax.dot_general(kch[:, g * HEAD_DIM:(g + 1) * HEAD_DIM], q_grp[g], (((1,), (1,)), ((), ())),
                             preferred_element_type=F32) for g in range(DSA_KV_HEADS)], axis=1)
        sc = sc - slopes * _lane_tile(distm_s[rows, :], N_HEADS)
        m_new = jnp.maximum(m, jnp.max(sc, axis=0, keepdims=True))
        alpha = jnp.exp2(m - m_new)
        eb = jnp.exp2(sc - m_new).astype(MXU_DTYPE)
        pv = jnp.concatenate(
            [jnp.dot(vt[g * VAL_ROWS:(g + 1) * VAL_ROWS, :], eb[:, g * gw:(g + 1) * gw], preferred_element_type=F32)
             for g in range(DSA_KV_HEADS)], axis=1)
        return m_new, alpha * acc + pv

    init = (jnp.full((1, N_HEADS * qb), SOFTMAX_FLOOR, F32), jnp.zeros((VAL_ROWS, N_HEADS * qb), F32))
    _, acc = lax.fori_loop(0, n_live, attend_chunk, init)
    l = acc[HEAD_DIM:HEAD_DIM + 1]
    acc = acc[:HEAD_DIM]
    o_t = jnp.concatenate([acc[:, h * qb:(h + 1) * qb] / l[:, h * qb:(h + 1) * qb] for h in range(N_HEADS)], axis=0)
    o_ref[0] = o_t.T


def _dsa(p, q_norm, k_norm, ln_w, ln_b):
    b, t, cols = p.shape
    qb = DSA_Q_BLOCK
    kc = DSA_KEY_CHUNK
    assert t % qb == 0 and cols == DSA_COLS and t % DSA_COUNT_ROWS == 0 and t % kc == 0 and kc % DSA_COUNT_ROWS == 0
    top_k = min(DSA_TOPK_MAX, t // 4)
    kvw = 2 * DSA_KV_HEADS * HEAD_DIM
    row = lambda v: v.reshape(1, -1).astype(F32)
    params = [row(q_norm), row(k_norm), row(ln_w), row(ln_b)]
    const = lambda i, j: (0, 0)
    return pl.pallas_call(
        functools.partial(_dsa_kernel, top_k=top_k),
        grid=(b, t // qb),
        in_specs=[pl.BlockSpec((1, qb, WIDTH), lambda i, j: (i, j, 0)),
                  pl.BlockSpec((1, qb, WIDTH), lambda i, j: (i, j, 1)),
                  pl.BlockSpec((1, t, kvw), lambda i, j: (i, 0, 2 * WIDTH // kvw)),
                  pl.BlockSpec((1, t, LANES), lambda i, j: (i, 0, (2 * WIDTH + kvw) // LANES)),
                  pl.BlockSpec((1, qb, LANES), lambda i, j: (i, j, (2 * WIDTH + kvw) // LANES))]
        + [pl.BlockSpec(q.shape, const) for q in params],
        out_specs=pl.BlockSpec((1, qb, WIDTH), lambda i, j: (i, j, 0)),
        out_shape=jax.ShapeDtypeStruct((b, t, WIDTH), F32),
        scratch_shapes=[pltpu.VMEM((t, DSA_KV_HEADS * HEAD_DIM), MXU_DTYPE),
                        pltpu.VMEM((t // kc, DSA_KV_HEADS * VAL_ROWS, kc), MXU_DTYPE),
                        pltpu.VMEM((t, 3 * IDX_DIM), MXU_DTYPE),
                        pltpu.VMEM((t, qb), F32),
                        pltpu.VMEM((t, qb), COARSE_DTYPE),
                        pltpu.VMEM((t, qb), F32)],
        compiler_params=pltpu.CompilerParams(dimension_semantics=("parallel", "arbitrary"),
                                             vmem_limit_bytes=VMEM_LIMIT),
        name="dsa",
    )(p, p, p, p, p, *params)


MOBA_PENALTY = 1e30


def _moba_kernel(q_ref, k_ref, v_ref, qn_ref, kn_ref, o_ref, k_s, vt_s, km_s, causal_s):
    bs = q_ref.shape[1]
    t_len = k_ref.shape[1]
    n_kb = t_len // bs
    n_pad = -(-n_kb // 8) * 8
    i = pl.program_id(1)

    @pl.when(i == 0)
    def _():
        kraw = k_ref[0]
        kn = kraw * lax.rsqrt(_head_sums(kraw * kraw, _head_segments(WIDTH)) * (1.0 / HEAD_DIM) + RMS_EPS)
        kn = kn * _lane_tile(kn_ref[...], N_HEADS)
        s_pos = _iota((t_len, 1), 0)
        fcol = _iota((1, HEAD_DIM), 1)
        extra = jnp.where(fcol == n_pad, (s_pos % bs).astype(F32),
                          jnp.where((fcol == n_pad + 1) | (fcol == s_pos // bs), 1.0, 0.0)).astype(MXU_DTYPE)
        for h in range(N_HEADS):
            k_s[:, 2 * h * HEAD_DIM:(2 * h + 1) * HEAD_DIM] = _head(kn, h).astype(MXU_DTYPE)
            k_s[:, (2 * h + 1) * HEAD_DIM:(2 * h + 2) * HEAD_DIM] = extra
        means = jnp.concatenate([jnp.mean(kn[j * bs:(j + 1) * bs], axis=0, keepdims=True) for j in range(n_kb)]
                                + [jnp.zeros((n_pad - n_kb, WIDTH), F32)] * (n_pad > n_kb), axis=0)
        head_of_lane = _iota((1, WIDTH), 1) // HEAD_DIM
        rows = [jnp.where(head_of_lane == h, means, 0.0) for h in range(N_HEADS)]
        rows.append(jnp.zeros((LANES - N_HEADS * n_pad, WIDTH), F32))
        table = jnp.concatenate(rows, axis=0).T
        hi, lo = _split_hi_lo(table)
        km_s[...] = jnp.concatenate([hi, lo, hi], axis=0)
        v = v_ref[0]
        ones_rows = jnp.where(_iota((8, bs), 0) == 0, 1.0, 0.0)
        for j in range(n_kb):
            v_t = v[j * bs:(j + 1) * bs, :].T
            vt_s[j] = jnp.concatenate([blk for h in range(N_HEADS) for blk in (_head_rows(v_t, h), ones_rows)],
                                      axis=0).astype(MXU_DTYPE)
        above = _iota((bs, 1), 0) > _iota((1, bs), 1)
        causal_s[...] = _lane_tile(jnp.where(above, NEG_INF, 0.0), N_HEADS)

    qraw = q_ref[0]
    jcol = _iota((n_pad, 1), 0)
    own = pl.multiple_of(i * bs, bs)
    qn = qraw * lax.rsqrt(_head_sums(qraw * qraw, _head_segments(WIDTH)) * (1.0 / HEAD_DIM) + RMS_EPS)
    qn = qn * _lane_tile(qn_ref[...], N_HEADS)
    q_hi, q_lo = _split_hi_lo(qn)
    gate_qm = jnp.dot(jnp.concatenate([q_hi, q_hi, q_lo], axis=1), km_s[...], preferred_element_type=F32)
    gate_rows = gate_qm.T
    gate = jnp.concatenate([gate_rows[h * n_pad:(h + 1) * n_pad] for h in range(N_HEADS)], axis=1)
    gate = jnp.where(jcol < i, gate, NEG_INF)
    rank = jnp.zeros(gate.shape, F32)
    for j2 in range(n_kb):
        other = gate[j2:j2 + 1, :]
        beats = (other > gate) | ((other == gate) & (j2 < jcol))
        rank = rank + jnp.where(beats, 1.0, 0.0)
    picked = (rank < float(MOBA_TOPK)) & (jcol < i)
    slopes = jnp.concatenate([jnp.full((1, bs), MOBA_SLOPES[h], F32) for h in range(N_HEADS)], axis=1)
    block_term = jnp.where(jcol < i, -(slopes * ((i - jcol) * bs).astype(F32) + jnp.where(picked, 0.0, MOBA_PENALTY)),
                           0.0)
    q_t = (qn * (HEAD_DIM ** -0.5)).T
    frow = _iota((HEAD_DIM - n_pad, 1), 0)
    t_loc = _iota((1, bs), 1).astype(F32)
    q_aug = []
    for h in range(N_HEADS):
        tail = jnp.where(frow == 0, MOBA_SLOPES[h], jnp.where(frow == 1, -MOBA_SLOPES[h] * t_loc, 0.0))
        q_aug.append(jnp.concatenate([q_t[h * HEAD_DIM:(h + 1) * HEAD_DIM], block_term[:, h * bs:(h + 1) * bs], tail],
                                     axis=0).astype(MXU_DTYPE))

    def scores(start):
        return jnp.concatenate(
            [jnp.dot(k_s[pl.ds(start, bs), 2 * h * HEAD_DIM:(2 * h + 2) * HEAD_DIM], q_aug[h],
                     preferred_element_type=F32) for h in range(N_HEADS)], axis=1)

    def weighted_values(j, e):
        vt = vt_s[j]
        eb = e.astype(MXU_DTYPE)
        return jnp.concatenate(
            [jnp.dot(vt[h * VAL_ROWS:(h + 1) * VAL_ROWS, :], eb[:, h * bs:(h + 1) * bs], preferred_element_type=F32)
             for h in range(N_HEADS)], axis=1)

    sc = scores(own) + causal_s[...]
    m0 = jnp.max(sc, axis=0, keepdims=True)
    acc0 = weighted_values(i, jnp.exp(sc - m0))

    def block_step(j, carry):
        m, acc = carry
        start = pl.multiple_of(j * bs, bs)
        again = pl.multiple_of(start + jnp.minimum(i, 0) * bs, bs)
        m_new = jnp.maximum(m, jnp.max(scores(start), axis=0, keepdims=True))
        alpha = jnp.exp(m - m_new)
        return m_new, alpha * acc + weighted_values(j, jnp.exp(scores(again) - m_new))

    _, acc = lax.fori_loop(0, i, block_step, (m0, acc0))
    l = acc[HEAD_DIM:HEAD_DIM + 1]
    acc = acc[:HEAD_DIM]
    o_t = jnp.concatenate([acc[:, h * bs:(h + 1) * bs] / l[:, h * bs:(h + 1) * bs] for h in range(N_HEADS)], axis=0)
    o_ref[0] = o_t.T


def _moba(p, q_norm, k_norm):
    b, t, cols = p.shape
    bs = MOBA_BLOCK
    n_pad = -(-(t // bs) // 8) * 8
    assert t % bs == 0 and cols == MOBA_COLS and n_pad + 2 <= HEAD_DIM and N_HEADS * n_pad <= LANES and bs <= 256
    row = lambda v: v.reshape(1, -1).astype(F32)
    const = lambda i, j: (0, 0)
    return pl.pallas_call(
        _moba_kernel,
        grid=(b, t // bs),
        in_specs=[pl.BlockSpec((1, bs, WIDTH), lambda i, j: (i, j, 0)),
                  pl.BlockSpec((1, t, WIDTH), lambda i, j: (i, 0, 1)),
                  pl.BlockSpec((1, t, WIDTH), lambda i, j: (i, 0, 2)),
                  pl.BlockSpec((1, HEAD_DIM), const), pl.BlockSpec((1, HEAD_DIM), const)],
        out_specs=pl.BlockSpec((1, bs, WIDTH), lambda i, j: (i, j, 0)),
        out_shape=jax.ShapeDtypeStruct((b, t, WIDTH), F32),
        scratch_shapes=[pltpu.VMEM((t, 2 * WIDTH), MXU_DTYPE),
                        pltpu.VMEM((t // bs, N_HEADS * VAL_ROWS, bs), MXU_DTYPE),
                        pltpu.VMEM((3 * WIDTH, LANES), MXU_DTYPE),
                        pltpu.VMEM((bs, N_HEADS * bs), F32)],
        compiler_params=pltpu.CompilerParams(dimension_semantics=("parallel", "arbitrary"),
                                             vmem_limit_bytes=VMEM_LIMIT),
        name="moba",
    )(p, p, p, row(q_norm), row(k_norm))


def _pad_cols(w, n):
    return jnp.pad(w, ((0, 0), (0, n - w.shape[1])))


def _even_layer(x, b, t, norm_g, norm2_g, mlp_w1, mlp_w2, w_in, w_out, mu, w0, w2, a0, a2, g2, k_k, k_a, r_k,
                lnx_w, lnx_b, conv_w, a_log, dt_bias, gdn_norm_w):
    qkv_w = 3 * WIDTH
    w_rwkv = w_in[:, :RWKV_COLS]
    w_g = w_in[:, RWKV_COLS:]
    w_gdn = jnp.concatenate([w_g[:, :qkv_w], w_g[:, qkv_w + 2 * N_HEADS:],
                             _pad_cols(w_g[:, qkv_w:qkv_w + 2 * N_HEADS], LANES)], axis=1)
    p_rwkv, p_gdn = _norm_proj(x, norm_g, [w_rwkv.astype(MXU_DTYPE), w_gdn.astype(MXU_DTYPE)])
    o_a = _rwkv(p_rwkv.reshape(b, t, -1), mu, w0, w2, a0, a2, g2, k_k, k_a, r_k, lnx_w, lnx_b)
    o_b = _gdn(p_gdn.reshape(b, t, -1), conv_w, a_log, dt_bias, gdn_norm_w)
    return _out_mlp(x, o_a.reshape(b * t, -1), o_b.reshape(b * t, -1), w_out, norm2_g, mlp_w1, mlp_w2)


def _odd_layer(x, b, t, norm_g, norm2_g, mlp_w1, mlp_w2, w_in, w_out, dsa_q_norm, dsa_k_norm, idx_ln_w, idx_ln_b,
               moba_q_norm, moba_k_norm):
    kvw = 2 * DSA_KV_HEADS * HEAD_DIM
    sizes = (WIDTH, kvw, IDX_HEADS * IDX_DIM, IDX_DIM, IDX_HEADS, WIDTH, WIDTH, WIDTH)
    offs = [0]
    for s in sizes:
        offs.append(offs[-1] + s)
    dq, dkv, iq, ik, iw, mq, mk, mv = (w_in[:, offs[n]:offs[n + 1]] for n in range(len(sizes)))
    w_dsa = jnp.concatenate([dq, iq, dkv, _pad_cols(jnp.concatenate([ik, iw], axis=1), LANES)], axis=1)
    w_moba = jnp.concatenate([mq, mk, mv], axis=1)
    p_dsa, p_moba = _norm_proj(x, norm_g, [w_dsa.astype(MXU_DTYPE), w_moba.astype(MXU_DTYPE)])
    o_c = _dsa(p_dsa.reshape(b, t, -1), dsa_q_norm, dsa_k_norm, idx_ln_w, idx_ln_b)
    o_d = _moba(p_moba.reshape(b, t, -1), moba_q_norm, moba_k_norm)
    return _out_mlp(x, o_c.reshape(b * t, -1), o_d.reshape(b * t, -1), w_out, norm2_g, mlp_w1, mlp_w2)


def kernel(x, norm1_g, norm2_g, mlp_w1, mlp_w2, ev_w_in, ev_w_out, rwkv_mu, rwkv_w0, rwkv_w2, rwkv_a0, rwkv_a2, rwkv_g2, rwkv_k_k, rwkv_k_a, rwkv_r_k, rwkv_lnx_w, rwkv_lnx_b, gdn_conv_w, gdn_a_log, gdn_dt_bias, gdn_norm_w, od_w_in, od_w_out, dsa_q_norm, dsa_k_norm, idx_k_ln_w, idx_k_ln_b, moba_q_norm, moba_k_norm):
    b, t, d = x.shape
    depth = norm1_g.shape[0]
    h = x.reshape(b * t, d)
    for i in range(depth):
        j = i // 2
        if i % 2 == 0:
            h = _even_layer(h, b, t, norm1_g[i], norm2_g[i], mlp_w1[i], mlp_w2[i], ev_w_in[j], ev_w_out[j],
                            rwkv_mu[j], rwkv_w0[j], rwkv_w2[j], rwkv_a0[j], rwkv_a2[j], rwkv_g2[j], rwkv_k_k[j], rwkv_k_a[j], rwkv_r_k[j],
                            rwkv_lnx_w[j], rwkv_lnx_b[j], gdn_conv_w[j], gdn_a_log[j], gdn_dt_bias[j],
                            gdn_norm_w[j])
        else:
            h = _odd_layer(h, b, t, norm1_g[i], norm2_g[i], mlp_w1[i], mlp_w2[i], od_w_in[j], od_w_out[j],
                           dsa_q_norm[j], dsa_k_norm[j], idx_k_ln_w[j], idx_k_ln_b[j], moba_q_norm[j], moba_k_norm[j])
    return h.reshape(b, t, d)
```

```python
import functools
import math

import jax
import jax.numpy as jnp
from jax import lax
from jax.experimental import pallas as pl
from jax.experimental.pallas import tpu as pltpu

F32 = jnp.float32
MXU_DTYPE = jnp.bfloat16
HIGHEST = lax.Precision.HIGHEST

LANES = 128
VMEM_LIMIT = 56 * 1024 * 1024

HEAD_DIM = 64
N_HEADS = 8
WIDTH = N_HEADS * HEAD_DIM
RMS_EPS = 1e-6
VAL_ROWS = HEAD_DIM + 8

RWKV_DECAY_LORA = 64
RWKV_A_LORA = 64
RWKV_GATE_LORA = 128
RWKV_COLS = 3 * WIDTH + RWKV_DECAY_LORA + RWKV_A_LORA + RWKV_GATE_LORA
RWKV_GN_EPS = 6.4e-4

GDN_CONV = 4
GDN_COLS = 3 * WIDTH + WIDTH + LANES
SCAN_CHUNK = 64
SCAN_BLOCK = 256

DSA_KV_HEADS = 2
DSA_GROUP = N_HEADS // DSA_KV_HEADS
IDX_HEADS = 8
IDX_DIM = 64
DSA_TOPK_MAX = 256
DSA_Q_BLOCK = 256
DSA_KEY_CHUNK = 512
DSA_COUNT_ROWS = 256
DSA_COLS = 2 * WIDTH + 2 * DSA_KV_HEADS * HEAD_DIM + LANES

MOBA_BLOCK = 256
MOBA_TOPK = 3
MOBA_COLS = 3 * WIDTH

PROJ_ROWS = 512
MLP_ROWS = 1024
MLP_FF_COLS = 1024

ALIBI_HEADS = 2 * N_HEADS
NEG_INF = float("-inf")
LOG2_E = math.log2(math.e)


def _alibi_slope(i):
    return 2.0 ** (-8.0 * (i + 1) / ALIBI_HEADS)


DSA_SLOPES = tuple(_alibi_slope(2 * h) for h in range(N_HEADS))
MOBA_SLOPES = tuple(_alibi_slope(2 * h + 1) for h in range(N_HEADS))


def _mm(a, b):
    return jnp.dot(a.astype(MXU_DTYPE), b.astype(MXU_DTYPE), preferred_element_type=F32)


def _mm32_nt(a, b):
    return lax.dot_general(a, b, (((1,), (1,)), ((), ())), preferred_element_type=F32, precision=HIGHEST)


def _iota(shape, dim):
    return lax.broadcasted_iota(jnp.int32, shape, dim)


def _sigmoid(x):
    return 1.0 / (1.0 + jnp.exp(-x))


def _silu(x):
    return x * _sigmoid(x)


def _softplus(x):
    return jnp.maximum(x, 0.0) + jnp.log1p(jnp.exp(-jnp.abs(x)))


def _head(x, h):
    return x[:, h * HEAD_DIM:(h + 1) * HEAD_DIM]


def _head_rows(x, h):
    return x[h * HEAD_DIM:(h + 1) * HEAD_DIM]


def _split_hi_lo(x):
    hi = x.astype(MXU_DTYPE)
    lo = (x - hi.astype(F32)).astype(MXU_DTYPE)
    return hi, lo


def _head_segments(width):
    return (_iota((width, width), 0) // HEAD_DIM == _iota((width, width), 1) // HEAD_DIM).astype(MXU_DTYPE)


def _head_sums(x, seg):
    hi, lo = _split_hi_lo(x)
    return jnp.dot(hi, seg, preferred_element_type=F32) + jnp.dot(lo, seg, preferred_element_type=F32)


def _tree_sum(parts):
    parts = list(parts)
    while len(parts) > 1:
        nxt = [parts[a] + parts[a + 1] for a in range(0, len(parts) - 1, 2)]
        if len(parts) % 2:
            nxt.append(parts[-1])
        parts = nxt
    return parts[0]


def _lane_tile(x, n):
    return jnp.concatenate([x] * n, axis=1)


def _norm_proj_kernel(x_ref, g_ref, *refs):
    n = len(refs) // 2
    x = x_ref[...]
    h = x * lax.rsqrt(jnp.mean(x * x, axis=-1, keepdims=True) + RMS_EPS) * g_ref[...]
    h = h.astype(MXU_DTYPE)
    for w_ref, o_ref in zip(refs[:n], refs[n:]):
        o_ref[...] = jnp.dot(h, w_ref[...], preferred_element_type=F32)


def _norm_proj(x, g, ws, tm=PROJ_ROWS):
    n, d = x.shape
    assert n % tm == 0
    const = lambda i: (0, 0)
    return pl.pallas_call(
        _norm_proj_kernel,
        grid=(n // tm,),
        in_specs=[pl.BlockSpec((tm, d), lambda i: (i, 0)), pl.BlockSpec((1, d), const)]
        + [pl.BlockSpec(w.shape, const) for w in ws],
        out_specs=[pl.BlockSpec((tm, w.shape[1]), lambda i: (i, 0)) for w in ws],
        out_shape=[jax.ShapeDtypeStruct((n, w.shape[1]), F32) for w in ws],
        compiler_params=pltpu.CompilerParams(dimension_semantics=("parallel",), vmem_limit_bytes=VMEM_LIMIT),
        name="norm_proj",
    )(x, g.reshape(1, d), *ws)


def _out_mlp_kernel(x_ref, a_ref, b_ref, wa_ref, wb_ref, g_ref, w1_ref, w2_ref, o_ref, h_ref, acc_ref):
    j = pl.program_id(1)

    @pl.when(j == 0)
    def _():
        x = x_ref[...] + _mm(a_ref[...], wa_ref[...]) + _mm(b_ref[...], wb_ref[...])
        h = x * lax.rsqrt(jnp.mean(x * x, axis=-1, keepdims=True) + RMS_EPS) * g_ref[...]
        h_ref[...] = h.astype(MXU_DTYPE)
        acc_ref[...] = x

    u = jnp.maximum(jnp.dot(h_ref[...], w1_ref[...], preferred_element_type=F32), 0.0)
    acc_ref[...] += jnp.dot((u * u).astype(MXU_DTYPE), w2_ref[...], preferred_element_type=F32)

    @pl.when(j == pl.num_programs(1) - 1)
    def _():
        o_ref[...] = acc_ref[...]


def _out_mlp(x, a, b, w_out, g, w1, w2, tm=MLP_ROWS, tf=MLP_FF_COLS):
    n, d = x.shape
    f = w1.shape[1]
    tm = min(tm, n)
    tf = min(tf, f)
    assert n % tm == 0 and f % tf == 0
    wa = w_out[:a.shape[1]].astype(MXU_DTYPE)
    wb = w_out[a.shape[1]:].astype(MXU_DTYPE)
    row = lambda i, j: (i, 0)
    const = lambda i, j: (0, 0)
    return pl.pallas_call(
        _out_mlp_kernel,
        grid=(n // tm, f // tf),
        in_specs=[pl.BlockSpec((tm, d), row), pl.BlockSpec((tm, a.shape[1]), row), pl.BlockSpec((tm, b.shape[1]), row),
                  pl.BlockSpec(wa.shape, const), pl.BlockSpec(wb.shape, const), pl.BlockSpec((1, d), const),
                  pl.BlockSpec((d, tf), lambda i, j: (0, j)), pl.BlockSpec((tf, d), lambda i, j: (j, 0))],
        out_specs=pl.BlockSpec((tm, d), row),
        out_shape=jax.ShapeDtypeStruct((n, d), F32),
        scratch_shapes=[pltpu.VMEM((tm, d), MXU_DTYPE), pltpu.VMEM((tm, d), F32)],
        compiler_params=pltpu.CompilerParams(dimension_semantics=("parallel", "arbitrary"),
                                             vmem_limit_bytes=VMEM_LIMIT),
        name="out_mlp",
    )(x, a, b, wa, wb, g.reshape(1, d), w1.astype(MXU_DTYPE), w2.astype(MXU_DTYPE))


def _bmm(a, b):
    return lax.dot_general(a.astype(MXU_DTYPE), b.astype(MXU_DTYPE), (((2,), (1,)), ((0,), (0,))),
                           preferred_element_type=F32)


def _bmm_nt(a, b):
    return lax.dot_general(a.astype(MXU_DTYPE), b.astype(MXU_DTYPE), (((2,), (2,)), ((0,), (0,))),
                           preferred_element_type=F32)


def _bmm_tn(a, b):
    return lax.dot_general(a.astype(MXU_DTYPE), b.astype(MXU_DTYPE), (((1,), (1,)), ((0,), (0,))),
                           preferred_element_type=F32)


def _stack_heads(x, nc, c):
    return jnp.stack([x[i * c:(i + 1) * c, h * HEAD_DIM:(h + 1) * HEAD_DIM] for i in range(nc) for h in range(N_HEADS)])


def _unstack_heads(y, nc):
    return jnp.concatenate([jnp.concatenate([y[i * N_HEADS + h] for h in range(N_HEADS)], axis=-1)
                            for i in range(nc)], axis=0)


def _batched_unit_lower_solve(x, y):
    c = x.shape[1]
    y = y + _bmm(x, y)
    p = 2
    while p < c:
        x = _bmm(x, x)
        y = y + _bmm(x, y)
        p *= 2
    return y


def _split3(x):
    hi = x.astype(MXU_DTYPE)
    r = x - hi.astype(F32)
    mid = r.astype(MXU_DTYPE)
    lo = (r - mid.astype(F32)).astype(MXU_DTYPE)
    return hi, mid, lo


def _chunk_cumsum_rows(x, c):
    tri = (_iota((c, c), 0) >= _iota((c, c), 1)).astype(MXU_DTYPE)
    parts = _split3(x)
    return jnp.concatenate(
        [_tree_sum([jnp.dot(tri, p[i * c:(i + 1) * c], preferred_element_type=F32) for p in parts])
         for i in range(x.shape[0] // c)], axis=0)


def _chunk_cumsum_cols(x, c):
    tri = (_iota((c, c), 0) <= _iota((c, c), 1)).astype(MXU_DTYPE)
    parts = _split3(x)
    return jnp.concatenate(
        [_tree_sum([jnp.dot(p[:, i * c:(i + 1) * c], tri, preferred_element_type=F32) for p in parts])
         for i in range(x.shape[1] // c)], axis=1)


def _run_chunks(st_ref, m, q, oq, o0, gamma, nc):
    s = st_ref[...]
    outs = []
    for i in range(nc):
        g = slice(i * N_HEADS, (i + 1) * N_HEADS)
        mo = _bmm(jnp.concatenate([m[g], oq[g]], axis=1), s)
        outs.append(mo[:, HEAD_DIM:] + o0[g])
        s = gamma[g] * s + mo[:, :HEAD_DIM] + q[g]
    st_ref[...] = s
    return jnp.concatenate(outs, axis=0)


def _rwkv_kernel(p_ref, mu_ref, w0_ref, w2_ref, a0_ref, a2_ref, g2_ref, kk_ref, ka_ref, rk_ref, lnw_ref, lnb_ref,
                 o_ref, prev_ref, st_ref):
    tb = p_ref.shape[1]
    c = SCAN_CHUNK
    nc = tb // c

    @pl.when(pl.program_id(1) == 0)
    def _():
        prev_ref[...] = jnp.zeros_like(prev_ref)
        st_ref[...] = jnp.zeros_like(st_ref)

    p = p_ref[0]
    shifted = jnp.where(_iota((tb, 1), 0) == 0, prev_ref[...], pltpu.roll(p, 1, 0))
    prev_ref[...] = p[tb - 1:tb, :]
    x = p + (shifted - p) * mu_ref[...]

    r = x[:, 0:WIDTH]
    k = x[:, WIDTH:2 * WIDTH]
    v = x[:, 2 * WIDTH:3 * WIDTH]
    off = 3 * WIDTH
    wd = x[:, off:off + RWKV_DECAY_LORA]
    ad = x[:, off + RWKV_DECAY_LORA:off + RWKV_DECAY_LORA + RWKV_A_LORA]
    gd = x[:, off + RWKV_DECAY_LORA + RWKV_A_LORA:]

    w_pre = w0_ref[...] + _mm(jnp.tanh(wd), w2_ref[...])
    log_w = -(_sigmoid(w_pre) * math.exp(-0.5))
    a = _sigmoid(a0_ref[...] + _mm(ad, a2_ref[...]))
    g = _mm(_sigmoid(gd), g2_ref[...])
    kk_all = k * kk_ref[...]
    k = k * (1.0 + (a - 1.0) * ka_ref[...])

    gam = _chunk_cumsum_rows(log_w, c)
    gam_last = jnp.concatenate([jnp.broadcast_to(gam[(i + 1) * c - 1:(i + 1) * c, :], (c, WIDTH)) for i in range(nc)],
                               axis=0)

    seg = _head_segments(WIDTH)
    kk = kk_all * lax.rsqrt(_head_sums(kk_all * kk_all, seg) + 1e-6)
    b = kk * a
    e_neg = jnp.exp(-gam)
    tail = jnp.exp(gam_last - gam)

    st = lambda y: _stack_heads(y, nc, c)
    st_mxu = lambda y: st(y.astype(MXU_DTYPE))
    v_h = st_mxu(v)
    a_t = st(-kk * jnp.exp(gam - log_w))
    r_t = st(r * jnp.exp(gam))
    b_t = st_mxu(b * e_neg)
    k_t = st_mxu(k * e_neg)
    e_last = st(jnp.exp(gam_last))[:, 0:1, :]

    eye = (_iota((c, c), 0) == _iota((c, c), 1))[None]

    inter = _bmm_nt(jnp.concatenate([a_t, r_t], axis=1), jnp.concatenate([b_t, k_t], axis=1))
    row2 = _iota((c, 2 * c), 0)
    col2 = _iota((c, 2 * c), 1)
    col2 = jnp.where(col2 >= c, col2 - c, col2)
    top = jnp.where((row2 > col2)[None], inter[:, :c], 0.0)
    bot = jnp.where((row2 >= col2)[None], inter[:, c:], 0.0)
    akv = _bmm(top[:, :, c:], v_h)
    wu = _batched_unit_lower_solve(top[:, :, :c], jnp.concatenate([a_t, akv], axis=-1))
    wu_v = jnp.concatenate([wu.astype(MXU_DTYPE), jnp.concatenate([jnp.zeros_like(v_h), v_h], axis=-1)], axis=1)
    ro = _bmm(bot, wu_v)
    oq = r_t + ro[:, :, :HEAD_DIM]
    o0 = ro[:, :, HEAD_DIM:]
    mq = _bmm_tn(jnp.concatenate([st_mxu(b * tail), st_mxu(k * tail)], axis=1), wu_v)
    m = mq[:, :, :HEAD_DIM]
    q = mq[:, :, HEAD_DIM:]
    gamma = jnp.sum(jnp.where(eye, e_last, 0.0), axis=2, keepdims=True)

    o = _unstack_heads(_run_chunks(st_ref, m, q, oq, o0, gamma, nc), nc)

    oc = o - _head_sums(o, seg) * (1.0 / HEAD_DIM)
    on = oc * lax.rsqrt(_head_sums(oc * oc, seg) * (1.0 / HEAD_DIM) + RWKV_GN_EPS)
    bonus = _head_sums(r * k * rk_ref[...], seg) * v
    o_ref[0] = (on * lnw_ref[...] + lnb_ref[...] + bonus) * g


def _rwkv(p, mu, w0, w2, a0, a2, g2, k_k, k_a, r_k, lnx_w, lnx_b, tb=SCAN_BLOCK):
    b, t, cols = p.shape
    tb = min(tb, t)
    assert t % tb == 0 and tb % SCAN_CHUNK == 0 and cols == RWKV_COLS
    row = lambda v: v.reshape(1, -1).astype(F32)
    params = [row(mu), row(w0), w2.astype(MXU_DTYPE), row(a0), a2.astype(MXU_DTYPE), g2.astype(MXU_DTYPE),
              row(k_k), row(k_a), row(r_k), row(lnx_w), row(lnx_b)]
    const = lambda i, j: (0, 0)
    return pl.pallas_call(
        _rwkv_kernel,
        grid=(b, t // tb),
        in_specs=[pl.BlockSpec((1, tb, cols), lambda i, j: (i, j, 0))] + [pl.BlockSpec(q.shape, const) for q in params],
        out_specs=pl.BlockSpec((1, tb, WIDTH), lambda i, j: (i, j, 0)),
        out_shape=jax.ShapeDtypeStruct((b, t, WIDTH), F32),
        scratch_shapes=[pltpu.VMEM((1, cols), F32), pltpu.VMEM((N_HEADS, HEAD_DIM, HEAD_DIM), F32)],
        compiler_params=pltpu.CompilerParams(dimension_semantics=("parallel", "arbitrary"),
                                             vmem_limit_bytes=VMEM_LIMIT),
        name="rwkv7",
    )(p, *params)


def _gdn_kernel(p_ref, cw_ref, alog_r_ref, dt_r_ref, alog_c_ref, dt_c_ref, nw_ref, o_ref, xpad_ref, st_ref):
    tb = p_ref.shape[1]
    c = SCAN_CHUNK
    nc = tb // c
    qkv_w = 3 * WIDTH

    @pl.when(pl.program_id(1) == 0)
    def _():
        xpad_ref[:8, :] = jnp.zeros((8, xpad_ref.shape[1]), F32)
        st_ref[...] = jnp.zeros_like(st_ref)

    p = p_ref[0]
    xin = p[:, :qkv_w]
    z = p[:, qkv_w:qkv_w + WIDTH]
    b_in = p[:, qkv_w + WIDTH:qkv_w + WIDTH + N_HEADS]
    a_in = p[:, qkv_w + WIDTH + N_HEADS:qkv_w + WIDTH + 2 * N_HEADS]

    xpad_ref[8:, :] = xin
    conv = xin * cw_ref[GDN_CONV - 1:GDN_CONV, :]
    for s in range(1, GDN_CONV):
        conv = conv + xpad_ref[8 - s:8 - s + tb, :] * cw_ref[GDN_CONV - 1 - s:GDN_CONV - s, :]
    xpad_ref[:8, :] = xin[tb - 8:, :]
    qkv = _silu(conv)

    beta = _sigmoid(b_in)
    g_col = -jnp.exp(alog_r_ref[...]) * _softplus(a_in + dt_r_ref[...])
    gc_col = _chunk_cumsum_rows(g_col, c)
    eye_h = (_iota((N_HEADS, N_HEADS), 0) == _iota((N_HEADS, N_HEADS), 1)).astype(F32)
    a_row = _mm32_nt(eye_h, a_in)
    g_row = -jnp.exp(alog_c_ref[...]) * _softplus(a_row + dt_c_ref[...])
    gc_row = _chunk_cumsum_cols(g_row, c)

    def per_head_cols(y):
        return jnp.stack([y[i * c:(i + 1) * c, h:h + 1] for i in range(nc) for h in range(N_HEADS)])

    def over_head_lanes(y):
        expand = (_iota((N_HEADS, WIDTH), 1) // HEAD_DIM == _iota((N_HEADS, WIDTH), 0)).astype(MXU_DTYPE)
        return _tree_sum([jnp.dot(part, expand, preferred_element_type=F32) for part in _split3(y)])

    gc = per_head_cols(gc_col)
    g_rows = jnp.stack([gc_row[h:h + 1, i * c:(i + 1) * c] for i in range(nc) for h in range(N_HEADS)])
    gc_d = over_head_lanes(gc_col)
    beta_d = over_head_lanes(beta)
    g_last_d = jnp.concatenate([jnp.broadcast_to(gc_d[(i + 1) * c - 1:(i + 1) * c, :], (c, WIDTH)) for i in range(nc)],
                               axis=0)

    seg = _head_segments(WIDTH)
    q = qkv[:, :WIDTH]
    k = qkv[:, WIDTH:2 * WIDTH]
    q = q * lax.rsqrt(_head_sums(q * q, seg) + 1e-6) * (HEAD_DIM ** -0.5)
    k = k * lax.rsqrt(_head_sums(k * k, seg) + 1e-6)
    kb_d = k * beta_d
    e_g = jnp.exp(gc_d)

    st = lambda y: _stack_heads(y, nc, c)
    q_h, k_h, kb = st(q), st(k), st(kb_d)

    ri = _iota((c, c), 0)
    ci = _iota((c, c), 1)
    incl = (ri >= ci)[None]
    strict = (ri > ci)[None]
    decay = jnp.exp(jnp.where(incl, gc - g_rows, NEG_INF))
    mm = _bmm_nt(jnp.concatenate([kb, q_h], axis=1), k_h)
    a_mat = jnp.where(strict, mm[:, :c] * decay, 0.0)
    qk = jnp.where(incl, mm[:, c:] * decay, 0.0)
    rhs = jnp.concatenate([st(qkv[:, 2 * WIDTH:] * beta_d), st(kb_d * e_g)], axis=-1)
    uw = _batched_unit_lower_solve(-a_mat, rhs)
    qkuw = _bmm(qk, uw)
    o0 = qkuw[:, :, :HEAD_DIM]
    oq = st(q * e_g) - qkuw[:, :, HEAD_DIM:]
    kuw = _bmm_tn(st(k * jnp.exp(g_last_d - gc_d)), uw)
    gamma = jnp.exp(gc[:, c - 1:c, :])
    o = _run_chunks(st_ref, -kuw[:, :, HEAD_DIM:], kuw[:, :, :HEAD_DIM], oq, o0, gamma, nc)

    o = _unstack_heads(o, nc)
    o = o * lax.rsqrt(_head_sums(o * o, seg) * (1.0 / HEAD_DIM) + RMS_EPS) * _lane_tile(nw_ref[...], N_HEADS)
    o_ref[0] = o * _silu(z)


def _gdn(p, conv_w, a_log, dt_bias, norm_w, tb=SCAN_BLOCK):
    b, t, cols = p.shape
    tb = min(tb, t)
    assert t % tb == 0 and tb % SCAN_CHUNK == 0 and cols == GDN_COLS
    params = [conv_w.astype(F32), a_log.reshape(1, -1), dt_bias.reshape(1, -1), a_log.reshape(-1, 1),
              dt_bias.reshape(-1, 1), norm_w.reshape(1, -1)]
    const = lambda i, j: (0, 0)
    return pl.pallas_call(
        _gdn_kernel,
        grid=(b, t // tb),
        in_specs=[pl.BlockSpec((1, tb, cols), lambda i, j: (i, j, 0))] + [pl.BlockSpec(q.shape, const) for q in params],
        out_specs=pl.BlockSpec((1, tb, WIDTH), lambda i, j: (i, j, 0)),
        out_shape=jax.ShapeDtypeStruct((b, t, WIDTH), F32),
        scratch_shapes=[pltpu.VMEM((8 + tb, 3 * WIDTH), F32), pltpu.VMEM((N_HEADS, HEAD_DIM, HEAD_DIM), F32)],
        compiler_params=pltpu.CompilerParams(dimension_semantics=("parallel", "arbitrary"),
                                             vmem_limit_bytes=VMEM_LIMIT),
        name="gdn",
    )(p, *params)


def _key_to_float(key):
    return pltpu.bitcast(jnp.where(key >= 0, key, key ^ jnp.int32(0x7FFFFFFF)), F32)


NEG_INF_KEY = -2139095041
COARSE_DTYPE = jnp.bfloat16
SOFTMAX_FLOOR = -1e30


def _dsa_kernel(dq_ref, iq_ref, kv_ref, ikw_ref, iwq_ref, qn_ref, kn_ref, lnw_ref, lnb_ref, o_ref,
                k_s, vt_s, ik_s, score_s, coarse_s, distm_s, *, top_k):
    qb = dq_ref.shape[1]
    t_len = kv_ref.shape[1]
    kc = DSA_KEY_CHUNK
    n_chunks = t_len // kc
    kvw = DSA_KV_HEADS * HEAD_DIM
    i = pl.program_id(1)
    n_full = (i * qb + qb) // kc
    tail = (i * qb + qb) % kc

    @pl.when(i == 0)
    def _():
        kv = kv_ref[0]
        kraw = kv[:, :kvw]
        kn = kraw * lax.rsqrt(_head_sums(kraw * kraw, _head_segments(kvw)) * (1.0 / HEAD_DIM) + RMS_EPS)
        k_s[...] = (kn * _lane_tile(kn_ref[...], DSA_KV_HEADS)).astype(MXU_DTYPE)
        ones_rows = jnp.where(_iota((8, kc), 0) == 0, 1.0, 0.0)
        for c in range(n_chunks):
            v_t = kv[c * kc:(c + 1) * kc, kvw:].T
            vt_s[c] = jnp.concatenate([blk for g in range(DSA_KV_HEADS) for blk in (_head_rows(v_t, g), ones_rows)],
                                      axis=0).astype(MXU_DTYPE)
        ik = ikw_ref[0][:, :IDX_DIM]
        ikc = ik - jnp.mean(ik, axis=-1, keepdims=True)
        ik = ikc * lax.rsqrt(jnp.mean(ikc * ikc, axis=-1, keepdims=True) + 1e-6) * lnw_ref[...] + lnb_ref[...]
        hi, lo = _split_hi_lo(ik)
        ik_s[...] = jnp.concatenate([hi, hi, lo], axis=-1)
        score_s[...] = jnp.full(score_s.shape, NEG_INF, F32)

    t_row = i * qb + _iota((1, qb), 1)

    eye_h = (_iota((IDX_HEADS, IDX_HEADS), 0) == _iota((IDX_HEADS, IDX_HEADS), 1)).astype(F32)
    iw_t = _mm32_nt(eye_h, iwq_ref[0][:, IDX_DIM:IDX_DIM + IDX_HEADS]) * (IDX_HEADS ** -0.5 * IDX_DIM ** -0.5)
    iw_wide = jnp.concatenate([iw_t[h:h + 1, :] for h in range(IDX_HEADS)], axis=1)
    hi, lo = _split_hi_lo(iq_ref[0])
    iq_all = jnp.concatenate([jnp.concatenate([_head(hi, h), _head(lo, h), _head(hi, h)], axis=-1)
                              for h in range(IDX_HEADS)], axis=0)

    def score_rows(start, size):
        rows = pl.ds(pl.multiple_of(start, size), size)
        dots = lax.dot_general(ik_s[rows, :], iq_all, (((1,), (1,)), ((), ())), preferred_element_type=F32)
        w = iw_wide * jnp.maximum(dots, 0.0)
        sc = _tree_sum([w[:, h * qb:(h + 1) * qb] for h in range(IDX_HEADS)])
        sc = jnp.where(start + _iota((size, 1), 0) <= t_row, sc, NEG_INF)
        score_s[rows, :] = sc
        coarse_s[rows, :] = sc.astype(COARSE_DTYPE)

    def over_chunks(fn):
        def body(c, carry):
            fn(c * kc, kc)
            return carry
        lax.fori_loop(0, n_full, body, 0)
        if kc > qb:
            @pl.when(tail > 0)
            def _():
                fn(n_full * kc, qb)

    over_chunks(score_rows)

    kf = float(top_k)
    grp = DSA_COUNT_ROWS
    n_grp = (i * qb + qb + grp - 1) // grp

    def count(pred):
        def body(g, acc):
            blk = score_s[pl.ds(pl.multiple_of(g * grp, grp), grp), :]
            ones = jnp.where(pred(blk), 1.0, 0.0)
            return acc + _tree_sum([ones[r * 8:(r + 1) * 8] for r in range(grp // 8)])
        acc = lax.fori_loop(0, n_grp, body, jnp.zeros((8, qb), F32))
        return jnp.sum(acc, axis=0, keepdims=True)

    def count_coarse(cand):
        def body(g, acc):
            blk = coarse_s[pl.ds(pl.multiple_of(g * grp, grp), grp), :]
            ones = jnp.where(blk >= cand, jnp.ones((), COARSE_DTYPE), jnp.zeros((), COARSE_DTYPE))
            return acc + _tree_sum([ones[r * 16:(r + 1) * 16] for r in range(grp // 16)]).astype(F32)
        acc = lax.fori_loop(0, n_grp, body, jnp.zeros((16, qb), F32))
        return jnp.sum(acc, axis=0, keepdims=True)

    def coarse_key(base):
        return jnp.where(base >= 0, base, base | jnp.int32(0xFFFF))

    int_min = jnp.int32(-2 ** 31)
    base = jnp.where(count_coarse(jnp.zeros((1, qb), COARSE_DTYPE)) >= kf, jnp.int32(0), int_min)

    def coarse_step(b, base):
        cand = base | (jnp.int32(1) << (30 - b))
        cand_f = _key_to_float(coarse_key(cand)).astype(COARSE_DTYPE)
        return jnp.where(count_coarse(cand_f) >= kf, cand, base)

    base = lax.fori_loop(0, 15, coarse_step, base)
    lo_key = coarse_key(base) - jnp.int32(2 ** 15)

    def fine_step(b, off):
        cand_off = off | (jnp.int32(1) << (16 - b))
        cand_f = _key_to_float(lo_key + cand_off)
        return jnp.where(count(lambda blk: blk >= cand_f) >= kf, cand_off, off)

    tau = lo_key + lax.fori_loop(0, 17, fine_step, jnp.zeros((1, qb), jnp.int32))
    tau_f = _key_to_float(jnp.maximum(tau, jnp.int32(NEG_INF_KEY)))

    n_ge = count(lambda blk: blk >= tau_f)
    no_partial_tie = jnp.min(jnp.where((n_ge <= kf) | (tau_f == NEG_INF), 1.0, 0.0)) > 0.5

    def masked_distance(start, size, sel_fn):
        rows = pl.ds(pl.multiple_of(start, size), size)
        s_pos = start + _iota((size, 1), 0)
        sel = sel_fn(score_s[rows, :]) & (s_pos <= t_row)
        distm_s[rows, :] = jnp.where(sel, (t_row - s_pos).astype(F32), jnp.inf)

    @pl.when(no_partial_tie)
    def _():
        over_chunks(lambda start, size: masked_distance(start, size, lambda blk: blk >= tau_f))

    @pl.when(jnp.logical_not(no_partial_tie))
    def _():
        lower = (_iota((kc, kc), 0) >= _iota((kc, kc), 1)).astype(MXU_DTYPE)
        need = kf - count(lambda blk: blk > tau_f)

        def select_rows(start, size, run):
            def sel_fn(blk):
                eq = blk == tau_f
                pref = run + jnp.dot(lower[:size, :size], jnp.where(eq, 1.0, 0.0).astype(MXU_DTYPE),
                                     preferred_element_type=F32)
                return (blk > tau_f) | (eq & (pref <= need))
            masked_distance(start, size, sel_fn)
            rows = pl.ds(pl.multiple_of(start, size), size)
            return run + jnp.sum(jnp.where(score_s[rows, :] == tau_f, 1.0, 0.0), axis=0, keepdims=True)

        run = lax.fori_loop(0, n_full, lambda c, run: select_rows(c * kc, kc, run), jnp.zeros((1, qb), F32))
        if kc > qb:
            @pl.when(tail > 0)
            def _():
                select_rows(n_full * kc, qb, run)

    dq = dq_ref[0]
    dq = dq * lax.rsqrt(_head_sums(dq * dq, _head_segments(WIDTH)) * (1.0 / HEAD_DIM) + RMS_EPS)
    dq = (dq * _lane_tile(qn_ref[...] * (HEAD_DIM ** -0.5 * LOG2_E), N_HEADS)).astype(MXU_DTYPE)
    qs = [_head(dq, h) for h in range(N_HEADS)]
    q_grp = [jnp.concatenate(qs[g * DSA_GROUP:(g + 1) * DSA_GROUP], axis=0) for g in range(DSA_KV_HEADS)]
    slopes = jnp.concatenate([jnp.full((1, qb), DSA_SLOPES[h] * LOG2_E, F32) for h in range(N_HEADS)], axis=1)
    gw = DSA_GROUP * qb

    def attend_rows(start, size, vt, m, acc):
        rows = pl.ds(pl.multiple_of(start, size), size)
        kch = k_s[rows, :]
        sc = jnp.concatenate(
            [lax.dot_general(kch[:, g * HEAD_DIM:(g + 1) * HEAD_DIM], q_grp[g], (((1,), (1,)), ((), ())),
                             preferred_element_type=F32) for g in range(DSA_KV_HEADS)], axis=1)
        sc = sc - slopes * _lane_tile(distm_s[rows, :], N_HEADS)
        m_new = jnp.maximum(m, jnp.max(sc, axis=0, keepdims=True))
        alpha = jnp.exp2(m - m_new)
        eb = jnp.exp2(sc - m_new).astype(MXU_DTYPE)
        pv = jnp.concatenate(
            [jnp.dot(vt[g * VAL_ROWS:(g + 1) * VAL_ROWS, :], eb[:, g * gw:(g + 1) * gw], preferred_element_type=F32)
             for g in range(DSA_KV_HEADS)], axis=1)
        return m_new, alpha * acc + pv

    init = (jnp.full((1, N_HEADS * qb), SOFTMAX_FLOOR, F32), jnp.zeros((VAL_ROWS, N_HEADS * qb), F32))
    m, acc = lax.fori_loop(0, n_full, lambda c, carry: attend_rows(c * kc, kc, vt_s[c], *carry), init)
    if kc > qb:
        m, acc = lax.cond(tail > 0, lambda: attend_rows(n_full * kc, qb, vt_s[n_full][:, :qb], m, acc),
                          lambda: (m, acc))
    l = acc[HEAD_DIM:HEAD_DIM + 1]
    acc = acc[:HEAD_DIM]
    o_t = jnp.concatenate([acc[:, h * qb:(h + 1) * qb] / l[:, h * qb:(h + 1) * qb] for h in range(N_HEADS)], axis=0)
    o_ref[0] = o_t.T


def _dsa(p, q_norm, k_norm, ln_w, ln_b):
    b, t, cols = p.shape
    qb = DSA_Q_BLOCK
    kc = DSA_KEY_CHUNK
    assert t % qb == 0 and cols == DSA_COLS and t % DSA_COUNT_ROWS == 0 and t % kc == 0 and kc % DSA_COUNT_ROWS == 0 and kc in (qb, 2 * qb)
    top_k = min(DSA_TOPK_MAX, t // 4)
    kvw = 2 * DSA_KV_HEADS * HEAD_DIM
    row = lambda v: v.reshape(1, -1).astype(F32)
    params = [row(q_norm), row(k_norm), row(ln_w), row(ln_b)]
    const = lambda i, j: (0, 0)
    return pl.pallas_call(
        functools.partial(_dsa_kernel, top_k=top_k),
        grid=(b, t // qb),
        in_specs=[pl.BlockSpec((1, qb, WIDTH), lambda i, j: (i, j, 0)),
                  pl.BlockSpec((1, qb, WIDTH), lambda i, j: (i, j, 1)),
                  pl.BlockSpec((1, t, kvw), lambda i, j: (i, 0, 2 * WIDTH // kvw)),
                  pl.BlockSpec((1, t, LANES), lambda i, j: (i, 0, (2 * WIDTH + kvw) // LANES)),
                  pl.BlockSpec((1, qb, LANES), lambda i, j: (i, j, (2 * WIDTH + kvw) // LANES))]
        + [pl.BlockSpec(q.shape, const) for q in params],
        out_specs=pl.BlockSpec((1, qb, WIDTH), lambda i, j: (i, j, 0)),
        out_shape=jax.ShapeDtypeStruct((b, t, WIDTH), F32),
        scratch_shapes=[pltpu.VMEM((t, DSA_KV_HEADS * HEAD_DIM), MXU_DTYPE),
                        pltpu.VMEM((t // kc, DSA_KV_HEADS * VAL_ROWS, kc), MXU_DTYPE),
                        pltpu.VMEM((t, 3 * IDX_DIM), MXU_DTYPE),
                        pltpu.VMEM((t, qb), F32),
                        pltpu.VMEM((t, qb), COARSE_DTYPE),
                        pltpu.VMEM((t, qb), F32)],
        compiler_params=pltpu.CompilerParams(dimension_semantics=("parallel", "arbitrary"),
                                             vmem_limit_bytes=VMEM_LIMIT),
        name="dsa",
    )(p, p, p, p, p, *params)


MOBA_PENALTY = 1e30


def _moba_kernel(q_ref, k_ref, v_ref, qn_ref, kn_ref, o_ref, k_s, vt_s, km_s, causal_s):
    bs = q_ref.shape[1]
    t_len = k_ref.shape[1]
    n_kb = t_len // bs
    n_pad = -(-n_kb // 8) * 8
    i = pl.program_id(1)

    @pl.when(i == 0)
    def _():
        kraw = k_ref[0]
        kn = kraw * lax.rsqrt(_head_sums(kraw * kraw, _head_segments(WIDTH)) * (1.0 / HEAD_DIM) + RMS_EPS)
        kn = kn * _lane_tile(kn_ref[...], N_HEADS)
        s_pos = _iota((t_len, 1), 0)
        fcol = _iota((1, HEAD_DIM), 1)
        extra = jnp.where(fcol == n_pad, (s_pos % bs).astype(F32),
                          jnp.where((fcol == n_pad + 1) | (fcol == s_pos // bs), 1.0, 0.0)).astype(MXU_DTYPE)
        for h in range(N_HEADS):
            k_s[:, 2 * h * HEAD_DIM:(2 * h + 1) * HEAD_DIM] = _head(kn, h).astype(MXU_DTYPE)
            k_s[:, (2 * h + 1) * HEAD_DIM:(2 * h + 2) * HEAD_DIM] = extra
        means = jnp.concatenate([jnp.mean(kn[j * bs:(j + 1) * bs], axis=0, keepdims=True) for j in range(n_kb)]
                                + [jnp.zeros((n_pad - n_kb, WIDTH), F32)] * (n_pad > n_kb), axis=0)
        head_of_lane = _iota((1, WIDTH), 1) // HEAD_DIM
        rows = [jnp.where(head_of_lane == h, means, 0.0) for h in range(N_HEADS)]
        rows.append(jnp.zeros((LANES - N_HEADS * n_pad, WIDTH), F32))
        table = jnp.concatenate(rows, axis=0).T
        hi, lo = _split_hi_lo(table)
        km_s[...] = jnp.concatenate([hi, lo, hi], axis=0)
        v = v_ref[0]
        ones_rows = jnp.where(_iota((8, bs), 0) == 0, 1.0, 0.0)
        for j in range(n_kb):
            v_t = v[j * bs:(j + 1) * bs, :].T
            vt_s[j] = jnp.concatenate([blk for h in range(N_HEADS) for blk in (_head_rows(v_t, h), ones_rows)],
                                      axis=0).astype(MXU_DTYPE)
        above = _iota((bs, 1), 0) > _iota((1, bs), 1)
        causal_s[...] = _lane_tile(jnp.where(above, NEG_INF, 0.0), N_HEADS)

    qraw = q_ref[0]
    jcol = _iota((n_pad, 1), 0)
    own = pl.multiple_of(i * bs, bs)
    qn = qraw * lax.rsqrt(_head_sums(qraw * qraw, _head_segments(WIDTH)) * (1.0 / HEAD_DIM) + RMS_EPS)
    qn = qn * _lane_tile(qn_ref[...], N_HEADS)
    q_hi, q_lo = _split_hi_lo(qn)
    gate_qm = jnp.dot(jnp.concatenate([q_hi, q_hi, q_lo], axis=1), km_s[...], preferred_element_type=F32)
    gate_rows = gate_qm.T
    gate = jnp.concatenate([gate_rows[h * n_pad:(h + 1) * n_pad] for h in range(N_HEADS)], axis=1)
    gate = jnp.where(jcol < i, gate, NEG_INF)
    rank = jnp.zeros(gate.shape, F32)
    for j2 in range(n_kb):
        other = gate[j2:j2 + 1, :]
        beats = (other > gate) | ((other == gate) & (j2 < jcol))
        rank = rank + jnp.where(beats, 1.0, 0.0)
    picked = (rank < float(MOBA_TOPK)) & (jcol < i)
    slopes = jnp.concatenate([jnp.full((1, bs), MOBA_SLOPES[h], F32) for h in range(N_HEADS)], axis=1)
    block_term = jnp.where(jcol < i, -(slopes * ((i - jcol) * bs).astype(F32) + jnp.where(picked, 0.0, MOBA_PENALTY)),
                           0.0)
    q_t = (qn * (HEAD_DIM ** -0.5)).T
    frow = _iota((HEAD_DIM - n_pad, 1), 0)
    t_loc = _iota((1, bs), 1).astype(F32)
    q_aug = []
    for h in range(N_HEADS):
        tail = jnp.where(frow == 0, MOBA_SLOPES[h], jnp.where(frow == 1, -MOBA_SLOPES[h] * t_loc, 0.0))
        q_aug.append(jnp.concatenate([q_t[h * HEAD_DIM:(h + 1) * HEAD_DIM], block_term[:, h * bs:(h + 1) * bs], tail],
                                     axis=0).astype(MXU_DTYPE))

    def scores(start):
        return jnp.concatenate(
            [jnp.dot(k_s[pl.ds(start, bs), 2 * h * HEAD_DIM:(2 * h + 2) * HEAD_DIM], q_aug[h],
                     preferred_element_type=F32) for h in range(N_HEADS)], axis=1)

    def weighted_values(j, e):
        vt = vt_s[j]
        eb = e.astype(MXU_DTYPE)
        return jnp.concatenate(
            [jnp.dot(vt[h * VAL_ROWS:(h + 1) * VAL_ROWS, :], eb[:, h * bs:(h + 1) * bs], preferred_element_type=F32)
             for h in range(N_HEADS)], axis=1)

    sc = scores(own) + causal_s[...]
    m0 = jnp.max(sc, axis=0, keepdims=True)
    acc0 = weighted_values(i, jnp.exp(sc - m0))

    def block_step(j, carry):
        m, acc = carry
        start = pl.multiple_of(j * bs, bs)
        again = pl.multiple_of(start + jnp.minimum(i, 0) * bs, bs)
        m_new = jnp.maximum(m, jnp.max(scores(start), axis=0, keepdims=True))
        alpha = jnp.exp(m - m_new)
        return m_new, alpha * acc + weighted_values(j, jnp.exp(scores(again) - m_new))

    _, acc = lax.fori_loop(0, i, block_step, (m0, acc0))
    l = acc[HEAD_DIM:HEAD_DIM + 1]
    acc = acc[:HEAD_DIM]
    o_t = jnp.concatenate([acc[:, h * bs:(h + 1) * bs] / l[:, h * bs:(h + 1) * bs] for h in range(N_HEADS)], axis=0)
    o_ref[0] = o_t.T


def _moba(p, q_norm, k_norm):
    b, t, cols = p.shape
    bs = MOBA_BLOCK
    n_pad = -(-(t // bs) // 8) * 8
    assert t % bs == 0 and cols == MOBA_COLS and n_pad + 2 <= HEAD_DIM and N_HEADS * n_pad <= LANES and bs <= 256
    row = lambda v: v.reshape(1, -1).astype(F32)
    const = lambda i, j: (0, 0)
    return pl.pallas_call(
        _moba_kernel,
        grid=(b, t // bs),
        in_specs=[pl.BlockSpec((1, bs, WIDTH), lambda i, j: (i, j, 0)),
                  pl.BlockSpec((1, t, WIDTH), lambda i, j: (i, 0, 1)),
                  pl.BlockSpec((1, t, WIDTH), lambda i, j: (i, 0, 2)),
                  pl.BlockSpec((1, HEAD_DIM), const), pl.BlockSpec((1, HEAD_DIM), const)],
        out_specs=pl.BlockSpec((1, bs, WIDTH), lambda i, j: (i, j, 0)),
        out_shape=jax.ShapeDtypeStruct((b, t, WIDTH), F32),
        scratch_shapes=[pltpu.VMEM((t, 2 * WIDTH), MXU_DTYPE),
                        pltpu.VMEM((t // bs, N_HEADS * VAL_ROWS, bs), MXU_DTYPE),
                        pltpu.VMEM((3 * WIDTH, LANES), MXU_DTYPE),
                        pltpu.VMEM((bs, N_HEADS * bs), F32)],
        compiler_params=pltpu.CompilerParams(dimension_semantics=("parallel", "arbitrary"),
                                             vmem_limit_bytes=VMEM_LIMIT),
        name="moba",
    )(p, p, p, row(q_norm), row(k_norm))


def _pad_cols(w, n):
    return jnp.pad(w, ((0, 0), (0, n - w.shape[1])))


def _even_layer(x, b, t, norm_g, norm2_g, mlp_w1, mlp_w2, w_in, w_out, mu, w0, w2, a0, a2, g2, k_k, k_a, r_k,
                lnx_w, lnx_b, conv_w, a_log, dt_bias, gdn_norm_w):
    qkv_w = 3 * WIDTH
    w_rwkv = w_in[:, :RWKV_COLS]
    w_g = w_in[:, RWKV_COLS:]
    w_gdn = jnp.concatenate([w_g[:, :qkv_w], w_g[:, qkv_w + 2 * N_HEADS:],
                             _pad_cols(w_g[:, qkv_w:qkv_w + 2 * N_HEADS], LANES)], axis=1)
    p_rwkv, p_gdn = _norm_proj(x, norm_g, [w_rwkv.astype(MXU_DTYPE), w_gdn.astype(MXU_DTYPE)])
    o_a = _rwkv(p_rwkv.reshape(b, t, -1), mu, w0, w2, a0, a2, g2, k_k, k_a, r_k, lnx_w, lnx_b)
    o_b = _gdn(p_gdn.reshape(b, t, -1), conv_w, a_log, dt_bias, gdn_norm_w)
    return _out_mlp(x, o_a.reshape(b * t, -1), o_b.reshape(b * t, -1), w_out, norm2_g, mlp_w1, mlp_w2)


def _odd_layer(x, b, t, norm_g, norm2_g, mlp_w1, mlp_w2, w_in, w_out, dsa_q_norm, dsa_k_norm, idx_ln_w, idx_ln_b,
               moba_q_norm, moba_k_norm):
    kvw = 2 * DSA_KV_HEADS * HEAD_DIM
    sizes = (WIDTH, kvw, IDX_HEADS * IDX_DIM, IDX_DIM, IDX_HEADS, WIDTH, WIDTH, WIDTH)
    offs = [0]
    for s in sizes:
        offs.append(offs[-1] + s)
    dq, dkv, iq, ik, iw, mq, mk, mv = (w_in[:, offs[n]:offs[n + 1]] for n in range(len(sizes)))
    w_dsa = jnp.concatenate([dq, iq, dkv, _pad_cols(jnp.concatenate([ik, iw], axis=1), LANES)], axis=1)
    w_moba = jnp.concatenate([mq, mk, mv], axis=1)
    p_dsa, p_moba = _norm_proj(x, norm_g, [w_dsa.astype(MXU_DTYPE), w_moba.astype(MXU_DTYPE)])
    o_c = _dsa(p_dsa.reshape(b, t, -1), dsa_q_norm, dsa_k_norm, idx_ln_w, idx_ln_b)
    o_d = _moba(p_moba.reshape(b, t, -1), moba_q_norm, moba_k_norm)
    return _out_mlp(x, o_c.reshape(b * t, -1), o_d.reshape(b * t, -1), w_out, norm2_g, mlp_w1, mlp_w2)


def kernel(x, norm1_g, norm2_g, mlp_w1, mlp_w2, ev_w_in, ev_w_out, rwkv_mu, rwkv_w0, rwkv_w2, rwkv_a0, rwkv_a2, rwkv_g2, rwkv_k_k, rwkv_k_a, rwkv_r_k, rwkv_lnx_w, rwkv_lnx_b, gdn_conv_w, gdn_a_log, gdn_dt_bias, gdn_norm_w, od_w_in, od_w_out, dsa_q_norm, dsa_k_norm, idx_k_ln_w, idx_k_ln_b, moba_q_norm, moba_k_norm):
    b, t, d = x.shape
    depth = norm1_g.shape[0]
    h = x.reshape(b * t, d)
    for i in range(depth):
        j = i // 2
        if i % 2 == 0:
            h = _even_layer(h, b, t, norm1_g[i], norm2_g[i], mlp_w1[i], mlp_w2[i], ev_w_in[j], ev_w_out[j],
                            rwkv_mu[j], rwkv_w0[j], rwkv_w2[j], rwkv_a0[j], rwkv_a2[j], rwkv_g2[j], rwkv_k_k[j], rwkv_k_a[j], rwkv_r_k[j],
                            rwkv_lnx_w[j], rwkv_lnx_b[j], gdn_conv_w[j], gdn_a_log[j], gdn_dt_bias[j],
                            gdn_norm_w[j])
        else:
            h = _odd_layer(h, b, t, norm1_g[i], norm2_g[i], mlp_w1[i], mlp_w2[i], od_w_in[j], od_w_out[j],
                           dsa_q_norm[j], dsa_k_norm[j], idx_k_ln_w[j], idx_k_ln_b[j], moba_q_norm[j], moba_k_norm[j])
    return h.reshape(b, t, d)
```

```python
import functools
import math

import jax
import jax.numpy as jnp
from jax import lax
from jax.experimental import pallas as pl
from jax.experimental.pallas import tpu as pltpu

F32 = jnp.float32
MXU_DTYPE = jnp.bfloat16
HIGHEST = lax.Precision.HIGHEST

LANES = 128
VMEM_LIMIT = 56 * 1024 * 1024

HEAD_DIM = 64
N_HEADS = 8
WIDTH = N_HEADS * HEAD_DIM
RMS_EPS = 1e-6
VAL_ROWS = HEAD_DIM + 8

RWKV_DECAY_LORA = 64
RWKV_A_LORA = 64
RWKV_GATE_LORA = 128
RWKV_COLS = 3 * WIDTH + RWKV_DECAY_LORA + RWKV_A_LORA + RWKV_GATE_LORA
RWKV_GN_EPS = 6.4e-4

GDN_CONV = 4
GDN_COLS = 3 * WIDTH + WIDTH + LANES
SCAN_CHUNK = 64
SCAN_BLOCK = 256
GDN_BLOCK = 1024

DSA_KV_HEADS = 2
DSA_GROUP = N_HEADS // DSA_KV_HEADS
IDX_HEADS = 8
IDX_DIM = 64
DSA_TOPK_MAX = 256
DSA_Q_BLOCK = 256
DSA_KEY_CHUNK = 512
DSA_COUNT_ROWS = 256
DSA_COLS = 2 * WIDTH + 2 * DSA_KV_HEADS * HEAD_DIM + LANES

MOBA_BLOCK = 256
MOBA_TOPK = 3
MOBA_COLS = 3 * WIDTH

PROJ_ROWS = 512
MLP_ROWS = 1024
MLP_FF_COLS = 1024

ALIBI_HEADS = 2 * N_HEADS
NEG_INF = float("-inf")
LOG2_E = math.log2(math.e)


def _alibi_slope(i):
    return 2.0 ** (-8.0 * (i + 1) / ALIBI_HEADS)


DSA_SLOPES = tuple(_alibi_slope(2 * h) for h in range(N_HEADS))
MOBA_SLOPES = tuple(_alibi_slope(2 * h + 1) for h in range(N_HEADS))


def _mm(a, b):
    return jnp.dot(a.astype(MXU_DTYPE), b.astype(MXU_DTYPE), preferred_element_type=F32)


def _mm32_nt(a, b):
    return lax.dot_general(a, b, (((1,), (1,)), ((), ())), preferred_element_type=F32, precision=HIGHEST)


def _iota(shape, dim):
    return lax.broadcasted_iota(jnp.int32, shape, dim)


def _sigmoid(x):
    return 1.0 / (1.0 + jnp.exp(-x))


def _silu(x):
    return x * _sigmoid(x)


def _softplus(x):
    return jnp.maximum(x, 0.0) + jnp.log1p(jnp.exp(-jnp.abs(x)))


def _head(x, h):
    return x[:, h * HEAD_DIM:(h + 1) * HEAD_DIM]


def _head_rows(x, h):
    return x[h * HEAD_DIM:(h + 1) * HEAD_DIM]


def _split_hi_lo(x):
    hi = x.astype(MXU_DTYPE)
    lo = (x - hi.astype(F32)).astype(MXU_DTYPE)
    return hi, lo


def _head_segments(width):
    return (_iota((width, width), 0) // HEAD_DIM == _iota((width, width), 1) // HEAD_DIM).astype(MXU_DTYPE)


def _head_sums(x, seg):
    hi, lo = _split_hi_lo(x)
    return jnp.dot(hi, seg, preferred_element_type=F32) + jnp.dot(lo, seg, preferred_element_type=F32)


def _tree_sum(parts):
    parts = list(parts)
    while len(parts) > 1:
        nxt = [parts[a] + parts[a + 1] for a in range(0, len(parts) - 1, 2)]
        if len(parts) % 2:
            nxt.append(parts[-1])
        parts = nxt
    return parts[0]


def _lane_tile(x, n):
    return jnp.concatenate([x] * n, axis=1)


def _norm_proj_kernel(x_ref, g_ref, *refs):
    n = len(refs) // 2
    x = x_ref[...]
    h = x * lax.rsqrt(jnp.mean(x * x, axis=-1, keepdims=True) + RMS_EPS) * g_ref[...]
    h = h.astype(MXU_DTYPE)
    for w_ref, o_ref in zip(refs[:n], refs[n:]):
        o_ref[...] = jnp.dot(h, w_ref[...], preferred_element_type=F32)


def _norm_proj(x, g, ws, tm=PROJ_ROWS):
    n, d = x.shape
    assert n % tm == 0
    const = lambda i: (0, 0)
    return pl.pallas_call(
        _norm_proj_kernel,
        grid=(n // tm,),
        in_specs=[pl.BlockSpec((tm, d), lambda i: (i, 0)), pl.BlockSpec((1, d), const)]
        + [pl.BlockSpec(w.shape, const) for w in ws],
        out_specs=[pl.BlockSpec((tm, w.shape[1]), lambda i: (i, 0)) for w in ws],
        out_shape=[jax.ShapeDtypeStruct((n, w.shape[1]), F32) for w in ws],
        compiler_params=pltpu.CompilerParams(dimension_semantics=("parallel",), vmem_limit_bytes=VMEM_LIMIT),
        name="norm_proj",
    )(x, g.reshape(1, d), *ws)


def _out_mlp_kernel(x_ref, a_ref, b_ref, wa_ref, wb_ref, g_ref, w1_ref, w2_ref, o_ref, h_ref, acc_ref):
    j = pl.program_id(1)

    @pl.when(j == 0)
    def _():
        x = x_ref[...] + _mm(a_ref[...], wa_ref[...]) + _mm(b_ref[...], wb_ref[...])
        h = x * lax.rsqrt(jnp.mean(x * x, axis=-1, keepdims=True) + RMS_EPS) * g_ref[...]
        h_ref[...] = h.astype(MXU_DTYPE)
        acc_ref[...] = x

    u = jnp.maximum(jnp.dot(h_ref[...], w1_ref[...], preferred_element_type=F32), 0.0)
    acc_ref[...] += jnp.dot((u * u).astype(MXU_DTYPE), w2_ref[...], preferred_element_type=F32)

    @pl.when(j == pl.num_programs(1) - 1)
    def _():
        o_ref[...] = acc_ref[...]


def _out_mlp(x, a, b, w_out, g, w1, w2, tm=MLP_ROWS, tf=MLP_FF_COLS):
    n, d = x.shape
    f = w1.shape[1]
    tm = min(tm, n)
    tf = min(tf, f)
    assert n % tm == 0 and f % tf == 0
    wa = w_out[:a.shape[1]].astype(MXU_DTYPE)
    wb = w_out[a.shape[1]:].astype(MXU_DTYPE)
    row = lambda i, j: (i, 0)
    const = lambda i, j: (0, 0)
    return pl.pallas_call(
        _out_mlp_kernel,
        grid=(n // tm, f // tf),
        in_specs=[pl.BlockSpec((tm, d), row), pl.BlockSpec((tm, a.shape[1]), row), pl.BlockSpec((tm, b.shape[1]), row),
                  pl.BlockSpec(wa.shape, const), pl.BlockSpec(wb.shape, const), pl.BlockSpec((1, d), const),
                  pl.BlockSpec((d, tf), lambda i, j: (0, j)), pl.BlockSpec((tf, d), lambda i, j: (j, 0))],
        out_specs=pl.BlockSpec((tm, d), row),
        out_shape=jax.ShapeDtypeStruct((n, d), F32),
        scratch_shapes=[pltpu.VMEM((tm, d), MXU_DTYPE), pltpu.VMEM((tm, d), F32)],
        compiler_params=pltpu.CompilerParams(dimension_semantics=("parallel", "arbitrary"),
                                             vmem_limit_bytes=VMEM_LIMIT),
        name="out_mlp",
    )(x, a, b, wa, wb, g.reshape(1, d), w1.astype(MXU_DTYPE), w2.astype(MXU_DTYPE))


def _bmm(a, b):
    return lax.dot_general(a.astype(MXU_DTYPE), b.astype(MXU_DTYPE), (((2,), (1,)), ((0,), (0,))),
                           preferred_element_type=F32)


def _bmm_nt(a, b):
    return lax.dot_general(a.astype(MXU_DTYPE), b.astype(MXU_DTYPE), (((2,), (2,)), ((0,), (0,))),
                           preferred_element_type=F32)


def _bmm_tn(a, b):
    return lax.dot_general(a.astype(MXU_DTYPE), b.astype(MXU_DTYPE), (((1,), (1,)), ((0,), (0,))),
                           preferred_element_type=F32)


def _stack_heads(x, nc, c):
    return jnp.stack([x[i * c:(i + 1) * c, h * HEAD_DIM:(h + 1) * HEAD_DIM] for i in range(nc) for h in range(N_HEADS)])


def _unstack_heads(y, nc):
    return jnp.concatenate([jnp.concatenate([y[i * N_HEADS + h] for h in range(N_HEADS)], axis=-1)
                            for i in range(nc)], axis=0)


def _batched_unit_lower_solve(x, y):
    c = x.shape[1]
    y = y + _bmm(x, y)
    p = 2
    while p < c:
        x = _bmm(x, x)
        y = y + _bmm(x, y)
        p *= 2
    return y


def _split3(x):
    hi = x.astype(MXU_DTYPE)
    r = x - hi.astype(F32)
    mid = r.astype(MXU_DTYPE)
    lo = (r - mid.astype(F32)).astype(MXU_DTYPE)
    return hi, mid, lo


def _chunk_cumsum_rows(x, c):
    tri = (_iota((c, c), 0) >= _iota((c, c), 1)).astype(MXU_DTYPE)
    parts = _split3(x)
    return jnp.concatenate(
        [_tree_sum([jnp.dot(tri, p[i * c:(i + 1) * c], preferred_element_type=F32) for p in parts])
         for i in range(x.shape[0] // c)], axis=0)


def _chunk_cumsum_cols(x, c):
    tri = (_iota((c, c), 0) <= _iota((c, c), 1)).astype(MXU_DTYPE)
    parts = _split3(x)
    return jnp.concatenate(
        [_tree_sum([jnp.dot(p[:, i * c:(i + 1) * c], tri, preferred_element_type=F32) for p in parts])
         for i in range(x.shape[1] // c)], axis=1)


def _run_chunks(st_ref, m, q, oq, o0, gamma, nc):
    s = st_ref[...]
    outs = []
    for i in range(nc):
        g = slice(i * N_HEADS, (i + 1) * N_HEADS)
        mo = _bmm(jnp.concatenate([m[g], oq[g]], axis=1), s)
        outs.append(mo[:, HEAD_DIM:] + o0[g])
        s = gamma[g] * s + mo[:, :HEAD_DIM] + q[g]
    st_ref[...] = s
    return jnp.concatenate(outs, axis=0)


def _rwkv_kernel(p_ref, mu_ref, w0_ref, w2_ref, a0_ref, a2_ref, g2_ref, kk_ref, ka_ref, rk_ref, lnw_ref, lnb_ref,
                 o_ref, prev_ref, st_ref):
    tb = p_ref.shape[1]
    c = SCAN_CHUNK
    nc = tb // c

    @pl.when(pl.program_id(1) == 0)
    def _():
        prev_ref[...] = jnp.zeros_like(prev_ref)
        st_ref[...] = jnp.zeros_like(st_ref)

    p = p_ref[0]
    shifted = jnp.where(_iota((tb, 1), 0) == 0, prev_ref[...], pltpu.roll(p, 1, 0))
    prev_ref[...] = p[tb - 1:tb, :]
    x = p + (shifted - p) * mu_ref[...]

    r = x[:, 0:WIDTH]
    k = x[:, WIDTH:2 * WIDTH]
    v = x[:, 2 * WIDTH:3 * WIDTH]
    off = 3 * WIDTH
    wd = x[:, off:off + RWKV_DECAY_LORA]
    ad = x[:, off + RWKV_DECAY_LORA:off + RWKV_DECAY_LORA + RWKV_A_LORA]
    gd = x[:, off + RWKV_DECAY_LORA + RWKV_A_LORA:]

    w_pre = w0_ref[...] + _mm(jnp.tanh(wd), w2_ref[...])
    log_w = -(_sigmoid(w_pre) * math.exp(-0.5))
    a = _sigmoid(a0_ref[...] + _mm(ad, a2_ref[...]))
    g = _mm(_sigmoid(gd), g2_ref[...])
    kk_all = k * kk_ref[...]
    k = k * (1.0 + (a - 1.0) * ka_ref[...])

    gam = _chunk_cumsum_rows(log_w, c)
    gam_last = jnp.concatenate([jnp.broadcast_to(gam[(i + 1) * c - 1:(i + 1) * c, :], (c, WIDTH)) for i in range(nc)],
                               axis=0)

    seg = _head_segments(WIDTH)
    kk = kk_all * lax.rsqrt(_head_sums(kk_all * kk_all, seg) + 1e-6)
    b = kk * a
    e_neg = jnp.exp(-gam)
    tail = jnp.exp(gam_last - gam)

    st = lambda y: _stack_heads(y, nc, c)
    st_mxu = lambda y: st(y.astype(MXU_DTYPE))
    v_h = st_mxu(v)
    a_t = st(-kk * jnp.exp(gam - log_w))
    r_t = st(r * jnp.exp(gam))
    b_t = st_mxu(b * e_neg)
    k_t = st_mxu(k * e_neg)
    e_last = st(jnp.exp(gam_last))[:, 0:1, :]

    eye = (_iota((c, c), 0) == _iota((c, c), 1))[None]

    inter = _bmm_nt(jnp.concatenate([a_t, r_t], axis=1), jnp.concatenate([b_t, k_t], axis=1))
    row2 = _iota((c, 2 * c), 0)
    col2 = _iota((c, 2 * c), 1)
    col2 = jnp.where(col2 >= c, col2 - c, col2)
    top = jnp.where((row2 > col2)[None], inter[:, :c], 0.0)
    bot = jnp.where((row2 >= col2)[None], inter[:, c:], 0.0)
    akv = _bmm(top[:, :, c:], v_h)
    wu = _batched_unit_lower_solve(top[:, :, :c], jnp.concatenate([a_t, akv], axis=-1))
    wu_v = jnp.concatenate([wu.astype(MXU_DTYPE), jnp.concatenate([jnp.zeros_like(v_h), v_h], axis=-1)], axis=1)
    ro = _bmm(bot, wu_v)
    oq = r_t + ro[:, :, :HEAD_DIM]
    o0 = ro[:, :, HEAD_DIM:]
    mq = _bmm_tn(jnp.concatenate([st_mxu(b * tail), st_mxu(k * tail)], axis=1), wu_v)
    m = mq[:, :, :HEAD_DIM]
    q = mq[:, :, HEAD_DIM:]
    gamma = jnp.sum(jnp.where(eye, e_last, 0.0), axis=2, keepdims=True)

    o = _unstack_heads(_run_chunks(st_ref, m, q, oq, o0, gamma, nc), nc)

    oc = o - _head_sums(o, seg) * (1.0 / HEAD_DIM)
    on = oc * lax.rsqrt(_head_sums(oc * oc, seg) * (1.0 / HEAD_DIM) + RWKV_GN_EPS)
    bonus = _head_sums(r * k * rk_ref[...], seg) * v
    o_ref[0] = (on * lnw_ref[...] + lnb_ref[...] + bonus) * g


def _rwkv(p, mu, w0, w2, a0, a2, g2, k_k, k_a, r_k, lnx_w, lnx_b, tb=SCAN_BLOCK):
    b, t, cols = p.shape
    tb = min(tb, t)
    assert t % tb == 0 and tb % SCAN_CHUNK == 0 and cols == RWKV_COLS
    row = lambda v: v.reshape(1, -1).astype(F32)
    params = [row(mu), row(w0), w2.astype(MXU_DTYPE), row(a0), a2.astype(MXU_DTYPE), g2.astype(MXU_DTYPE),
              row(k_k), row(k_a), row(r_k), row(lnx_w), row(lnx_b)]
    const = lambda i, j: (0, 0)
    return pl.pallas_call(
        _rwkv_kernel,
        grid=(b, t // tb),
        in_specs=[pl.BlockSpec((1, tb, cols), lambda i, j: (i, j, 0))] + [pl.BlockSpec(q.shape, const) for q in params],
        out_specs=pl.BlockSpec((1, tb, WIDTH), lambda i, j: (i, j, 0)),
        out_shape=jax.ShapeDtypeStruct((b, t, WIDTH), F32),
        scratch_shapes=[pltpu.VMEM((1, cols), F32), pltpu.VMEM((N_HEADS, HEAD_DIM, HEAD_DIM), F32)],
        compiler_params=pltpu.CompilerParams(dimension_semantics=("parallel", "arbitrary"),
                                             vmem_limit_bytes=VMEM_LIMIT),
        name="rwkv7",
    )(p, *params)


def _gdn_kernel(p_ref, cw_ref, alog_r_ref, dt_r_ref, alog_c_ref, dt_c_ref, nw_ref, o_ref, xpad_ref, st_ref):
    tb = p_ref.shape[1]
    c = SCAN_CHUNK
    nc = tb // c
    qkv_w = 3 * WIDTH

    @pl.when(pl.program_id(1) == 0)
    def _():
        xpad_ref[:8, :] = jnp.zeros((8, xpad_ref.shape[1]), F32)
        st_ref[...] = jnp.zeros_like(st_ref)

    p = p_ref[0]
    xin = p[:, :qkv_w]
    z = p[:, qkv_w:qkv_w + WIDTH]
    b_in = p[:, qkv_w + WIDTH:qkv_w + WIDTH + N_HEADS]
    a_in = p[:, qkv_w + WIDTH + N_HEADS:qkv_w + WIDTH + 2 * N_HEADS]

    xpad_ref[8:, :] = xin
    conv = xin * cw_ref[GDN_CONV - 1:GDN_CONV, :]
    for s in range(1, GDN_CONV):
        conv = conv + xpad_ref[8 - s:8 - s + tb, :] * cw_ref[GDN_CONV - 1 - s:GDN_CONV - s, :]
    xpad_ref[:8, :] = xin[tb - 8:, :]
    qkv = _silu(conv)

    beta = _sigmoid(b_in)
    g_col = -jnp.exp(alog_r_ref[...]) * _softplus(a_in + dt_r_ref[...])
    gc_col = _chunk_cumsum_rows(g_col, c)
    eye_h = (_iota((N_HEADS, N_HEADS), 0) == _iota((N_HEADS, N_HEADS), 1)).astype(F32)
    a_row = _mm32_nt(eye_h, a_in)
    g_row = -jnp.exp(alog_c_ref[...]) * _softplus(a_row + dt_c_ref[...])
    gc_row = _chunk_cumsum_cols(g_row, c)

    def per_head_cols(y):
        return jnp.stack([y[i * c:(i + 1) * c, h:h + 1] for i in range(nc) for h in range(N_HEADS)])

    def over_head_lanes(y):
        expand = (_iota((N_HEADS, WIDTH), 1) // HEAD_DIM == _iota((N_HEADS, WIDTH), 0)).astype(MXU_DTYPE)
        return _tree_sum([jnp.dot(part, expand, preferred_element_type=F32) for part in _split3(y)])

    gc = per_head_cols(gc_col)
    g_rows = jnp.stack([gc_row[h:h + 1, i * c:(i + 1) * c] for i in range(nc) for h in range(N_HEADS)])
    gc_d = over_head_lanes(gc_col)
    beta_d = over_head_lanes(beta)
    g_last_d = jnp.concatenate([jnp.broadcast_to(gc_d[(i + 1) * c - 1:(i + 1) * c, :], (c, WIDTH)) for i in range(nc)],
                               axis=0)

    seg = _head_segments(WIDTH)
    q = qkv[:, :WIDTH]
    k = qkv[:, WIDTH:2 * WIDTH]
    q = q * lax.rsqrt(_head_sums(q * q, seg) + 1e-6) * (HEAD_DIM ** -0.5)
    k = k * lax.rsqrt(_head_sums(k * k, seg) + 1e-6)
    kb_d = k * beta_d
    e_g = jnp.exp(gc_d)

    st = lambda y: _stack_heads(y, nc, c)
    q_h, k_h, kb = st(q), st(k), st(kb_d)

    ri = _iota((c, c), 0)
    ci = _iota((c, c), 1)
    incl = (ri >= ci)[None]
    strict = (ri > ci)[None]
    decay = jnp.exp(jnp.where(incl, gc - g_rows, NEG_INF))
    mm = _bmm_nt(jnp.concatenate([kb, q_h], axis=1), k_h)
    a_mat = jnp.where(strict, mm[:, :c] * decay, 0.0)
    qk = jnp.where(incl, mm[:, c:] * decay, 0.0)
    rhs = jnp.concatenate([st(qkv[:, 2 * WIDTH:] * beta_d), st(kb_d * e_g)], axis=-1)
    uw = _batched_unit_lower_solve(-a_mat, rhs)
    qkuw = _bmm(qk, uw)
    o0 = qkuw[:, :, :HEAD_DIM]
    oq = st(q * e_g) - qkuw[:, :, HEAD_DIM:]
    kuw = _bmm_tn(st(k * jnp.exp(g_last_d - gc_d)), uw)
    gamma = jnp.exp(gc[:, c - 1:c, :])
    o = _run_chunks(st_ref, -kuw[:, :, HEAD_DIM:], kuw[:, :, :HEAD_DIM], oq, o0, gamma, nc)

    o = _unstack_heads(o, nc)
    o = o * lax.rsqrt(_head_sums(o * o, seg) * (1.0 / HEAD_DIM) + RMS_EPS) * _lane_tile(nw_ref[...], N_HEADS)
    o_ref[0] = o * _silu(z)


def _gdn(p, conv_w, a_log, dt_bias, norm_w, tb=GDN_BLOCK):
    b, t, cols = p.shape
    tb = min(tb, t)
    assert t % tb == 0 and tb % SCAN_CHUNK == 0 and cols == GDN_COLS
    params = [conv_w.astype(F32), a_log.reshape(1, -1), dt_bias.reshape(1, -1), a_log.reshape(-1, 1),
              dt_bias.reshape(-1, 1), norm_w.reshape(1, -1)]
    const = lambda i, j: (0, 0)
    return pl.pallas_call(
        _gdn_kernel,
        grid=(b, t // tb),
        in_specs=[pl.BlockSpec((1, tb, cols), lambda i, j: (i, j, 0))] + [pl.BlockSpec(q.shape, const) for q in params],
        out_specs=pl.BlockSpec((1, tb, WIDTH), lambda i, j: (i, j, 0)),
        out_shape=jax.ShapeDtypeStruct((b, t, WIDTH), F32),
        scratch_shapes=[pltpu.VMEM((8 + tb, 3 * WIDTH), F32), pltpu.VMEM((N_HEADS, HEAD_DIM, HEAD_DIM), F32)],
        compiler_params=pltpu.CompilerParams(dimension_semantics=("parallel", "arbitrary"),
                                             vmem_limit_bytes=VMEM_LIMIT),
        name="gdn",
    )(p, *params)


def _key_to_float(key):
    return pltpu.bitcast(jnp.where(key >= 0, key, key ^ jnp.int32(0x7FFFFFFF)), F32)


NEG_INF_KEY = -2139095041
COARSE_DTYPE = jnp.bfloat16
SOFTMAX_FLOOR = -1e30


def _dsa_kernel(dq_ref, iq_ref, kv_ref, ikw_ref, iwq_ref, qn_ref, kn_ref, lnw_ref, lnb_ref, o_ref,
                k_s, vt_s, ik_s, score_s, coarse_s, distm_s, *, top_k):
    qb = dq_ref.shape[1]
    t_len = kv_ref.shape[1]
    kc = DSA_KEY_CHUNK
    n_chunks = t_len // kc
    kvw = DSA_KV_HEADS * HEAD_DIM
    i = pl.program_id(1)
    n_full = (i * qb + qb) // kc
    tail = (i * qb + qb) % kc

    @pl.when(i == 0)
    def _():
        kv = kv_ref[0]
        kraw = kv[:, :kvw]
        kn = kraw * lax.rsqrt(_head_sums(kraw * kraw, _head_segments(kvw)) * (1.0 / HEAD_DIM) + RMS_EPS)
        k_s[...] = (kn * _lane_tile(kn_ref[...], DSA_KV_HEADS)).astype(MXU_DTYPE)
        ones_rows = jnp.where(_iota((8, kc), 0) == 0, 1.0, 0.0)
        for c in range(n_chunks):
            v_t = kv[c * kc:(c + 1) * kc, kvw:].T
            vt_s[c] = jnp.concatenate([blk for g in range(DSA_KV_HEADS) for blk in (_head_rows(v_t, g), ones_rows)],
                                      axis=0).astype(MXU_DTYPE)
        ik = ikw_ref[0][:, :IDX_DIM]
        ikc = ik - jnp.mean(ik, axis=-1, keepdims=True)
        ik = ikc * lax.rsqrt(jnp.mean(ikc * ikc, axis=-1, keepdims=True) + 1e-6) * lnw_ref[...] + lnb_ref[...]
        hi, lo = _split_hi_lo(ik)
        ik_s[...] = jnp.concatenate([hi, hi, lo], axis=-1)
        score_s[...] = jnp.full(score_s.shape, NEG_INF, F32)

    t_row = i * qb + _iota((1, qb), 1)

    eye_h = (_iota((IDX_HEADS, IDX_HEADS), 0) == _iota((IDX_HEADS, IDX_HEADS), 1)).astype(F32)
    iw_t = _mm32_nt(eye_h, iwq_ref[0][:, IDX_DIM:IDX_DIM + IDX_HEADS]) * (IDX_HEADS ** -0.5 * IDX_DIM ** -0.5)
    iw_wide = jnp.concatenate([iw_t[h:h + 1, :] for h in range(IDX_HEADS)], axis=1)
    hi, lo = _split_hi_lo(iq_ref[0])
    iq_all = jnp.concatenate([jnp.concatenate([_head(hi, h), _head(lo, h), _head(hi, h)], axis=-1)
                              for h in range(IDX_HEADS)], axis=0)

    def score_rows(start, size):
        rows = pl.ds(pl.multiple_of(start, size), size)
        dots = lax.dot_general(ik_s[rows, :], iq_all, (((1,), (1,)), ((), ())), preferred_element_type=F32)
        w = iw_wide * jnp.maximum(dots, 0.0)
        sc = _tree_sum([w[:, h * qb:(h + 1) * qb] for h in range(IDX_HEADS)])
        sc = jnp.where(start + _iota((size, 1), 0) <= t_row, sc, NEG_INF)
        score_s[rows, :] = sc
        coarse_s[rows, :] = sc.astype(COARSE_DTYPE)

    def over_chunks(fn):
        def body(c, carry):
            fn(c * kc, kc)
            return carry
        lax.fori_loop(0, n_full, body, 0)
        if kc > qb:
            @pl.when(tail > 0)
            def _():
                fn(n_full * kc, qb)

    over_chunks(score_rows)

    kf = float(top_k)
    grp = DSA_COUNT_ROWS
    n_grp = (i * qb + qb + grp - 1) // grp

    def count(pred):
        def body(g, acc):
            blk = score_s[pl.ds(pl.multiple_of(g * grp, grp), grp), :]
            ones = jnp.where(pred(blk), 1.0, 0.0)
            return acc + _tree_sum([ones[r * 8:(r + 1) * 8] for r in range(grp // 8)])
        acc = lax.fori_loop(0, n_grp, body, jnp.zeros((8, qb), F32))
        return jnp.sum(acc, axis=0, keepdims=True)

    def count_coarse(cand):
        def body(g, acc):
            blk = coarse_s[pl.ds(pl.multiple_of(g * grp, grp), grp), :]
            ones = jnp.where(blk >= cand, jnp.ones((), COARSE_DTYPE), jnp.zeros((), COARSE_DTYPE))
            return acc + _tree_sum([ones[r * 16:(r + 1) * 16] for r in range(grp // 16)]).astype(F32)
        acc = lax.fori_loop(0, n_grp, body, jnp.zeros((16, qb), F32))
        return jnp.sum(acc, axis=0, keepdims=True)

    def coarse_key(base):
        return jnp.where(base >= 0, base, base | jnp.int32(0xFFFF))

    int_min = jnp.int32(-2 ** 31)
    base = jnp.where(count_coarse(jnp.zeros((1, qb), COARSE_DTYPE)) >= kf, jnp.int32(0), int_min)

    def coarse_step(b, base):
        cand = base | (jnp.int32(1) << (30 - b))
        cand_f = _key_to_float(coarse_key(cand)).astype(COARSE_DTYPE)
        return jnp.where(count_coarse(cand_f) >= kf, cand, base)

    base = lax.fori_loop(0, 15, coarse_step, base)
    lo_key = coarse_key(base) - jnp.int32(2 ** 15)

    def fine_step(b, off):
        cand_off = off | (jnp.int32(1) << (16 - b))
        cand_f = _key_to_float(lo_key + cand_off)
        return jnp.where(count(lambda blk: blk >= cand_f) >= kf, cand_off, off)

    tau = lo_key + lax.fori_loop(0, 17, fine_step, jnp.zeros((1, qb), jnp.int32))
    tau_f = _key_to_float(jnp.maximum(tau, jnp.int32(NEG_INF_KEY)))

    n_ge = count(lambda blk: blk >= tau_f)
    no_partial_tie = jnp.min(jnp.where((n_ge <= kf) | (tau_f == NEG_INF), 1.0, 0.0)) > 0.5

    def masked_distance(start, size, sel_fn):
        rows = pl.ds(pl.multiple_of(start, size), size)
        s_pos = start + _iota((size, 1), 0)
        sel = sel_fn(score_s[rows, :]) & (s_pos <= t_row)
        distm_s[rows, :] = jnp.where(sel, (t_row - s_pos).astype(F32), jnp.inf)

    @pl.when(no_partial_tie)
    def _():
        over_chunks(lambda start, size: masked_distance(start, size, lambda blk: blk >= tau_f))

    @pl.when(jnp.logical_not(no_partial_tie))
    def _():
        lower = (_iota((kc, kc), 0) >= _iota((kc, kc), 1)).astype(MXU_DTYPE)
        need = kf - count(lambda blk: blk > tau_f)

        def select_rows(start, size, run):
            def sel_fn(blk):
                eq = blk == tau_f
                pref = run + jnp.dot(lower[:size, :size], jnp.where(eq, 1.0, 0.0).astype(MXU_DTYPE),
                                     preferred_element_type=F32)
                return (blk > tau_f) | (eq & (pref <= need))
            masked_distance(start, size, sel_fn)
            rows = pl.ds(pl.multiple_of(start, size), size)
            return run + jnp.sum(jnp.where(score_s[rows, :] == tau_f, 1.0, 0.0), axis=0, keepdims=True)

        run = lax.fori_loop(0, n_full, lambda c, run: select_rows(c * kc, kc, run), jnp.zeros((1, qb), F32))
        if kc > qb:
            @pl.when(tail > 0)
            def _():
                select_rows(n_full * kc, qb, run)

    dq = dq_ref[0]
    dq = dq * lax.rsqrt(_head_sums(dq * dq, _head_segments(WIDTH)) * (1.0 / HEAD_DIM) + RMS_EPS)
    dq = (dq * _lane_tile(qn_ref[...] * (HEAD_DIM ** -0.5 * LOG2_E), N_HEADS)).astype(MXU_DTYPE)
    qs = [_head(dq, h) for h in range(N_HEADS)]
    q_grp = [jnp.concatenate(qs[g * DSA_GROUP:(g + 1) * DSA_GROUP], axis=0) for g in range(DSA_KV_HEADS)]
    slopes = jnp.concatenate([jnp.full((1, qb), DSA_SLOPES[h] * LOG2_E, F32) for h in range(N_HEADS)], axis=1)
    gw = DSA_GROUP * qb

    def attend_rows(start, size, vt, m, acc):
        rows = pl.ds(pl.multiple_of(start, size), size)
        kch = k_s[rows, :]
        sc = jnp.concatenate(
            [lax.dot_general(kch[:, g * HEAD_DIM:(g + 1) * HEAD_DIM], q_grp[g], (((1,), (1,)), ((), ())),
                             preferred_element_type=F32) for g in range(DSA_KV_HEADS)], axis=1)
        sc = sc - slopes * _lane_tile(distm_s[rows, :], N_HEADS)
        m_new = jnp.maximum(m, jnp.max(sc, axis=0, keepdims=True))
        alpha = jnp.exp2(m - m_new)
        eb = jnp.exp2(sc - m_new).astype(MXU_DTYPE)
        pv = jnp.concatenate(
            [jnp.dot(vt[g * VAL_ROWS:(g + 1) * VAL_ROWS, :], eb[:, g * gw:(g + 1) * gw], preferred_element_type=F32)
             for g in range(DSA_KV_HEADS)], axis=1)
        return m_new, alpha * acc + pv

    init = (jnp.full((1, N_HEADS * qb), SOFTMAX_FLOOR, F32), jnp.zeros((VAL_ROWS, N_HEADS * qb), F32))
    m, acc = lax.fori_loop(0, n_full, lambda c, carry: attend_rows(c * kc, kc, vt_s[c], *carry), init)
    if kc > qb:
        m, acc = lax.cond(tail > 0, lambda: attend_rows(n_full * kc, qb, vt_s[n_full][:, :qb], m, acc),
                          lambda: (m, acc))
    l = acc[HEAD_DIM:HEAD_DIM + 1]
    acc = acc[:HEAD_DIM]
    o_t = jnp.concatenate([acc[:, h * qb:(h + 1) * qb] / l[:, h * qb:(h + 1) * qb] for h in range(N_HEADS)], axis=0)
    o_ref[0] = o_t.T


def _dsa(p, q_norm, k_norm, ln_w, ln_b):
    b, t, cols = p.shape
    qb = DSA_Q_BLOCK
    kc = DSA_KEY_CHUNK
    assert t % qb == 0 and cols == DSA_COLS and t % DSA_COUNT_ROWS == 0 and t % kc == 0 and kc % DSA_COUNT_ROWS == 0 and kc in (qb, 2 * qb)
    top_k = min(DSA_TOPK_MAX, t // 4)
    kvw = 2 * DSA_KV_HEADS * HEAD_DIM
    row = lambda v: v.reshape(1, -1).astype(F32)
    params = [row(q_norm), row(k_norm), row(ln_w), row(ln_b)]
    const = lambda i, j: (0, 0)
    return pl.pallas_call(
        functools.partial(_dsa_kernel, top_k=top_k),
        grid=(b, t // qb),
        in_specs=[pl.BlockSpec((1, qb, WIDTH), lambda i, j: (i, j, 0)),
                  pl.BlockSpec((1, qb, WIDTH), lambda i, j: (i, j, 1)),
                  pl.BlockSpec((1, t, kvw), lambda i, j: (i, 0, 2 * WIDTH // kvw)),
                  pl.BlockSpec((1, t, LANES), lambda i, j: (i, 0, (2 * WIDTH + kvw) // LANES)),
                  pl.BlockSpec((1, qb, LANES), lambda i, j: (i, j, (2 * WIDTH + kvw) // LANES))]
        + [pl.BlockSpec(q.shape, const) for q in params],
        out_specs=pl.BlockSpec((1, qb, WIDTH), lambda i, j: (i, j, 0)),
        out_shape=jax.ShapeDtypeStruct((b, t, WIDTH), F32),
        scratch_shapes=[pltpu.VMEM((t, DSA_KV_HEADS * HEAD_DIM), MXU_DTYPE),
                        pltpu.VMEM((t // kc, DSA_KV_HEADS * VAL_ROWS, kc), MXU_DTYPE),
                        pltpu.VMEM((t, 3 * IDX_DIM), MXU_DTYPE),
                        pltpu.VMEM((t, qb), F32),
                        pltpu.VMEM((t, qb), COARSE_DTYPE),
                        pltpu.VMEM((t, qb), F32)],
        compiler_params=pltpu.CompilerParams(dimension_semantics=("parallel", "arbitrary"),
                                             vmem_limit_bytes=VMEM_LIMIT),
        name="dsa",
    )(p, p, p, p, p, *params)


MOBA_PENALTY = 1e30


def _moba_kernel(q_ref, k_ref, v_ref, qn_ref, kn_ref, o_ref, k_s, vt_s, km_s, causal_s):
    bs = q_ref.shape[1]
    t_len = k_ref.shape[1]
    n_kb = t_len // bs
    n_pad = -(-n_kb // 8) * 8
    i = pl.program_id(1)

    @pl.when(i == 0)
    def _():
        kraw = k_ref[0]
        kn = kraw * lax.rsqrt(_head_sums(kraw * kraw, _head_segments(WIDTH)) * (1.0 / HEAD_DIM) + RMS_EPS)
        kn = kn * _lane_tile(kn_ref[...], N_HEADS)
        s_pos = _iota((t_len, 1), 0)
        fcol = _iota((1, HEAD_DIM), 1)
        extra = jnp.where(fcol == n_pad, (s_pos % bs).astype(F32),
                          jnp.where((fcol == n_pad + 1) | (fcol == s_pos // bs), 1.0, 0.0)).astype(MXU_DTYPE)
        for h in range(N_HEADS):
            k_s[:, 2 * h * HEAD_DIM:(2 * h + 1) * HEAD_DIM] = _head(kn, h).astype(MXU_DTYPE)
            k_s[:, (2 * h + 1) * HEAD_DIM:(2 * h + 2) * HEAD_DIM] = extra
        means = jnp.concatenate([jnp.mean(kn[j * bs:(j + 1) * bs], axis=0, keepdims=True) for j in range(n_kb)]
                                + [jnp.zeros((n_pad - n_kb, WIDTH), F32)] * (n_pad > n_kb), axis=0)
        head_of_lane = _iota((1, WIDTH), 1) // HEAD_DIM
        rows = [jnp.where(head_of_lane == h, means, 0.0) for h in range(N_HEADS)]
        rows.append(jnp.zeros((LANES - N_HEADS * n_pad, WIDTH), F32))
        table = jnp.concatenate(rows, axis=0).T
        hi, lo = _split_hi_lo(table)
        km_s[...] = jnp.concatenate([hi, lo, hi], axis=0)
        v = v_ref[0]
        ones_rows = jnp.where(_iota((8, bs), 0) == 0, 1.0, 0.0)
        for j in range(n_kb):
            v_t = v[j * bs:(j + 1) * bs, :].T
            vt_s[j] = jnp.concatenate([blk for h in range(N_HEADS) for blk in (_head_rows(v_t, h), ones_rows)],
                                      axis=0).astype(MXU_DTYPE)
        above = _iota((bs, 1), 0) > _iota((1, bs), 1)
        causal_s[...] = _lane_tile(jnp.where(above, NEG_INF, 0.0), N_HEADS)

    qraw = q_ref[0]
    jcol = _iota((n_pad, 1), 0)
    own = pl.multiple_of(i * bs, bs)
    qn = qraw * lax.rsqrt(_head_sums(qraw * qraw, _head_segments(WIDTH)) * (1.0 / HEAD_DIM) + RMS_EPS)
    qn = qn * _lane_tile(qn_ref[...], N_HEADS)
    q_hi, q_lo = _split_hi_lo(qn)
    gate_qm = jnp.dot(jnp.concatenate([q_hi, q_hi, q_lo], axis=1), km_s[...], preferred_element_type=F32)
    gate_rows = gate_qm.T
    gate = jnp.concatenate([gate_rows[h * n_pad:(h + 1) * n_pad] for h in range(N_HEADS)], axis=1)
    gate = jnp.where(jcol < i, gate, NEG_INF)
    rank = jnp.zeros(gate.shape, F32)
    for j2 in range(n_kb):
        other = gate[j2:j2 + 1, :]
        beats = (other > gate) | ((other == gate) & (j2 < jcol))
        rank = rank + jnp.where(beats, 1.0, 0.0)
    picked = (rank < float(MOBA_TOPK)) & (jcol < i)
    slopes = jnp.concatenate([jnp.full((1, bs), MOBA_SLOPES[h], F32) for h in range(N_HEADS)], axis=1)
    block_term = jnp.where(jcol < i, -(slopes * ((i - jcol) * bs).astype(F32) + jnp.where(picked, 0.0, MOBA_PENALTY)),
                           0.0)
    q_t = (qn * (HEAD_DIM ** -0.5)).T
    frow = _iota((HEAD_DIM - n_pad, 1), 0)
    t_loc = _iota((1, bs), 1).astype(F32)
    q_aug = []
    for h in range(N_HEADS):
        tail = jnp.where(frow == 0, MOBA_SLOPES[h], jnp.where(frow == 1, -MOBA_SLOPES[h] * t_loc, 0.0))
        q_aug.append(jnp.concatenate([q_t[h * HEAD_DIM:(h + 1) * HEAD_DIM], block_term[:, h * bs:(h + 1) * bs], tail],
                                     axis=0).astype(MXU_DTYPE))

    def scores(start):
        return jnp.concatenate(
            [jnp.dot(k_s[pl.ds(start, bs), 2 * h * HEAD_DIM:(2 * h + 2) * HEAD_DIM], q_aug[h],
                     preferred_element_type=F32) for h in range(N_HEADS)], axis=1)

    def weighted_values(j, e):
        vt = vt_s[j]
        eb = e.astype(MXU_DTYPE)
        return jnp.concatenate(
            [jnp.dot(vt[h * VAL_ROWS:(h + 1) * VAL_ROWS, :], eb[:, h * bs:(h + 1) * bs], preferred_element_type=F32)
             for h in range(N_HEADS)], axis=1)

    sc = scores(own) + causal_s[...]
    m0 = jnp.max(sc, axis=0, keepdims=True)
    acc0 = weighted_values(i, jnp.exp(sc - m0))

    def block_step(j, carry):
        m, acc = carry
        start = pl.multiple_of(j * bs, bs)
        again = pl.multiple_of(start + jnp.minimum(i, 0) * bs, bs)
        m_new = jnp.maximum(m, jnp.max(scores(start), axis=0, keepdims=True))
        alpha = jnp.exp(m - m_new)
        return m_new, alpha * acc + weighted_values(j, jnp.exp(scores(again) - m_new))

    _, acc = lax.fori_loop(0, i, block_step, (m0, acc0))
    l = acc[HEAD_DIM:HEAD_DIM + 1]
    acc = acc[:HEAD_DIM]
    o_t = jnp.concatenate([acc[:, h * bs:(h + 1) * bs] / l[:, h * bs:(h + 1) * bs] for h in range(N_HEADS)], axis=0)
    o_ref[0] = o_t.T


def _moba(p, q_norm, k_norm):
    b, t, cols = p.shape
    bs = MOBA_BLOCK
    n_pad = -(-(t // bs) // 8) * 8
    assert t % bs == 0 and cols == MOBA_COLS and n_pad + 2 <= HEAD_DIM and N_HEADS * n_pad <= LANES and bs <= 256
    row = lambda v: v.reshape(1, -1).astype(F32)
    const = lambda i, j: (0, 0)
    return pl.pallas_call(
        _moba_kernel,
        grid=(b, t // bs),
        in_specs=[pl.BlockSpec((1, bs, WIDTH), lambda i, j: (i, j, 0)),
                  pl.BlockSpec((1, t, WIDTH), lambda i, j: (i, 0, 1)),
                  pl.BlockSpec((1, t, WIDTH), lambda i, j: (i, 0, 2)),
                  pl.BlockSpec((1, HEAD_DIM), const), pl.BlockSpec((1, HEAD_DIM), const)],
        out_specs=pl.BlockSpec((1, bs, WIDTH), lambda i, j: (i, j, 0)),
        out_shape=jax.ShapeDtypeStruct((b, t, WIDTH), F32),
        scratch_shapes=[pltpu.VMEM((t, 2 * WIDTH), MXU_DTYPE),
                        pltpu.VMEM((t // bs, N_HEADS * VAL_ROWS, bs), MXU_DTYPE),
                        pltpu.VMEM((3 * WIDTH, LANES), MXU_DTYPE),
                        pltpu.VMEM((bs, N_HEADS * bs), F32)],
        compiler_params=pltpu.CompilerParams(dimension_semantics=("parallel", "arbitrary"),
                                             vmem_limit_bytes=VMEM_LIMIT),
        name="moba",
    )(p, p, p, row(q_norm), row(k_norm))


def _pad_cols(w, n):
    return jnp.pad(w, ((0, 0), (0, n - w.shape[1])))


def _even_layer(x, b, t, norm_g, norm2_g, mlp_w1, mlp_w2, w_in, w_out, mu, w0, w2, a0, a2, g2, k_k, k_a, r_k,
                lnx_w, lnx_b, conv_w, a_log, dt_bias, gdn_norm_w):
    qkv_w = 3 * WIDTH
    w_rwkv = w_in[:, :RWKV_COLS]
    w_g = w_in[:, RWKV_COLS:]
    w_gdn = jnp.concatenate([w_g[:, :qkv_w], w_g[:, qkv_w + 2 * N_HEADS:],
                             _pad_cols(w_g[:, qkv_w:qkv_w + 2 * N_HEADS], LANES)], axis=1)
    p_rwkv, p_gdn = _norm_proj(x, norm_g, [w_rwkv.astype(MXU_DTYPE), w_gdn.astype(MXU_DTYPE)])
    o_a = _rwkv(p_rwkv.reshape(b, t, -1), mu, w0, w2, a0, a2, g2, k_k, k_a, r_k, lnx_w, lnx_b)
    o_b = _gdn(p_gdn.reshape(b, t, -1), conv_w, a_log, dt_bias, gdn_norm_w)
    return _out_mlp(x, o_a.reshape(b * t, -1), o_b.reshape(b * t, -1), w_out, norm2_g, mlp_w1, mlp_w2)


def _odd_layer(x, b, t, norm_g, norm2_g, mlp_w1, mlp_w2, w_in, w_out, dsa_q_norm, dsa_k_norm, idx_ln_w, idx_ln_b,
               moba_q_norm, moba_k_norm):
    kvw = 2 * DSA_KV_HEADS * HEAD_DIM
    sizes = (WIDTH, kvw, IDX_HEADS * IDX_DIM, IDX_DIM, IDX_HEADS, WIDTH, WIDTH, WIDTH)
    offs = [0]
    for s in sizes:
        offs.append(offs[-1] + s)
    dq, dkv, iq, ik, iw, mq, mk, mv = (w_in[:, offs[n]:offs[n + 1]] for n in range(len(sizes)))
    w_dsa = jnp.concatenate([dq, iq, dkv, _pad_cols(jnp.concatenate([ik, iw], axis=1), LANES)], axis=1)
    w_moba = jnp.concatenate([mq, mk, mv], axis=1)
    p_dsa, p_moba = _norm_proj(x, norm_g, [w_dsa.astype(MXU_DTYPE), w_moba.astype(MXU_DTYPE)])
    o_c = _dsa(p_dsa.reshape(b, t, -1), dsa_q_norm, dsa_k_norm, idx_ln_w, idx_ln_b)
    o_d = _moba(p_moba.reshape(b, t, -1), moba_q_norm, moba_k_norm)
    return _out_mlp(x, o_c.reshape(b * t, -1), o_d.reshape(b * t, -1), w_out, norm2_g, mlp_w1, mlp_w2)


def kernel(x, norm1_g, norm2_g, mlp_w1, mlp_w2, ev_w_in, ev_w_out, rwkv_mu, rwkv_w0, rwkv_w2, rwkv_a0, rwkv_a2, rwkv_g2, rwkv_k_k, rwkv_k_a, rwkv_r_k, rwkv_lnx_w, rwkv_lnx_b, gdn_conv_w, gdn_a_log, gdn_dt_bias, gdn_norm_w, od_w_in, od_w_out, dsa_q_norm, dsa_k_norm, idx_k_ln_w, idx_k_ln_b, moba_q_norm, moba_k_norm):
    b, t, d = x.shape
    depth = norm1_g.shape[0]
    h = x.reshape(b * t, d)
    for i in range(depth):
        j = i // 2
        if i % 2 == 0:
            h = _even_layer(h, b, t, norm1_g[i], norm2_g[i], mlp_w1[i], mlp_w2[i], ev_w_in[j], ev_w_out[j],
                            rwkv_mu[j], rwkv_w0[j], rwkv_w2[j], rwkv_a0[j], rwkv_a2[j], rwkv_g2[j], rwkv_k_k[j], rwkv_k_a[j], rwkv_r_k[j],
                            rwkv_lnx_w[j], rwkv_lnx_b[j], gdn_conv_w[j], gdn_a_log[j], gdn_dt_bias[j],
                            gdn_norm_w[j])
        else:
            h = _odd_layer(h, b, t, norm1_g[i], norm2_g[i], mlp_w1[i], mlp_w2[i], od_w_in[j], od_w_out[j],
                           dsa_q_norm[j], dsa_k_norm[j], idx_k_ln_w[j], idx_k_ln_b[j], moba_q_norm[j], moba_k_norm[j])
    return h.reshape(b, t, d)
```
